```python
import math
import jax, jax.numpy as jnp
from jax import lax
import numpy as np

D_MODEL = 2048
BATCH = 4
SEQ = 4096
DEPTH = 1
DEC_BATCH = 1
DEC_SEQ = 16384
PAST_LEN = 128

D_INNER = 2 * D_MODEL
SSD_HEADDIM = 64
SSD_HEADS = D_INNER // SSD_HEADDIM
SSD_GROUPS = 8
SSD_HPG = SSD_HEADS // SSD_GROUPS
D_STATE = 128
CONV_K = 5
CHUNK = 128
CONV_DIM = D_INNER + 2 * SSD_GROUPS * D_STATE
ATT_HEAD_DIM = 128
ATT_Q_HEADS = D_MODEL // ATT_HEAD_DIM
ATT_KV_HEADS = 4
ATT_REP = ATT_Q_HEADS // ATT_KV_HEADS
WINDOW = 128
ATT_BLOCK = 128
N_BUCKETS = 32
MAX_DIST = 128
D_FF = ((8 * D_MODEL // 3 + 255) // 256) * 256
EPS = 1e-6

IN_SIZES = (
    D_INNER,
    CONV_DIM,
    2 * SSD_HEADS,
    ATT_Q_HEADS * ATT_HEAD_DIM,
    ATT_KV_HEADS * ATT_HEAD_DIM,
    ATT_KV_HEADS * ATT_HEAD_DIM,
    D_MODEL,
    D_MODEL,
)
IN_TOTAL = sum(IN_SIZES)
IN_SPLITS = tuple(int(v) for v in np.cumsum(IN_SIZES)[:-1])

kernel_name = "hybrid_ssd_bandgqa_macaron_encoder"


def rmsnorm(x, g):
    xf = x.astype(jnp.float32)
    out = xf * lax.rsqrt(jnp.mean(xf * xf, axis=-1, keepdims=True) + EPS) * g.astype(jnp.float32)
    return out.astype(x.dtype)


def swiglu(x, w_gate, w_up, w_down):
    return (jax.nn.silu(x @ w_gate) * (x @ w_up)) @ w_down


def centred_dwconv(u, w, b):
    out = lax.conv_general_dilated(
        u, w[:, None, :].astype(u.dtype), window_strides=(1,),
        padding=[(CONV_K // 2, CONV_K // 2)],
        dimension_numbers=("NWC", "WIO", "NWC"),
        feature_group_count=u.shape[-1])
    return out + b.astype(u.dtype)


def ssd_chunked(X, dA, B, C):
    b, s, g, r, p = X.shape
    c = s // CHUNK
    Xc = X.reshape(b, c, CHUNK, g, r, p)
    Bc = B.reshape(b, c, CHUNK, g, D_STATE)
    Cc = C.reshape(b, c, CHUNK, g, D_STATE)
    A = dA.reshape(b, c, CHUNK, g, r).transpose(0, 3, 4, 1, 2)
    Acum = jnp.cumsum(A, axis=-1)
    tril = np.tril(np.ones((CHUNK, CHUNK), dtype=bool))
    Lmat = jnp.exp(jnp.where(tril, Acum[..., :, None] - Acum[..., None, :], -jnp.inf))
    CB = jnp.einsum("bclgn,bcsgn->bcgls", Cc, Bc)
    y_diag = jnp.einsum("bcgls,bgrcls,bcsgrp->bclgrp", CB, Lmat, Xc)
    decay_states = jnp.exp(Acum[..., -1:] - Acum)
    states = jnp.einsum("bclgn,bgrcl,bclgrp->bcgrpn", Bc, decay_states, Xc)
    chunk_decay = jnp.exp(Acum[..., -1])

    def step(h, inp):
        st, dec = inp
        return h * dec[..., None, None] + st, h

    _, prev = lax.scan(step, jnp.zeros_like(states[:, 0]),
                       (jnp.moveaxis(states, 1, 0), jnp.moveaxis(chunk_decay, 3, 0)))
    prev = jnp.moveaxis(prev, 0, 1)
    y_off = jnp.einsum("bclgn,bcgrpn,bgrcl->bclgrp", Cc, prev, jnp.exp(Acum))
    return (y_diag + y_off).reshape(b, s, g, r, p)


def ssd_mixer(z, xbc, dt_raw, conv_w, conv_b, A_log, dt_bias, D_skip, norm_g):
    b, s, _ = z.shape
    xbc = jax.nn.silu(centred_dwconv(xbc, conv_w, conv_b)).astype(jnp.float32)
    xs = xbc[..., :D_INNER].reshape(b, s, SSD_GROUPS, SSD_HPG, SSD_HEADDIM)
    Bm = xbc[..., D_INNER:D_INNER + SSD_GROUPS * D_STATE].reshape(b, s, SSD_GROUPS, D_STATE)
    Cm = xbc[..., D_INNER + SSD_GROUPS * D_STATE:].reshape(b, s, SSD_GROUPS, D_STATE)
    dt = jax.nn.softplus(dt_raw.astype(jnp.float32).reshape(b, s, 2, SSD_GROUPS, SSD_HPG)
                         + dt_bias.astype(jnp.float32).reshape(2, SSD_GROUPS, SSD_HPG))
    A = -jnp.exp(A_log.astype(jnp.float32)).reshape(2, SSD_GROUPS, SSD_HPG)
    dt_f, dt_b = dt[:, :, 0], dt[:, :, 1]
    flip = lambda t: jnp.flip(t, axis=1)
    y_f = ssd_chunked(xs * dt_f[..., None], dt_f * A[0], Bm, Cm)
    y_b = flip(ssd_chunked(flip(xs * dt_b[..., None]), flip(dt_b * A[1]), flip(Bm), flip(Cm)))
    y = y_f + y_b + D_skip.astype(jnp.float32).reshape(SSD_GROUPS, SSD_HPG)[..., None] * xs
    y = y.reshape(b, s, D_INNER) * jax.nn.silu(z.astype(jnp.float32))
    return rmsnorm(y, norm_g).astype(z.dtype)


def _t5_bucket(rel):
    nb = N_BUCKETS // 2
    ret = (rel > 0).astype(np.int32) * nb
    n = np.abs(rel)
    max_exact = nb // 2
    large = max_exact + (np.log(np.maximum(n, 1) / max_exact) / math.log(MAX_DIST / max_exact)
                         * (nb - max_exact)).astype(np.int32)
    large = np.minimum(large, nb - 1)
    return (ret + np.where(n < max_exact, n, large)).astype(np.int32)


def band_attention(q, k, v, q_gain, k_gain, sink, rel_bias):
    b, s = q.shape[:2]
    nblk = s // ATT_BLOCK
    q = rmsnorm(q, q_gain) * (ATT_HEAD_DIM ** -0.5)
    k = rmsnorm(k, k_gain)
    qb = q.reshape(b, nblk, ATT_BLOCK, ATT_KV_HEADS, ATT_REP, ATT_HEAD_DIM)

    def windows(t):
        tp = jnp.pad(t, ((0, 0), (ATT_BLOCK, ATT_BLOCK), (0, 0), (0, 0)))
        tb = tp.reshape(b, nblk + 2, ATT_BLOCK, ATT_KV_HEADS, ATT_HEAD_DIM)
        return jnp.concatenate([tb[:, :-2], tb[:, 1:-1], tb[:, 2:]], axis=2)

    kw, vw = windows(k), windows(v)
    i = np.arange(ATT_BLOCK)[:, None]
    j = np.arange(3 * ATT_BLOCK)[None, :]
    rel = j - ATT_BLOCK - i
    bias = rel_bias.astype(jnp.float32)[_t5_bucket(rel)]
    bias = jnp.transpose(bias, (2, 0, 1)).reshape(ATT_KV_HEADS, ATT_REP, ATT_BLOCK, 3 * ATT_BLOCK)
    kpos = np.arange(nblk)[:, None] * ATT_BLOCK - ATT_BLOCK + np.arange(3 * ATT_BLOCK)[None, :]
    valid = (np.abs(rel) <= WINDOW)[None] & ((kpos >= 0) & (kpos < s))[:, None, :]

    scores = jnp.einsum("bnqgrd,bnkgd->bngrqk", qb, kw).astype(jnp.float32) + bias
    scores = jnp.where(valid[None, :, None, None], scores, -jnp.inf)
    sinkb = sink.astype(jnp.float32).reshape(ATT_KV_HEADS, ATT_REP)[None, None, :, :, None, None]
    m = jnp.maximum(scores.max(axis=-1, keepdims=True), sinkb)
    p = jnp.exp(scores - m)
    probs = p / (p.sum(axis=-1, keepdims=True) + jnp.exp(sinkb - m))
    out = jnp.einsum("bngrqk,bnkgd->bnqgrd", probs.astype(v.dtype), vw)
    return out.reshape(b, s, ATT_Q_HEADS * ATT_HEAD_DIM)


def forward(x, ffn1_norm, ffn1_w_gate, ffn1_w_up, ffn1_w_down, mix_norm, w_in,
            conv_w, conv_b, ssd_A_log, ssd_dt_bias, ssd_D, ssd_out_norm,
            q_norm, k_norm, attn_sink, rel_bias, w_branch_ssd, w_branch_attn, w_out,
            ffn2_norm, ffn2_w_gate, ffn2_w_up, ffn2_w_down, final_norm):
    b, s, _ = x.shape
    for l in range(DEPTH):
        x = x + 0.5 * swiglu(rmsnorm(x, ffn1_norm[l]), ffn1_w_gate[l], ffn1_w_up[l], ffn1_w_down[l])
        h = rmsnorm(x, mix_norm[l])
        u = h @ w_in[l]
        z, xbc, dt_raw, q, k, v, g_a, g_b = jnp.split(u, IN_SPLITS, axis=-1)
        y_ssd = ssd_mixer(z, xbc, dt_raw, conv_w[l], conv_b[l], ssd_A_log[l], ssd_dt_bias[l],
                          ssd_D[l], ssd_out_norm[l])
        y_att = band_attention(q.reshape(b, s, ATT_Q_HEADS, ATT_HEAD_DIM),
                               k.reshape(b, s, ATT_KV_HEADS, ATT_HEAD_DIM),
                               v.reshape(b, s, ATT_KV_HEADS, ATT_HEAD_DIM),
                               q_norm[l], k_norm[l], attn_sink[l], rel_bias)
        merged = (jax.nn.sigmoid(g_a) * (y_ssd @ w_branch_ssd[l])
                  + jax.nn.sigmoid(g_b) * (y_att @ w_branch_attn[l]))
        x = x + merged @ w_out[l]
        x = x + 0.5 * swiglu(rmsnorm(x, ffn2_norm[l]), ffn2_w_gate[l], ffn2_w_up[l], ffn2_w_down[l])
        x = rmsnorm(x, final_norm[l])
    return x


def setup_inputs(seed: int = 0) -> dict:
    key = jax.random.key(seed)
    ks = iter(jax.random.split(key, 32))
    f32 = jnp.float32

    def dense(shape, fan_in):
        return jax.random.normal(next(ks), shape, f32) * (fan_in ** -0.5)

    def gain(shape):
        return 1.0 + 0.02 * jax.random.normal(next(ks), shape, f32)

    L = DEPTH
    x_prompt = jax.random.normal(next(ks), (BATCH, SEQ, D_MODEL), f32)
    x_sample = jax.random.normal(next(ks), (DEC_BATCH, DEC_SEQ, D_MODEL), f32)
    dt0 = jnp.exp(jax.random.uniform(next(ks), (L, 2, SSD_HEADS), f32,
                                     math.log(1e-3), math.log(1e-1)))
    return {
        "x_prompt": x_prompt,
        "x_sample": x_sample,
        "ffn1_norm": gain((L, D_MODEL)),
        "ffn1_w_gate": dense((L, D_MODEL, D_FF), D_MODEL),
        "ffn1_w_up": dense((L, D_MODEL, D_FF), D_MODEL),
        "ffn1_w_down": dense((L, D_FF, D_MODEL), D_FF),
        "mix_norm": gain((L, D_MODEL)),
        "w_in": dense((L, D_MODEL, IN_TOTAL), D_MODEL),
        "conv_w": dense((L, CONV_K, CONV_DIM), CONV_K),
        "conv_b": 0.02 * jax.random.normal(next(ks), (L, CONV_DIM), f32),
        "ssd_A_log": jnp.log(jax.random.uniform(next(ks), (L, 2, SSD_HEADS), f32, 1.0, 16.0)),
        "ssd_dt_bias": dt0 + jnp.log(-jnp.expm1(-dt0)),
        "ssd_D": gain((L, SSD_HEADS)),
        "ssd_out_norm": gain((L, D_INNER)),
        "q_norm": gain((L, ATT_HEAD_DIM)),
        "k_norm": gain((L, ATT_HEAD_DIM)),
        "attn_sink": 0.5 * jax.random.normal(next(ks), (L, ATT_Q_HEADS), f32),
        "rel_bias": 0.3 * jax.random.normal(next(ks), (N_BUCKETS, ATT_Q_HEADS), f32),
        "w_branch_ssd": dense((L, D_INNER, D_MODEL), D_INNER),
        "w_branch_attn": dense((L, ATT_Q_HEADS * ATT_HEAD_DIM, D_MODEL), ATT_Q_HEADS * ATT_HEAD_DIM),
        "w_out": dense((L, D_MODEL, D_MODEL), D_MODEL),
        "ffn2_norm": gain((L, D_MODEL)),
        "ffn2_w_gate": dense((L, D_MODEL, D_FF), D_MODEL),
        "ffn2_w_up": dense((L, D_MODEL, D_FF), D_MODEL),
        "ffn2_w_down": dense((L, D_FF, D_MODEL), D_FF),
        "final_norm": gain((L, D_MODEL)),
    }


def reference(x_prompt, x_sample, ffn1_norm, ffn1_w_gate, ffn1_w_up, ffn1_w_down, mix_norm, w_in,
              conv_w, conv_b, ssd_A_log, ssd_dt_bias, ssd_D, ssd_out_norm,
              q_norm, k_norm, attn_sink, rel_bias, w_branch_ssd, w_branch_attn, w_out,
              ffn2_norm, ffn2_w_gate, ffn2_w_up, ffn2_w_down, final_norm):
    y_prompt = forward(x_prompt, ffn1_norm, ffn1_w_gate, ffn1_w_up, ffn1_w_down, mix_norm, w_in,
                       conv_w, conv_b, ssd_A_log, ssd_dt_bias, ssd_D, ssd_out_norm,
                       q_norm, k_norm, attn_sink, rel_bias, w_branch_ssd, w_branch_attn, w_out,
                       ffn2_norm, ffn2_w_gate, ffn2_w_up, ffn2_w_down, final_norm)
    y_sample = forward(x_sample, ffn1_norm, ffn1_w_gate, ffn1_w_up, ffn1_w_down, mix_norm, w_in,
                       conv_w, conv_b, ssd_A_log, ssd_dt_bias, ssd_D, ssd_out_norm,
                       q_norm, k_norm, attn_sink, rel_bias, w_branch_ssd, w_branch_attn, w_out,
                       ffn2_norm, ffn2_w_gate, ffn2_w_up, ffn2_w_down, final_norm)
    return (y_prompt, y_sample)
```

```python
import functools
import math

import jax
import jax.numpy as jnp
import numpy as np
from jax import lax
from jax.experimental import pallas as pl
from jax.experimental.pallas import tpu as pltpu

D_MODEL = 2048
D_INNER = 2 * D_MODEL
SSD_HEADDIM = 64
SSD_HEADS = D_INNER // SSD_HEADDIM
SSD_GROUPS = 8
SSD_HPG = SSD_HEADS // SSD_GROUPS
D_STATE = 128
CONV_K = 5
CHUNK = 128
CONV_DIM = D_INNER + 2 * SSD_GROUPS * D_STATE
ATT_HEAD_DIM = 128
ATT_Q_HEADS = D_MODEL // ATT_HEAD_DIM
ATT_KV_HEADS = 4
ATT_REP = ATT_Q_HEADS // ATT_KV_HEADS
WINDOW = 128
ATT_BLOCK = 128
N_BUCKETS = 32
MAX_DIST = 128
D_FF = ((8 * D_MODEL // 3 + 255) // 256) * 256
EPS = 1e-6
GROUP_X = SSD_HPG * SSD_HEADDIM
KV_DIM = ATT_KV_HEADS * ATT_HEAD_DIM

U_Z = 0
U_XBC = U_Z + D_INNER
U_Q = U_XBC + CONV_DIM
U_GA = U_Q + D_MODEL
U_GB = U_GA + D_MODEL
U_K = U_GB + D_MODEL
U_V = U_K + KV_DIM
U_TOTAL = U_V + KV_DIM

V7X_VMEM_LIMIT_BYTES = 56 * 1024 * 1024
NEG_BIG = -1e30

BF16 = jnp.bfloat16
F32 = jnp.float32


def _params(semantics):
    return pltpu.CompilerParams(dimension_semantics=semantics,
                                vmem_limit_bytes=V7X_VMEM_LIMIT_BYTES)


def _rms(x, gain):
    return x * lax.rsqrt(jnp.mean(x * x, axis=-1, keepdims=True) + EPS) * gain


def _silu(x):
    return x * jax.nn.sigmoid(x)


def _dot(a, b):
    return jnp.dot(a, b, preferred_element_type=F32)


FFN_TM = 512
FFN_TF = 512


def _ffn_kernel(x_ref, g_ref, wg_ref, wu_ref, wd_ref, pg_ref, *refs, final):
    if final:
        y_ref, h_scr, acc_scr = refs
    else:
        x1_ref, h_ref, h_scr, acc_scr = refs
    j = pl.program_id(1)

    @pl.when(j == 0)
    def _():
        h_scr[...] = _rms(x_ref[...], g_ref[...]).astype(BF16)
        acc_scr[...] = jnp.zeros_like(acc_scr)

    h = h_scr[...]
    t = _silu(_dot(h, wg_ref[...])) * _dot(h, wu_ref[...])
    acc_scr[...] += _dot(t.astype(BF16), wd_ref[...])

    @pl.when(j == pl.num_programs(1) - 1)
    def _():
        x1 = x_ref[...] + 0.5 * acc_scr[...]
        normed = _rms(x1, pg_ref[...])
        if final:
            y_ref[...] = normed
        else:
            x1_ref[...] = x1
            h_ref[...] = normed.astype(BF16)


def _ffn(x, gain, wg, wu, wd, post_gain, final):
    n = x.shape[0]
    row = lambda i, j: (i, 0)
    fixed = lambda i, j: (0, 0)
    out_shape = [jax.ShapeDtypeStruct((n, D_MODEL), F32)]
    out_specs = [pl.BlockSpec((FFN_TM, D_MODEL), row)]
    if not final:
        out_shape.append(jax.ShapeDtypeStruct((n, D_MODEL), BF16))
        out_specs.append(pl.BlockSpec((FFN_TM, D_MODEL), row))
    return pl.pallas_call(
        functools.partial(_ffn_kernel, final=final),
        grid=(n // FFN_TM, D_FF // FFN_TF),
        in_specs=[
            pl.BlockSpec((FFN_TM, D_MODEL), row),
            pl.BlockSpec((1, D_MODEL), fixed),
            pl.BlockSpec((D_MODEL, FFN_TF), lambda i, j: (0, j)),
            pl.BlockSpec((D_MODEL, FFN_TF), lambda i, j: (0, j)),
            pl.BlockSpec((FFN_TF, D_MODEL), lambda i, j: (j, 0)),
            pl.BlockSpec((1, D_MODEL), fixed),
        ],
        out_specs=out_specs,
        out_shape=out_shape,
        scratch_shapes=[pltpu.VMEM((FFN_TM, D_MODEL), BF16), pltpu.VMEM((FFN_TM, D_MODEL), F32)],
        compiler_params=_params(("parallel", "arbitrary")),
        name="ffn_final" if final else "ffn",
    )(x, gain, wg, wu, wd, post_gain)


INP_TM = 1024
INP_TN = 1024


def _inproj_kernel(h_ref, w_ref, wdt_ref, bdt_ref, u_ref, dt_ref):
    h = h_ref[...]
    u_ref[...] = _dot(h, w_ref[...]).astype(BF16)

    @pl.when(pl.program_id(1) == 0)
    def _():
        raw = _dot(h, wdt_ref[...]) + bdt_ref[...]
        dt_ref[...] = jnp.maximum(raw, 0.0) + jnp.log1p(jnp.exp(-jnp.abs(raw)))


def _inproj(h, w_u, w_dt, dt_bias):
    n = h.shape[0]
    ndt = 2 * SSD_HEADS
    return pl.pallas_call(
        _inproj_kernel,
        grid=(n // INP_TM, U_TOTAL // INP_TN),
        in_specs=[
            pl.BlockSpec((INP_TM, D_MODEL), lambda i, j: (i, 0)),
            pl.BlockSpec((D_MODEL, INP_TN), lambda i, j: (0, j)),
            pl.BlockSpec((D_MODEL, ndt), lambda i, j: (0, 0)),
            pl.BlockSpec((1, ndt), lambda i, j: (0, 0)),
        ],
        out_specs=[
            pl.BlockSpec((INP_TM, INP_TN), lambda i, j: (i, j)),
            pl.BlockSpec((INP_TM, ndt), lambda i, j: (i, 0)),
        ],
        out_shape=[jax.ShapeDtypeStruct((n, U_TOTAL), BF16), jax.ShapeDtypeStruct((n, ndt), F32)],
        compiler_params=_params(("parallel", "arbitrary")),
        name="in_proj",
    )(h, w_u, w_dt, dt_bias)


CONV_TS = 512
CONV_TC = 512
CONV_HALO = 16
CONV_PAD = 8


def _conv_kernel(xm_ref, xp_ref, xn_ref, w_ref, b_ref, o_ref, scr, *, tiles_per_seq):
    i = pl.program_id(0)
    first = (i % tiles_per_seq) == 0
    last = ((i + 1) % tiles_per_seq) == 0
    scr[CONV_PAD:CONV_PAD + CONV_TS, :] = xm_ref[...].astype(F32)
    left = xp_ref[...].astype(F32)[CONV_HALO - CONV_PAD:, :]
    right = xn_ref[...].astype(F32)[:CONV_PAD, :]
    scr[0:CONV_PAD, :] = jnp.where(first, 0.0, left)
    scr[CONV_PAD + CONV_TS:, :] = jnp.where(last, 0.0, right)
    acc = jnp.broadcast_to(b_ref[...], (CONV_TS, CONV_TC))
    base = CONV_PAD - CONV_K // 2
    for k in range(CONV_K):
        acc = acc + scr[base + k:base + k + CONV_TS, :] * w_ref[k:k + 1, :]
    o_ref[...] = _silu(acc).astype(BF16)


def _conv(u, conv_w, conv_b, seq):
    n = u.shape[0]
    tiles_per_seq = seq // CONV_TS
    halo_per_tile = CONV_TS // CONV_HALO
    n_halo = n // CONV_HALO
    col0 = U_XBC // CONV_TC
    return pl.pallas_call(
        functools.partial(_conv_kernel, tiles_per_seq=tiles_per_seq),
        grid=(n // CONV_TS, CONV_DIM // CONV_TC),
        in_specs=[
            pl.BlockSpec((CONV_TS, CONV_TC), lambda i, j: (i, col0 + j)),
            pl.BlockSpec((CONV_HALO, CONV_TC),
                         lambda i, j: (jnp.maximum(i * halo_per_tile - 1, 0), col0 + j)),
            pl.BlockSpec((CONV_HALO, CONV_TC),
                         lambda i, j: (jnp.minimum((i + 1) * halo_per_tile, n_halo - 1), col0 + j)),
            pl.BlockSpec((CONV_K, CONV_TC), lambda i, j: (0, j)),
            pl.BlockSpec((1, CONV_TC), lambda i, j: (0, j)),
        ],
        out_specs=pl.BlockSpec((CONV_TS, CONV_TC), lambda i, j: (i, j)),
        out_shape=jax.ShapeDtypeStruct((n, CONV_DIM), BF16),
        scratch_shapes=[pltpu.VMEM((CONV_TS + 2 * CONV_PAD, CONV_TC), F32)],
        compiler_params=_params(("parallel", "parallel")),
        name="conv",
    )(u, u, u, conv_w, conv_b)


def _split3(a):
    hi = a.astype(BF16)
    r1 = a - hi.astype(F32)
    mid = r1.astype(BF16)
    lo = (r1 - mid.astype(F32)).astype(BF16)
    return hi, mid, lo


def _exact_lhs01(m01, a):
    hi, mid, lo = _split3(a)
    return _dot(jnp.concatenate([m01, m01, m01], axis=1), jnp.concatenate([hi, mid, lo], axis=0))


def _exact_rhs01(a, m01):
    hi, mid, lo = _split3(a)
    return _dot(jnp.concatenate([hi, mid, lo], axis=1), jnp.concatenate([m01, m01, m01], axis=0))


def _ssd_kernel(x_ref, b_ref, c_ref, z_ref, dt_ref, a_ref, d_ref, e_ref, y_ref,
                gst_scr, h_scr, cumt_scr, dtt_scr):
    g = pl.program_id(1)
    phase = pl.program_id(2)
    step = pl.program_id(3)
    nc = pl.num_programs(3)
    L = CHUNK
    GX = GROUP_X

    @pl.when(step == 0)
    def _():
        h_scr[...] = jnp.zeros_like(h_scr)

    row = lax.broadcasted_iota(jnp.int32, (L, L), 0)
    col = lax.broadcasted_iota(jnp.int32, (L, L), 1)
    lower = row >= col
    upper = row <= col
    is_fwd_col = col < SSD_HEADS

    dt = dt_ref[...]
    dA = dt * (-jnp.exp(a_ref[...]))
    pre = _exact_lhs01(lower.astype(BF16), dA)
    suf = _exact_lhs01(upper.astype(BF16), dA)
    cum = jnp.where(is_fwd_col, pre, suf)
    tot = jnp.where(is_fwd_col[:1], pre[L - 1:L, :], suf[0:1, :])

    x = x_ref[...].astype(F32)
    bm = b_ref[...]

    def state_update(e01):
        w = _exact_rhs01(jnp.exp(tot - cum) * dt, e01)
        dec = _exact_rhs01(jnp.broadcast_to(jnp.exp(tot), (8, L)), e01)[0:1, :]
        xdec = (w * x).astype(BF16)
        s_new = lax.dot_general(bm, xdec, (((0,), (0,)), ((), ())), preferred_element_type=F32)
        h_scr[...] = dec * h_scr[...] + s_new

    @pl.when(phase == 0)
    def _():
        gst_scr[nc - 1 - step] = h_scr[...].astype(BF16)
        state_update(e_ref[0, :, GX:])

    @pl.when(phase == 1)
    def _():
        c = step
        cm = c_ref[...]
        cb = lax.dot_general(cm, bm, (((1,), (1,)), ((), ())), preferred_element_type=F32)
        e01 = e_ref[0]
        cum_e = _exact_rhs01(cum, e01)
        cumt_scr[...] = cum.T
        dtt_scr[...] = dt.T
        xb = x_ref[...]
        lane = lax.broadcasted_iota(jnp.int32, (L, L), 1)
        pairs = []
        for k in range(SSD_HPG // 2):
            xp = xb[:, k * L:(k + 1) * L]
            acc = None
            for half in range(2):
                hh = 2 * k + half
                pc = cum_e[:, hh * SSD_HEADDIM:(hh + 1) * SSD_HEADDIM]
                pc = jnp.concatenate([pc, pc], axis=1)
                rc = cum_e[:, GX + hh * SSD_HEADDIM:GX + (hh + 1) * SSD_HEADDIM]
                rc = jnp.concatenate([rc, rc], axis=1)
                pr = cumt_scr[pl.ds(g * SSD_HPG + hh, 1), :]
                rr = cumt_scr[pl.ds(SSD_HEADS + g * SSD_HPG + hh, 1), :]
                dtf = dtt_scr[pl.ds(g * SSD_HPG + hh, 1), :]
                dtb = dtt_scr[pl.ds(SSD_HEADS + g * SSD_HPG + hh, 1), :]
                lf = jnp.exp(jnp.where(lower, pc - pr, NEG_BIG)) * dtf
                lb = jnp.exp(jnp.where(upper, rc - rr, NEG_BIG)) * dtb
                m = (cb * (lf + lb)).astype(BF16)
                keep = (lane < SSD_HEADDIM) if half == 0 else (lane >= SSD_HEADDIM)
                part = _dot(m, jnp.where(keep, xp, jnp.zeros_like(xp)))
                acc = part if acc is None else acc + part
            pairs.append(acc)
        y = jnp.concatenate(pairs, axis=1)
        y = y + jnp.exp(cum_e[:, :GX]) * _dot(cm, h_scr[...].astype(BF16))
        y = y + jnp.exp(cum_e[:, GX:]) * _dot(cm, gst_scr[c])
        y = y + d_ref[...] * x
        y_ref[...] = (y * _silu(z_ref[...].astype(F32))).astype(BF16)
        state_update(e_ref[0, :, :GX])


def _ssd(xc, u, dt, a_log, d_exp, e01, batch, seq):
    n = xc.shape[0]
    nc = seq // CHUNK
    gx_blocks = D_INNER // D_STATE

    def fb_row(b, p, s):
        return b * nc + jnp.where(p == 0, nc - 1 - s, s)

    def f_row(b, p, s):
        return b * nc + jnp.where(p == 0, 0, s)

    return pl.pallas_call(
        _ssd_kernel,
        grid=(batch, SSD_GROUPS, 2, nc),
        in_specs=[
            pl.BlockSpec((CHUNK, GROUP_X), lambda b, g, p, s: (fb_row(b, p, s), g)),
            pl.BlockSpec((CHUNK, D_STATE), lambda b, g, p, s: (fb_row(b, p, s), gx_blocks + g)),
            pl.BlockSpec((CHUNK, D_STATE),
                         lambda b, g, p, s: (f_row(b, p, s), gx_blocks + SSD_GROUPS + g)),
            pl.BlockSpec((CHUNK, GROUP_X), lambda b, g, p, s: (f_row(b, p, s), U_Z // GROUP_X + g)),
            pl.BlockSpec((CHUNK, 2 * SSD_HEADS), lambda b, g, p, s: (fb_row(b, p, s), 0)),
            pl.BlockSpec((1, 2 * SSD_HEADS), lambda b, g, p, s: (0, 0)),
            pl.BlockSpec((1, GROUP_X), lambda b, g, p, s: (0, g)),
            pl.BlockSpec((1, 2 * SSD_HEADS, 2 * GROUP_X), lambda b, g, p, s: (g, 0, 0)),
        ],
        out_specs=pl.BlockSpec((CHUNK, GROUP_X), lambda b, g, p, s: (f_row(b, p, s), g)),
        out_shape=jax.ShapeDtypeStruct((n, D_INNER), BF16),
        scratch_shapes=[
            pltpu.VMEM((nc, D_STATE, GROUP_X), BF16),
            pltpu.VMEM((D_STATE, GROUP_X), F32),
            pltpu.VMEM((2 * SSD_HEADS, CHUNK), F32),
            pltpu.VMEM((2 * SSD_HEADS, CHUNK), F32),
        ],
        compiler_params=_params(("arbitrary", "arbitrary", "arbitrary", "arbitrary")),
        name="ssd",
    )(xc, xc, xc, u, dt, a_log, d_exp, e01)


def _ssd_expand_matrix():
    e = np.zeros((SSD_GROUPS, 2 * SSD_HEADS, 2 * GROUP_X), np.float32)
    for g in range(SSD_GROUPS):
        for d in range(2):
            for r in range(SSD_HPG):
                lo = d * GROUP_X + r * SSD_HEADDIM
                e[g, d * SSD_HEADS + g * SSD_HPG + r, lo:lo + SSD_HEADDIM] = 1.0
    return e


def _t5_bucket(rel):
    nb = N_BUCKETS // 2
    ret = (rel > 0).astype(np.int32) * nb
    n = np.abs(rel)
    max_exact = nb // 2
    large = max_exact + (np.log(np.maximum(n, 1) / max_exact) / math.log(MAX_DIST / max_exact)
                         * (nb - max_exact)).astype(np.int32)
    large = np.minimum(large, nb - 1)
    return (ret + np.where(n < max_exact, n, large)).astype(np.int32)


def _bucket_table():
    i = np.arange(ATT_BLOCK)[:, None]
    j = np.arange(3 * ATT_BLOCK)[None, :]
    rel = j - ATT_BLOCK - i
    return np.where(np.abs(rel) <= WINDOW, _t5_bucket(rel), -1).astype(np.int32)


def _bias_kernel(rel_ref, bucket_ref, o_ref):
    h = pl.program_id(0)
    bucket = bucket_ref[...]
    acc = jnp.full(bucket.shape, NEG_BIG, F32)
    for b in range(N_BUCKETS):
        acc = jnp.where(bucket == b, rel_ref[b, h], acc)
    o_ref[0] = acc


def _bias_table(rel_bias):
    bucket = jnp.asarray(_bucket_table())
    return pl.pallas_call(
        _bias_kernel,
        grid=(ATT_Q_HEADS,),
        in_specs=[
            pl.BlockSpec(memory_space=pltpu.SMEM),
            pl.BlockSpec((ATT_BLOCK, 3 * ATT_BLOCK), lambda h: (0, 0)),
        ],
        out_specs=pl.BlockSpec((1, ATT_BLOCK, 3 * ATT_BLOCK), lambda h: (h, 0, 0)),
        out_shape=jax.ShapeDtypeStruct((ATT_Q_HEADS, ATT_BLOCK, 3 * ATT_BLOCK), F32),
        compiler_params=_params(("arbitrary",)),
        name="rel_bias_table",
    )(rel_bias, bucket)


def _attn_kernel(q_ref, kp_ref, kc_ref, kn_ref, vp_ref, vc_ref, vn_ref, bias_ref,
                 qg_ref, kg_ref, sink_ref, o_ref, *, blocks_per_seq):
    i = pl.program_id(0)
    first = (i % blocks_per_seq) == 0
    last = ((i + 1) % blocks_per_seq) == 0
    T = ATT_BLOCK
    hd = ATT_HEAD_DIM
    kcol = lax.broadcasted_iota(jnp.int32, (1, 3 * T), 1)
    dead = (first & (kcol < T)) | (last & (kcol >= 2 * T))
    edge = jnp.where(dead, NEG_BIG, 0.0)
    scale = hd ** -0.5
    for g in range(ATT_KV_HEADS):
        ks = slice(g * hd, (g + 1) * hd)
        kwin = jnp.concatenate([kp_ref[:, ks], kc_ref[:, ks], kn_ref[:, ks]], axis=0).astype(F32)
        kwin = _rms(kwin, kg_ref[...]).astype(BF16)
        vwin = jnp.concatenate([vp_ref[:, ks], vc_ref[:, ks], vn_ref[:, ks]], axis=0)
        qs = jnp.concatenate(
            [q_ref[:, (g * ATT_REP + r) * hd:(g * ATT_REP + r + 1) * hd] for r in range(ATT_REP)],
            axis=0).astype(F32)
        qs = (_rms(qs, qg_ref[...]) * scale).astype(BF16)
        s = lax.dot_general(qs, kwin, (((1,), (1,)), ((), ())), preferred_element_type=F32)
        bias = jnp.concatenate([bias_ref[g * ATT_REP + r] for r in range(ATT_REP)], axis=0)
        s = s + bias + edge
        sink = jnp.concatenate(
            [jnp.full((T, 1), sink_ref[g * ATT_REP + r], F32) for r in range(ATT_REP)], axis=0)
        m = jnp.maximum(jnp.max(s, axis=-1, keepdims=True), sink)
        p = jnp.exp(s - m)
        denom = jnp.sum(p, axis=-1, keepdims=True) + jnp.exp(sink - m)
        o = _dot(p.astype(BF16), vwin) / denom
        for r in range(ATT_REP):
            hq = g * ATT_REP + r
            o_ref[:, hq * hd:(hq + 1) * hd] = o[r * T:(r + 1) * T, :].astype(BF16)


def _attn(u, bias, q_gain, k_gain, sink, seq):
    n = u.shape[0]
    nblk = n // ATT_BLOCK
    blocks_per_seq = seq // ATT_BLOCK
    kcol = U_K // KV_DIM
    vcol = U_V // KV_DIM
    prev = lambda i: jnp.maximum(i - 1, 0)
    nxt = lambda i: jnp.minimum(i + 1, nblk - 1)
    kv_spec = lambda rowf, c: pl.BlockSpec((ATT_BLOCK, KV_DIM), lambda i: (rowf(i), c))
    same = lambda i: i
    return pl.pallas_call(
        functools.partial(_attn_kernel, blocks_per_seq=blocks_per_seq),
        grid=(nblk,),
        in_specs=[
            pl.BlockSpec((ATT_BLOCK, D_MODEL), lambda i: (i, U_Q // D_MODEL)),
            kv_spec(prev, kcol), kv_spec(same, kcol), kv_spec(nxt, kcol),
            kv_spec(prev, vcol), kv_spec(same, vcol), kv_spec(nxt, vcol),
            pl.BlockSpec((ATT_Q_HEADS, ATT_BLOCK, 3 * ATT_BLOCK), lambda i: (0, 0, 0)),
            pl.BlockSpec((1, ATT_HEAD_DIM), lambda i: (0, 0)),
            pl.BlockSpec((1, ATT_HEAD_DIM), lambda i: (0, 0)),
            pl.BlockSpec(memory_space=pltpu.SMEM),
        ],
        out_specs=pl.BlockSpec((ATT_BLOCK, D_MODEL), lambda i: (i, 0)),
        out_shape=jax.ShapeDtypeStruct((n, D_MODEL), BF16),
        compiler_params=_params(("parallel",)),
        name="band_attn",
    )(u, u, u, u, u, u, u, bias, q_gain, k_gain, sink)


BR_TM = 512
BR_TN = 512


def _branch_kernel(ys_ref, ya_ref, ga_ref, gb_ref, ng_ref, ws_ref, wa_ref, o_ref, yn_scr):
    @pl.when(pl.program_id(1) == 0)
    def _():
        yn_scr[...] = _rms(ys_ref[...].astype(F32), ng_ref[...]).astype(BF16)

    a = jax.nn.sigmoid(ga_ref[...].astype(F32)) * _dot(yn_scr[...], ws_ref[...])
    b = jax.nn.sigmoid(gb_ref[...].astype(F32)) * _dot(ya_ref[...], wa_ref[...])
    o_ref[...] = (a + b).astype(BF16)


def _branch(ys, ya, u, norm_gain, w_ssd, w_att):
    n = ys.shape[0]
    return pl.pallas_call(
        _branch_kernel,
        grid=(n // BR_TM, D_MODEL // BR_TN),
        in_specs=[
            pl.BlockSpec((BR_TM, D_INNER), lambda i, j: (i, 0)),
            pl.BlockSpec((BR_TM, D_MODEL), lambda i, j: (i, 0)),
            pl.BlockSpec((BR_TM, BR_TN), lambda i, j: (i, U_GA // BR_TN + j)),
            pl.BlockSpec((BR_TM, BR_TN), lambda i, j: (i, U_GB // BR_TN + j)),
            pl.BlockSpec((1, D_INNER), lambda i, j: (0, 0)),
            pl.BlockSpec((D_INNER, BR_TN), lambda i, j: (0, j)),
            pl.BlockSpec((D_MODEL, BR_TN), lambda i, j: (0, j)),
        ],
        out_specs=pl.BlockSpec((BR_TM, BR_TN), lambda i, j: (i, j)),
        out_shape=jax.ShapeDtypeStruct((n, D_MODEL), BF16),
        scratch_shapes=[pltpu.VMEM((BR_TM, D_INNER), BF16)],
        compiler_params=_params(("parallel", "arbitrary")),
        name="branch_merge",
    )(ys, ya, u, u, norm_gain, w_ssd, w_att)


OUT_TM = 512


def _outproj_kernel(m_ref, w_ref, x_ref, o_ref):
    o_ref[...] = x_ref[...] + _dot(m_ref[...], w_ref[...])


def _outproj(merged, w_out, x1):
    n = merged.shape[0]
    return pl.pallas_call(
        _outproj_kernel,
        grid=(n // OUT_TM,),
        in_specs=[
            pl.BlockSpec((OUT_TM, D_MODEL), lambda i: (i, 0)),
            pl.BlockSpec((D_MODEL, D_MODEL), lambda i: (0, 0)),
            pl.BlockSpec((OUT_TM, D_MODEL), lambda i: (i, 0)),
        ],
        out_specs=pl.BlockSpec((OUT_TM, D_MODEL), lambda i: (i, 0)),
        out_shape=jax.ShapeDtypeStruct((n, D_MODEL), F32),
        compiler_params=_params(("parallel",)),
        name="out_proj",
    )(merged, w_out, x1)


def _prepare(ffn1_norm, ffn1_w_gate, ffn1_w_up, ffn1_w_down, mix_norm, w_in, conv_w, conv_b,
             ssd_A_log, ssd_dt_bias, ssd_D, ssd_out_norm, q_norm, k_norm, attn_sink, rel_bias,
             w_branch_ssd, w_branch_attn, w_out, ffn2_norm, ffn2_w_gate, ffn2_w_up, ffn2_w_down,
             final_norm):
    l = 0
    w = w_in[l]
    o_z = 0
    o_xbc = o_z + D_INNER
    o_dt = o_xbc + CONV_DIM
    o_q = o_dt + 2 * SSD_HEADS
    o_k = o_q + D_MODEL
    o_v = o_k + KV_DIM
    o_ga = o_v + KV_DIM
    o_gb = o_ga + D_MODEL
    w_u = jnp.concatenate(
        [w[:, o_z:o_dt], w[:, o_q:o_k], w[:, o_ga:o_gb + D_MODEL], w[:, o_k:o_ga]], axis=1).astype(BF16)
    row = lambda v: v.reshape(1, -1).astype(F32)
    return dict(
        ffn1=(row(ffn1_norm[l]), ffn1_w_gate[l].astype(BF16), ffn1_w_up[l].astype(BF16),
              ffn1_w_down[l].astype(BF16), row(mix_norm[l])),
        ffn2=(row(ffn2_norm[l]), ffn2_w_gate[l].astype(BF16), ffn2_w_up[l].astype(BF16),
              ffn2_w_down[l].astype(BF16), row(final_norm[l])),
        w_u=w_u, w_dt=w[:, o_dt:o_q].astype(BF16), dt_bias=row(ssd_dt_bias[l]),
        conv_w=conv_w[l].astype(F32), conv_b=row(conv_b[l]),
        a_log=row(ssd_A_log[l]), d_exp=row(jnp.repeat(ssd_D[l], SSD_HEADDIM)),
        e01=jnp.asarray(_ssd_expand_matrix(), BF16), out_norm=row(ssd_out_norm[l]),
        q_gain=row(q_norm[l]), k_gain=row(k_norm[l]), sink=attn_sink[l].astype(F32),
        bias=_bias_table(rel_bias.astype(F32)),
        w_ssd=w_branch_ssd[l].astype(BF16), w_att=w_branch_attn[l].astype(BF16),
        w_out=w_out[l].astype(BF16),
    )


def _layer(x, p):
    batch, seq, _ = x.shape
    x2d = x.reshape(batch * seq, D_MODEL)
    x1, h = _ffn(x2d, *p["ffn1"], final=False)
    u, dt = _inproj(h, p["w_u"], p["w_dt"], p["dt_bias"])
    xc = _conv(u, p["conv_w"], p["conv_b"], seq)
    ys = _ssd(xc, u, dt, p["a_log"], p["d_exp"], p["e01"], batch, seq)
    ya = _attn(u, p["bias"], p["q_gain"], p["k_gain"], p["sink"], seq)
    merged = _branch(ys, ya, u, p["out_norm"], p["w_ssd"], p["w_att"])
    x2 = _outproj(merged, p["w_out"], x1)
    (y,) = _ffn(x2, *p["ffn2"], final=True)
    return y.reshape(batch, seq, D_MODEL)


def kernel(x_prompt, x_sample, ffn1_norm, ffn1_w_gate, ffn1_w_up, ffn1_w_down, mix_norm, w_in, conv_w, conv_b, ssd_A_log, ssd_dt_bias, ssd_D, ssd_out_norm, q_norm, k_norm, attn_sink, rel_bias, w_branch_ssd, w_branch_attn, w_out, ffn2_norm, ffn2_w_gate, ffn2_w_up, ffn2_w_down, final_norm):
    p = _prepare(ffn1_norm, ffn1_w_gate, ffn1_w_up, ffn1_w_down, mix_norm, w_in, conv_w, conv_b,
                 ssd_A_log, ssd_dt_bias, ssd_D, ssd_out_norm, q_norm, k_norm, attn_sink, rel_bias,
                 w_branch_ssd, w_branch_attn, w_out, ffn2_norm, ffn2_w_gate, ffn2_w_up, ffn2_w_down,
                 final_norm)
    return (_layer(x_prompt, p), _layer(x_sample, p))
```

```python
import functools
import math

import jax
import jax.numpy as jnp
import numpy as np
from jax import lax
from jax.experimental import pallas as pl
from jax.experimental.pallas import tpu as pltpu

D_MODEL = 2048
D_INNER = 2 * D_MODEL
SSD_HEADDIM = 64
SSD_HEADS = D_INNER // SSD_HEADDIM
SSD_GROUPS = 8
SSD_HPG = SSD_HEADS // SSD_GROUPS
D_STATE = 128
CONV_K = 5
CHUNK = 128
CONV_DIM = D_INNER + 2 * SSD_GROUPS * D_STATE
ATT_HEAD_DIM = 128
ATT_Q_HEADS = D_MODEL // ATT_HEAD_DIM
ATT_KV_HEADS = 4
ATT_REP = ATT_Q_HEADS // ATT_KV_HEADS
WINDOW = 128
ATT_BLOCK = 128
N_BUCKETS = 32
MAX_DIST = 128
D_FF = ((8 * D_MODEL // 3 + 255) // 256) * 256
EPS = 1e-6
GROUP_X = SSD_HPG * SSD_HEADDIM
KV_DIM = ATT_KV_HEADS * ATT_HEAD_DIM

U_Z = 0
U_XBC = U_Z + D_INNER
U_Q = U_XBC + CONV_DIM
U_GA = U_Q + D_MODEL
U_GB = U_GA + D_MODEL
U_K = U_GB + D_MODEL
U_V = U_K + KV_DIM
U_TOTAL = U_V + KV_DIM

V7X_VMEM_LIMIT_BYTES = 56 * 1024 * 1024
NEG_BIG = -1e30

BF16 = jnp.bfloat16
F32 = jnp.float32


def _params(semantics):
    return pltpu.CompilerParams(dimension_semantics=semantics,
                                vmem_limit_bytes=V7X_VMEM_LIMIT_BYTES)


def _rms(x, gain):
    return x * lax.rsqrt(jnp.mean(x * x, axis=-1, keepdims=True) + EPS) * gain


def _silu(x):
    return x * jax.nn.sigmoid(x)


def _dot(a, b):
    return jnp.dot(a, b, preferred_element_type=F32)


FFN_TM = 512
FFN_TF = 512


def _ffn_kernel(x_ref, g_ref, wg_ref, wu_ref, wd_ref, pg_ref, *refs, final):
    if final:
        y_ref, h_scr, acc_scr = refs
    else:
        x1_ref, h_ref, h_scr, acc_scr = refs
    j = pl.program_id(1)

    @pl.when(j == 0)
    def _():
        h_scr[...] = _rms(x_ref[...], g_ref[...]).astype(BF16)
        acc_scr[...] = jnp.zeros_like(acc_scr)

    h = h_scr[...]
    t = _silu(_dot(h, wg_ref[...])) * _dot(h, wu_ref[...])
    acc_scr[...] += _dot(t.astype(BF16), wd_ref[...])

    @pl.when(j == pl.num_programs(1) - 1)
    def _():
        x1 = x_ref[...] + 0.5 * acc_scr[...]
        normed = _rms(x1, pg_ref[...])
        if final:
            y_ref[...] = normed
        else:
            x1_ref[...] = x1
            h_ref[...] = normed.astype(BF16)


def _ffn(x, gain, wg, wu, wd, post_gain, final):
    n = x.shape[0]
    row = lambda i, j: (i, 0)
    fixed = lambda i, j: (0, 0)
    out_shape = [jax.ShapeDtypeStruct((n, D_MODEL), F32)]
    out_specs = [pl.BlockSpec((FFN_TM, D_MODEL), row)]
    if not final:
        out_shape.append(jax.ShapeDtypeStruct((n, D_MODEL), BF16))
        out_specs.append(pl.BlockSpec((FFN_TM, D_MODEL), row))
    return pl.pallas_call(
        functools.partial(_ffn_kernel, final=final),
        grid=(n // FFN_TM, D_FF // FFN_TF),
        in_specs=[
            pl.BlockSpec((FFN_TM, D_MODEL), row),
            pl.BlockSpec((1, D_MODEL), fixed),
            pl.BlockSpec((D_MODEL, FFN_TF), lambda i, j: (0, j)),
            pl.BlockSpec((D_MODEL, FFN_TF), lambda i, j: (0, j)),
            pl.BlockSpec((FFN_TF, D_MODEL), lambda i, j: (j, 0)),
            pl.BlockSpec((1, D_MODEL), fixed),
        ],
        out_specs=out_specs,
        out_shape=out_shape,
        scratch_shapes=[pltpu.VMEM((FFN_TM, D_MODEL), BF16), pltpu.VMEM((FFN_TM, D_MODEL), F32)],
        compiler_params=_params(("parallel", "arbitrary")),
        name="ffn_final" if final else "ffn",
    )(x, gain, wg, wu, wd, post_gain)


INP_TM = 1024
INP_TN = 1024


def _inproj_kernel(h_ref, w_ref, wdt_ref, bdt_ref, u_ref, dt_ref):
    h = h_ref[...]
    u_ref[...] = _dot(h, w_ref[...]).astype(BF16)

    @pl.when(pl.program_id(1) == 0)
    def _():
        raw = _dot(h, wdt_ref[...]) + bdt_ref[...]
        dt_ref[...] = jnp.maximum(raw, 0.0) + jnp.log1p(jnp.exp(-jnp.abs(raw)))


def _inproj(h, w_u, w_dt, dt_bias):
    n = h.shape[0]
    ndt = 2 * SSD_HEADS
    return pl.pallas_call(
        _inproj_kernel,
        grid=(n // INP_TM, U_TOTAL // INP_TN),
        in_specs=[
            pl.BlockSpec((INP_TM, D_MODEL), lambda i, j: (i, 0)),
            pl.BlockSpec((D_MODEL, INP_TN), lambda i, j: (0, j)),
            pl.BlockSpec((D_MODEL, ndt), lambda i, j: (0, 0)),
            pl.BlockSpec((1, ndt), lambda i, j: (0, 0)),
        ],
        out_specs=[
            pl.BlockSpec((INP_TM, INP_TN), lambda i, j: (i, j)),
            pl.BlockSpec((INP_TM, ndt), lambda i, j: (i, 0)),
        ],
        out_shape=[jax.ShapeDtypeStruct((n, U_TOTAL), BF16), jax.ShapeDtypeStruct((n, ndt), F32)],
        compiler_params=_params(("parallel", "arbitrary")),
        name="in_proj",
    )(h, w_u, w_dt, dt_bias)


CONV_TS = 512
CONV_TC = 512
CONV_HALO = 16
CONV_PAD = 8


def _conv_kernel(xm_ref, xp_ref, xn_ref, w_ref, b_ref, o_ref, scr, *, tiles_per_seq):
    i = pl.program_id(0)
    first = (i % tiles_per_seq) == 0
    last = ((i + 1) % tiles_per_seq) == 0
    scr[CONV_PAD:CONV_PAD + CONV_TS, :] = xm_ref[...].astype(F32)
    left = xp_ref[...].astype(F32)[CONV_HALO - CONV_PAD:, :]
    right = xn_ref[...].astype(F32)[:CONV_PAD, :]
    scr[0:CONV_PAD, :] = jnp.where(first, 0.0, left)
    scr[CONV_PAD + CONV_TS:, :] = jnp.where(last, 0.0, right)
    acc = jnp.broadcast_to(b_ref[...], (CONV_TS, CONV_TC))
    base = CONV_PAD - CONV_K // 2
    for k in range(CONV_K):
        acc = acc + scr[base + k:base + k + CONV_TS, :] * w_ref[k:k + 1, :]
    o_ref[...] = _silu(acc).astype(BF16)


def _conv(u, conv_w, conv_b, seq):
    n = u.shape[0]
    tiles_per_seq = seq // CONV_TS
    halo_per_tile = CONV_TS // CONV_HALO
    n_halo = n // CONV_HALO
    col0 = U_XBC // CONV_TC
    return pl.pallas_call(
        functools.partial(_conv_kernel, tiles_per_seq=tiles_per_seq),
        grid=(n // CONV_TS, CONV_DIM // CONV_TC),
        in_specs=[
            pl.BlockSpec((CONV_TS, CONV_TC), lambda i, j: (i, col0 + j)),
            pl.BlockSpec((CONV_HALO, CONV_TC),
                         lambda i, j: (jnp.maximum(i * halo_per_tile - 1, 0), col0 + j)),
            pl.BlockSpec((CONV_HALO, CONV_TC),
                         lambda i, j: (jnp.minimum((i + 1) * halo_per_tile, n_halo - 1), col0 + j)),
            pl.BlockSpec((CONV_K, CONV_TC), lambda i, j: (0, j)),
            pl.BlockSpec((1, CONV_TC), lambda i, j: (0, j)),
        ],
        out_specs=pl.BlockSpec((CONV_TS, CONV_TC), lambda i, j: (i, j)),
        out_shape=jax.ShapeDtypeStruct((n, CONV_DIM), BF16),
        scratch_shapes=[pltpu.VMEM((CONV_TS + 2 * CONV_PAD, CONV_TC), F32)],
        compiler_params=_params(("parallel", "parallel")),
        name="conv",
    )(u, u, u, conv_w, conv_b)


def _split3(a):
    hi = a.astype(BF16)
    r1 = a - hi.astype(F32)
    mid = r1.astype(BF16)
    lo = (r1 - mid.astype(F32)).astype(BF16)
    return hi, mid, lo


def _exact_lhs01(m01, a):
    hi, mid, lo = _split3(a)
    return _dot(jnp.concatenate([m01, m01, m01], axis=1), jnp.concatenate([hi, mid, lo], axis=0))


def _expand(a, e01):
    hi = a.astype(BF16)
    mid = (a - hi.astype(F32)).astype(BF16)
    return _dot(jnp.concatenate([hi, mid], axis=1), jnp.concatenate([e01, e01], axis=0))


PREP_CHUNKS = 8
ROWS_PER_GROUP = 6 * SSD_HPG


def _ssd_prep_kernel(dt_ref, a_ref, cum_ref, wdec_ref, ecum_ref, rows_ref):
    L = CHUNK
    H = SSD_HPG
    row = lax.broadcasted_iota(jnp.int32, (L, L), 0)
    col = lax.broadcasted_iota(jnp.int32, (L, L), 1)
    lower = (row >= col).astype(BF16)
    upper = (row <= col).astype(BF16)
    is_fwd = col < SSD_HEADS
    a = -jnp.exp(a_ref[...])
    for t in range(PREP_CHUNKS):
        sl = slice(t * L, (t + 1) * L)
        dt = dt_ref[sl, :]
        dA = dt * a
        pre = _exact_lhs01(lower, dA)
        suf = _exact_lhs01(upper, dA)
        cum = jnp.where(is_fwd, pre, suf)
        tot = jnp.where(is_fwd[:1], pre[L - 1:L, :], suf[0:1, :])
        cum_ref[sl, :] = cum
        wdec_ref[sl, :] = jnp.exp(tot - cum) * dt
        ecum_ref[sl, :] = jnp.exp(cum)
        cum_t = cum.T
        dt_t = dt.T
        dts_t = dt_t[:SSD_HEADS] + dt_t[SSD_HEADS:]
        etot = jnp.broadcast_to(jnp.exp(tot), (H, L))
        for g in range(SSD_GROUPS):
            f = slice(g * H, (g + 1) * H)
            b = slice(SSD_HEADS + g * H, SSD_HEADS + (g + 1) * H)
            rows_ref[t, g] = jnp.concatenate(
                [cum_t[f], cum_t[b], dt_t[f], dt_t[b], dts_t[f], etot], axis=0)


def _ssd_prep(dt, a_log):
    n = dt.shape[0]
    rows_blk = PREP_CHUNKS * CHUNK
    col_spec = pl.BlockSpec((rows_blk, 2 * SSD_HEADS), lambda i: (i, 0))
    col_shape = jax.ShapeDtypeStruct((n, 2 * SSD_HEADS), F32)
    return pl.pallas_call(
        _ssd_prep_kernel,
        grid=(n // rows_blk,),
        in_specs=[col_spec, pl.BlockSpec((1, 2 * SSD_HEADS), lambda i: (0, 0))],
        out_specs=[col_spec, col_spec, col_spec,
                   pl.BlockSpec((PREP_CHUNKS, SSD_GROUPS, ROWS_PER_GROUP, CHUNK), lambda i: (i, 0, 0, 0))],
        out_shape=[col_shape, col_shape, col_shape,
                   jax.ShapeDtypeStruct((n // CHUNK, SSD_GROUPS, ROWS_PER_GROUP, CHUNK), F32)],
        compiler_params=_params(("parallel",)),
        name="ssd_prep",
    )(dt, a_log)


SSD_T = 4


def _ssd_kernel(x_ref, b_ref, c_ref, z_ref, cum_ref, wdec_ref, ecum_ref, rows_ref, d_ref, e_ref,
                y_ref, gst_scr, h_scr):
    g = pl.program_id(1)
    phase = pl.program_id(2)
    step = pl.program_id(3)
    ns = pl.num_programs(3)
    L = CHUNK
    GX = GROUP_X
    H = SSD_HPG

    @pl.when(step == 0)
    def _():
        h_scr[...] = jnp.zeros_like(h_scr)

    def state_update(t, direction):
        sl = slice(t * L, (t + 1) * L)
        e_dir = e_ref[0, :, direction * GX:(direction + 1) * GX]
        w = _expand(wdec_ref[sl, :], e_dir)
        dec = _expand(rows_ref[t, 0, 5 * H:6 * H, :], e_dir)[0:1, :]
        xdec = (w * x_ref[sl, :].astype(F32)).astype(BF16)
        s_new = lax.dot_general(b_ref[sl, :], xdec, (((0,), (0,)), ((), ())),
                                preferred_element_type=F32)
        h_scr[...] = dec * h_scr[...] + s_new

    @pl.when(phase == 0)
    def _():
        for t in reversed(range(SSD_T)):
            gst_scr[(ns - 1 - step) * SSD_T + t] = h_scr[...].astype(BF16)
            state_update(t, 1)

    @pl.when(phase == 1)
    def _():
        row = lax.broadcasted_iota(jnp.int32, (L, L), 0)
        col = lax.broadcasted_iota(jnp.int32, (L, L), 1)
        lower = row >= col
        strict_lower = row > col
        diag = row == col
        low_lane = col < SSD_HEADDIM
        shift = (2 * SSD_HEADS - H * g) % (2 * SSD_HEADS)
        for t in range(SSD_T):
            sl = slice(t * L, (t + 1) * L)
            cm = c_ref[sl, :]
            bm = b_ref[sl, :]
            cb = lax.dot_general(cm, bm, (((1,), (1,)), ((), ())), preferred_element_type=F32)
            cum_r = pltpu.roll(cum_ref[sl, :], shift, axis=1)
            pairs = []
            for k in range(H // 2):
                ms = []
                for half in range(2):
                    hh = 2 * k + half
                    pc = cum_r[:, hh:hh + 1]
                    rc = cum_r[:, SSD_HEADS + hh:SSD_HEADS + hh + 1]
                    pr = rows_ref[t, 0, hh:hh + 1, :]
                    rr = rows_ref[t, 0, H + hh:H + hh + 1, :]
                    dtf = rows_ref[t, 0, 2 * H + hh:2 * H + hh + 1, :]
                    dtb = rows_ref[t, 0, 3 * H + hh:3 * H + hh + 1, :]
                    dts = rows_ref[t, 0, 4 * H + hh:4 * H + hh + 1, :]
                    arg = jnp.where(lower, pc - pr, rc - rr)
                    w = jnp.where(strict_lower, dtf, jnp.where(diag, dts, dtb))
                    ms.append((jnp.exp(arg) * w * cb).astype(BF16))
                xp = x_ref[sl, k * L:(k + 1) * L]
                zero = jnp.zeros_like(xp)
                rhs = jnp.concatenate([jnp.where(low_lane, xp, zero), jnp.where(low_lane, zero, xp)], axis=0)
                pairs.append(_dot(jnp.concatenate(ms, axis=1), rhs))
            y = jnp.concatenate(pairs, axis=1)
            c = step * SSD_T + t
            states = jnp.concatenate([h_scr[...].astype(BF16), gst_scr[c]], axis=1)
            y_off = _expand(ecum_ref[sl, :], e_ref[0]) * _dot(cm, states)
            x = x_ref[sl, :].astype(F32)
            y = y + y_off[:, :GX] + y_off[:, GX:] + d_ref[...] * x
            y_ref[sl, :] = (y * _silu(z_ref[sl, :].astype(F32))).astype(BF16)
            state_update(t, 0)


def _ssd(xc, u, cum, wdec, ecum, rows, d_exp, e01, batch, seq):
    n = xc.shape[0]
    nc = seq // CHUNK
    ns = nc // SSD_T
    tl = SSD_T * CHUNK
    gx_blocks = D_INNER // D_STATE

    def fb_row(b, p, s):
        return b * ns + jnp.where(p == 0, ns - 1 - s, s)

    def f_row(b, p, s):
        return b * ns + jnp.where(p == 0, 0, s)

    return pl.pallas_call(
        _ssd_kernel,
        grid=(batch, SSD_GROUPS, 2, ns),
        in_specs=[
            pl.BlockSpec((tl, GROUP_X), lambda b, g, p, s: (fb_row(b, p, s), g)),
            pl.BlockSpec((tl, D_STATE), lambda b, g, p, s: (fb_row(b, p, s), gx_blocks + g)),
            pl.BlockSpec((tl, D_STATE), lambda b, g, p, s: (f_row(b, p, s), gx_blocks + SSD_GROUPS + g)),
            pl.BlockSpec((tl, GROUP_X), lambda b, g, p, s: (f_row(b, p, s), U_Z // GROUP_X + g)),
            pl.BlockSpec((tl, 2 * SSD_HEADS), lambda b, g, p, s: (f_row(b, p, s), 0)),
            pl.BlockSpec((tl, 2 * SSD_HEADS), lambda b, g, p, s: (fb_row(b, p, s), 0)),
            pl.BlockSpec((tl, 2 * SSD_HEADS), lambda b, g, p, s: (f_row(b, p, s), 0)),
            pl.BlockSpec((SSD_T, 1, ROWS_PER_GROUP, CHUNK), lambda b, g, p, s: (fb_row(b, p, s), g, 0, 0)),
            pl.BlockSpec((1, GROUP_X), lambda b, g, p, s: (0, g)),
            pl.BlockSpec((1, 2 * SSD_HEADS, 2 * GROUP_X), lambda b, g, p, s: (g, 0, 0)),
        ],
        out_specs=pl.BlockSpec((tl, GROUP_X), lambda b, g, p, s: (f_row(b, p, s), g)),
        out_shape=jax.ShapeDtypeStruct((n, D_INNER), BF16),
        scratch_shapes=[
            pltpu.VMEM((nc, D_STATE, GROUP_X), BF16),
            pltpu.VMEM((D_STATE, GROUP_X), F32),
        ],
        compiler_params=_params(("arbitrary", "arbitrary", "arbitrary", "arbitrary")),
        name="ssd",
    )(xc, xc, xc, u, cum, wdec, ecum, rows, d_exp, e01)


def _ssd_expand_matrix():
    e = np.zeros((SSD_GROUPS, 2 * SSD_HEADS, 2 * GROUP_X), np.float32)
    for g in range(SSD_GROUPS):
        for d in range(2):
            for r in range(SSD_HPG):
                lo = d * GROUP_X + r * SSD_HEADDIM
                e[g, d * SSD_HEADS + g * SSD_HPG + r, lo:lo + SSD_HEADDIM] = 1.0
    return e


def _t5_bucket(rel):
    nb = N_BUCKETS // 2
    ret = (rel > 0).astype(np.int32) * nb
    n = np.abs(rel)
    max_exact = nb // 2
    large = max_exact + (np.log(np.maximum(n, 1) / max_exact) / math.log(MAX_DIST / max_exact)
                         * (nb - max_exact)).astype(np.int32)
    large = np.minimum(large, nb - 1)
    return (ret + np.where(n < max_exact, n, large)).astype(np.int32)


def _bucket_table():
    i = np.arange(ATT_BLOCK)[:, None]
    j = np.arange(3 * ATT_BLOCK)[None, :]
    rel = j - ATT_BLOCK - i
    return np.where(np.abs(rel) <= WINDOW, _t5_bucket(rel), -1).astype(np.int32)


def _bias_kernel(rel_ref, bucket_ref, o_ref):
    h = pl.program_id(0)
    bucket = bucket_ref[...]
    acc = jnp.full(bucket.shape, NEG_BIG, F32)
    for b in range(N_BUCKETS):
        acc = jnp.where(bucket == b, rel_ref[b, h], acc)
    o_ref[0] = acc


def _bias_table(rel_bias):
    bucket = jnp.asarray(_bucket_table())
    return pl.pallas_call(
        _bias_kernel,
        grid=(ATT_Q_HEADS,),
        in_specs=[
            pl.BlockSpec(memory_space=pltpu.SMEM),
            pl.BlockSpec((ATT_BLOCK, 3 * ATT_BLOCK), lambda h: (0, 0)),
        ],
        out_specs=pl.BlockSpec((1, ATT_BLOCK, 3 * ATT_BLOCK), lambda h: (h, 0, 0)),
        out_shape=jax.ShapeDtypeStruct((ATT_Q_HEADS, ATT_BLOCK, 3 * ATT_BLOCK), F32),
        compiler_params=_params(("arbitrary",)),
        name="rel_bias_table",
    )(rel_bias, bucket)


def _attn_kernel(q_ref, kp_ref, kc_ref, kn_ref, vp_ref, vc_ref, vn_ref, bias_ref,
                 qg_ref, kg_ref, sink_ref, o_ref, *, blocks_per_seq):
    i = pl.program_id(0)
    first = (i % blocks_per_seq) == 0
    last = ((i + 1) % blocks_per_seq) == 0
    T = ATT_BLOCK
    hd = ATT_HEAD_DIM
    kcol = lax.broadcasted_iota(jnp.int32, (1, 3 * T), 1)
    dead = (first & (kcol < T)) | (last & (kcol >= 2 * T))
    edge = jnp.where(dead, NEG_BIG, 0.0)
    scale = hd ** -0.5
    for g in range(ATT_KV_HEADS):
        ks = slice(g * hd, (g + 1) * hd)
        kwin = jnp.concatenate([kp_ref[:, ks], kc_ref[:, ks], kn_ref[:, ks]], axis=0).astype(F32)
        kwin = _rms(kwin, kg_ref[...]).astype(BF16)
        vwin = jnp.concatenate([vp_ref[:, ks], vc_ref[:, ks], vn_ref[:, ks]], axis=0)
        qs = jnp.concatenate(
            [q_ref[:, (g * ATT_REP + r) * hd:(g * ATT_REP + r + 1) * hd] for r in range(ATT_REP)],
            axis=0).astype(F32)
        qs = (_rms(qs, qg_ref[...]) * scale).astype(BF16)
        s = lax.dot_general(qs, kwin, (((1,), (1,)), ((), ())), preferred_element_type=F32)
        bias = jnp.concatenate([bias_ref[g * ATT_REP + r] for r in range(ATT_REP)], axis=0)
        s = s + bias + edge
        sink = jnp.concatenate(
            [jnp.full((T, 1), sink_ref[g * ATT_REP + r], F32) for r in range(ATT_REP)], axis=0)
        m = jnp.maximum(jnp.max(s, axis=-1, keepdims=True), sink)
        p = jnp.exp(s - m)
        denom = jnp.sum(p, axis=-1, keepdims=True) + jnp.exp(sink - m)
        o = _dot(p.astype(BF16), vwin) / denom
        for r in range(ATT_REP):
            hq = g * ATT_REP + r
            o_ref[:, hq * hd:(hq + 1) * hd] = o[r * T:(r + 1) * T, :].astype(BF16)


def _attn(u, bias, q_gain, k_gain, sink, seq):
    n = u.shape[0]
    nblk = n // ATT_BLOCK
    blocks_per_seq = seq // ATT_BLOCK
    kcol = U_K // KV_DIM
    vcol = U_V // KV_DIM
    prev = lambda i: jnp.maximum(i - 1, 0)
    nxt = lambda i: jnp.minimum(i + 1, nblk - 1)
    kv_spec = lambda rowf, c: pl.BlockSpec((ATT_BLOCK, KV_DIM), lambda i: (rowf(i), c))
    same = lambda i: i
    return pl.pallas_call(
        functools.partial(_attn_kernel, blocks_per_seq=blocks_per_seq),
        grid=(nblk,),
        in_specs=[
            pl.BlockSpec((ATT_BLOCK, D_MODEL), lambda i: (i, U_Q // D_MODEL)),
            kv_spec(prev, kcol), kv_spec(same, kcol), kv_spec(nxt, kcol),
            kv_spec(prev, vcol), kv_spec(same, vcol), kv_spec(nxt, vcol),
            pl.BlockSpec((ATT_Q_HEADS, ATT_BLOCK, 3 * ATT_BLOCK), lambda i: (0, 0, 0)),
            pl.BlockSpec((1, ATT_HEAD_DIM), lambda i: (0, 0)),
            pl.BlockSpec((1, ATT_HEAD_DIM), lambda i: (0, 0)),
            pl.BlockSpec(memory_space=pltpu.SMEM),
        ],
        out_specs=pl.BlockSpec((ATT_BLOCK, D_MODEL), lambda i: (i, 0)),
        out_shape=jax.ShapeDtypeStruct((n, D_MODEL), BF16),
        compiler_params=_params(("parallel",)),
        name="band_attn",
    )(u, u, u, u, u, u, u, bias, q_gain, k_gain, sink)


BR_TM = 512
BR_TN = 512


def _branch_kernel(ys_ref, ya_ref, ga_ref, gb_ref, ng_ref, ws_ref, wa_ref, o_ref, yn_scr):
    @pl.when(pl.program_id(1) == 0)
    def _():
        yn_scr[...] = _rms(ys_ref[...].astype(F32), ng_ref[...]).astype(BF16)

    a = jax.nn.sigmoid(ga_ref[...].astype(F32)) * _dot(yn_scr[...], ws_ref[...])
    b = jax.nn.sigmoid(gb_ref[...].astype(F32)) * _dot(ya_ref[...], wa_ref[...])
    o_ref[...] = (a + b).astype(BF16)


def _branch(ys, ya, u, norm_gain, w_ssd, w_att):
    n = ys.shape[0]
    return pl.pallas_call(
        _branch_kernel,
        grid=(n // BR_TM, D_MODEL // BR_TN),
        in_specs=[
            pl.BlockSpec((BR_TM, D_INNER), lambda i, j: (i, 0)),
            pl.BlockSpec((BR_TM, D_MODEL), lambda i, j: (i, 0)),
            pl.BlockSpec((BR_TM, BR_TN), lambda i, j: (i, U_GA // BR_TN + j)),
            pl.BlockSpec((BR_TM, BR_TN), lambda i, j: (i, U_GB // BR_TN + j)),
            pl.BlockSpec((1, D_INNER), lambda i, j: (0, 0)),
            pl.BlockSpec((D_INNER, BR_TN), lambda i, j: (0, j)),
            pl.BlockSpec((D_MODEL, BR_TN), lambda i, j: (0, j)),
        ],
        out_specs=pl.BlockSpec((BR_TM, BR_TN), lambda i, j: (i, j)),
        out_shape=jax.ShapeDtypeStruct((n, D_MODEL), BF16),
        scratch_shapes=[pltpu.VMEM((BR_TM, D_INNER), BF16)],
        compiler_params=_params(("parallel", "arbitrary")),
        name="branch_merge",
    )(ys, ya, u, u, norm_gain, w_ssd, w_att)


OUT_TM = 512


def _outproj_kernel(m_ref, w_ref, x_ref, o_ref):
    o_ref[...] = x_ref[...] + _dot(m_ref[...], w_ref[...])


def _outproj(merged, w_out, x1):
    n = merged.shape[0]
    return pl.pallas_call(
        _outproj_kernel,
        grid=(n // OUT_TM,),
        in_specs=[
            pl.BlockSpec((OUT_TM, D_MODEL), lambda i: (i, 0)),
            pl.BlockSpec((D_MODEL, D_MODEL), lambda i: (0, 0)),
            pl.BlockSpec((OUT_TM, D_MODEL), lambda i: (i, 0)),
        ],
        out_specs=pl.BlockSpec((OUT_TM, D_MODEL), lambda i: (i, 0)),
        out_shape=jax.ShapeDtypeStruct((n, D_MODEL), F32),
        compiler_params=_params(("parallel",)),
        name="out_proj",
    )(merged, w_out, x1)


def _prepare(ffn1_norm, ffn1_w_gate, ffn1_w_up, ffn1_w_down, mix_norm, w_in, conv_w, conv_b,
             ssd_A_log, ssd_dt_bias, ssd_D, ssd_out_norm, q_norm, k_norm, attn_sink, rel_bias,
             w_branch_ssd, w_branch_attn, w_out, ffn2_norm, ffn2_w_gate, ffn2_w_up, ffn2_w_down,
             final_norm):
    l = 0
    w = w_in[l]
    o_z = 0
    o_xbc = o_z + D_INNER
    o_dt = o_xbc + CONV_DIM
    o_q = o_dt + 2 * SSD_HEADS
    o_k = o_q + D_MODEL
    o_v = o_k + KV_DIM
    o_ga = o_v + KV_DIM
    o_gb = o_ga + D_MODEL
    w_u = jnp.concatenate(
        [w[:, o_z:o_dt], w[:, o_q:o_k], w[:, o_ga:o_gb + D_MODEL], w[:, o_k:o_ga]], axis=1).astype(BF16)
    row = lambda v: v.reshape(1, -1).astype(F32)
    return dict(
        ffn1=(row(ffn1_norm[l]), ffn1_w_gate[l].astype(BF16), ffn1_w_up[l].astype(BF16),
              ffn1_w_down[l].astype(BF16), row(mix_norm[l])),
        ffn2=(row(ffn2_norm[l]), ffn2_w_gate[l].astype(BF16), ffn2_w_up[l].astype(BF16),
              ffn2_w_down[l].astype(BF16), row(final_norm[l])),
        w_u=w_u, w_dt=w[:, o_dt:o_q].astype(BF16), dt_bias=row(ssd_dt_bias[l]),
        conv_w=conv_w[l].astype(F32), conv_b=row(conv_b[l]),
        a_log=row(ssd_A_log[l]), d_exp=row(jnp.repeat(ssd_D[l], SSD_HEADDIM)),
        e01=jnp.asarray(_ssd_expand_matrix(), BF16), out_norm=row(ssd_out_norm[l]),
        q_gain=row(q_norm[l]), k_gain=row(k_norm[l]), sink=attn_sink[l].astype(F32),
        bias=_bias_table(rel_bias.astype(F32)),
        w_ssd=w_branch_ssd[l].astype(BF16), w_att=w_branch_attn[l].astype(BF16),
        w_out=w_out[l].astype(BF16),
    )


def _layer(x, p):
    batch, seq, _ = x.shape
    x2d = x.reshape(batch * seq, D_MODEL)
    x1, h = _ffn(x2d, *p["ffn1"], final=False)
    u, dt = _inproj(h, p["w_u"], p["w_dt"], p["dt_bias"])
    xc = _conv(u, p["conv_w"], p["conv_b"], seq)
    cum, wdec, ecum, rows = _ssd_prep(dt, p["a_log"])
    ys = _ssd(xc, u, cum, wdec, ecum, rows, p["d_exp"], p["e01"], batch, seq)
    ya = _attn(u, p["bias"], p["q_gain"], p["k_gain"], p["sink"], seq)
    merged = _branch(ys, ya, u, p["out_norm"], p["w_ssd"], p["w_att"])
    x2 = _outproj(merged, p["w_out"], x1)
    (y,) = _ffn(x2, *p["ffn2"], final=True)
    return y.reshape(batch, seq, D_MODEL)


def kernel(x_prompt, x_sample, ffn1_norm, ffn1_w_gate, ffn1_w_up, ffn1_w_down, mix_norm, w_in, conv_w, conv_b, ssd_A_log, ssd_dt_bias, ssd_D, ssd_out_norm, q_norm, k_norm, attn_sink, rel_bias, w_branch_ssd, w_branch_attn, w_out, ffn2_norm, ffn2_w_gate, ffn2_w_up, ffn2_w_down, final_norm):
    p = _prepare(ffn1_norm, ffn1_w_gate, ffn1_w_up, ffn1_w_down, mix_norm, w_in, conv_w, conv_b,
                 ssd_A_log, ssd_dt_bias, ssd_D, ssd_out_norm, q_norm, k_norm, attn_sink, rel_bias,
                 w_branch_ssd, w_branch_attn, w_out, ffn2_norm, ffn2_w_gate, ffn2_w_up, ffn2_w_down,
                 final_norm)
    return (_layer(x_prompt, p), _layer(x_sample, p))
```

```python
import functools
import math

import jax
import jax.numpy as jnp
import numpy as np
from jax import lax
from jax.experimental import pallas as pl
from jax.experimental.pallas import tpu as pltpu

D_MODEL = 2048
D_INNER = 2 * D_MODEL
SSD_HEADDIM = 64
SSD_HEADS = D_INNER // SSD_HEADDIM
SSD_GROUPS = 8
SSD_HPG = SSD_HEADS // SSD_GROUPS
D_STATE = 128
CONV_K = 5
CHUNK = 128
CONV_DIM = D_INNER + 2 * SSD_GROUPS * D_STATE
ATT_HEAD_DIM = 128
ATT_Q_HEADS = D_MODEL // ATT_HEAD_DIM
ATT_KV_HEADS = 4
ATT_REP = ATT_Q_HEADS // ATT_KV_HEADS
WINDOW = 128
ATT_BLOCK = 128
N_BUCKETS = 32
MAX_DIST = 128
D_FF = ((8 * D_MODEL // 3 + 255) // 256) * 256
EPS = 1e-6
GROUP_X = SSD_HPG * SSD_HEADDIM
KV_DIM = ATT_KV_HEADS * ATT_HEAD_DIM

U_Z = 0
U_XBC = U_Z + D_INNER
U_Q = U_XBC + CONV_DIM
U_GA = U_Q + D_MODEL
U_GB = U_GA + D_MODEL
U_K = U_GB + D_MODEL
U_V = U_K + KV_DIM
U_TOTAL = U_V + KV_DIM

V7X_VMEM_LIMIT_BYTES = 56 * 1024 * 1024
NEG_BIG = -1e30

BF16 = jnp.bfloat16
F32 = jnp.float32


def _params(semantics):
    return pltpu.CompilerParams(dimension_semantics=semantics,
                                vmem_limit_bytes=V7X_VMEM_LIMIT_BYTES)


def _rms(x, gain):
    return x * lax.rsqrt(jnp.mean(x * x, axis=-1, keepdims=True) + EPS) * gain


def _silu(x):
    return x * jax.nn.sigmoid(x)


def _dot(a, b):
    return jnp.dot(a, b, preferred_element_type=F32)


FFN_TM = 512
FFN_TF = 512


def _ffn_kernel(x_ref, g_ref, wg_ref, wu_ref, wd_ref, pg_ref, *refs, final):
    if final:
        y_ref, h_scr, acc_scr = refs
    else:
        x1_ref, h_ref, h_scr, acc_scr = refs
    j = pl.program_id(1)

    @pl.when(j == 0)
    def _():
        h_scr[...] = _rms(x_ref[...], g_ref[...]).astype(BF16)
        acc_scr[...] = jnp.zeros_like(acc_scr)

    h = h_scr[...]
    t = _silu(_dot(h, wg_ref[...])) * _dot(h, wu_ref[...])
    acc_scr[...] += _dot(t.astype(BF16), wd_ref[...])

    @pl.when(j == pl.num_programs(1) - 1)
    def _():
        x1 = x_ref[...] + 0.5 * acc_scr[...]
        normed = _rms(x1, pg_ref[...])
        if final:
            y_ref[...] = normed
        else:
            x1_ref[...] = x1
            h_ref[...] = normed.astype(BF16)


def _ffn(x, gain, wg, wu, wd, post_gain, final):
    n = x.shape[0]
    row = lambda i, j: (i, 0)
    fixed = lambda i, j: (0, 0)
    out_shape = [jax.ShapeDtypeStruct((n, D_MODEL), F32)]
    out_specs = [pl.BlockSpec((FFN_TM, D_MODEL), row)]
    if not final:
        out_shape.append(jax.ShapeDtypeStruct((n, D_MODEL), BF16))
        out_specs.append(pl.BlockSpec((FFN_TM, D_MODEL), row))
    return pl.pallas_call(
        functools.partial(_ffn_kernel, final=final),
        grid=(n // FFN_TM, D_FF // FFN_TF),
        in_specs=[
            pl.BlockSpec((FFN_TM, D_MODEL), row),
            pl.BlockSpec((1, D_MODEL), fixed),
            pl.BlockSpec((D_MODEL, FFN_TF), lambda i, j: (0, j)),
            pl.BlockSpec((D_MODEL, FFN_TF), lambda i, j: (0, j)),
            pl.BlockSpec((FFN_TF, D_MODEL), lambda i, j: (j, 0)),
            pl.BlockSpec((1, D_MODEL), fixed),
        ],
        out_specs=out_specs,
        out_shape=out_shape,
        scratch_shapes=[pltpu.VMEM((FFN_TM, D_MODEL), BF16), pltpu.VMEM((FFN_TM, D_MODEL), F32)],
        compiler_params=_params(("parallel", "arbitrary")),
        name="ffn_final" if final else "ffn",
    )(x, gain, wg, wu, wd, post_gain)


INP_TM = 1024
INP_TN = 1024


def _inproj_kernel(h_ref, w_ref, wdt_ref, bdt_ref, u_ref, dt_ref):
    h = h_ref[...]
    u_ref[...] = _dot(h, w_ref[...]).astype(BF16)

    @pl.when(pl.program_id(1) == 0)
    def _():
        raw = _dot(h, wdt_ref[...]) + bdt_ref[...]
        dt_ref[...] = jnp.maximum(raw, 0.0) + jnp.log1p(jnp.exp(-jnp.abs(raw)))


def _inproj(h, w_u, w_dt, dt_bias):
    n = h.shape[0]
    ndt = 2 * SSD_HEADS
    return pl.pallas_call(
        _inproj_kernel,
        grid=(n // INP_TM, U_TOTAL // INP_TN),
        in_specs=[
            pl.BlockSpec((INP_TM, D_MODEL), lambda i, j: (i, 0)),
            pl.BlockSpec((D_MODEL, INP_TN), lambda i, j: (0, j)),
            pl.BlockSpec((D_MODEL, ndt), lambda i, j: (0, 0)),
            pl.BlockSpec((1, ndt), lambda i, j: (0, 0)),
        ],
        out_specs=[
            pl.BlockSpec((INP_TM, INP_TN), lambda i, j: (i, j)),
            pl.BlockSpec((INP_TM, ndt), lambda i, j: (i, 0)),
        ],
        out_shape=[jax.ShapeDtypeStruct((n, U_TOTAL), BF16), jax.ShapeDtypeStruct((n, ndt), F32)],
        compiler_params=_params(("parallel", "arbitrary")),
        name="in_proj",
    )(h, w_u, w_dt, dt_bias)


CONV_TS = 512
CONV_TC = 512
CONV_HALO = 16
CONV_PAD = 8


def _conv_kernel(xm_ref, xp_ref, xn_ref, w_ref, b_ref, o_ref, scr, *, tiles_per_seq):
    i = pl.program_id(0)
    first = (i % tiles_per_seq) == 0
    last = ((i + 1) % tiles_per_seq) == 0
    scr[CONV_PAD:CONV_PAD + CONV_TS, :] = xm_ref[...].astype(F32)
    left = xp_ref[...].astype(F32)[CONV_HALO - CONV_PAD:, :]
    right = xn_ref[...].astype(F32)[:CONV_PAD, :]
    scr[0:CONV_PAD, :] = jnp.where(first, 0.0, left)
    scr[CONV_PAD + CONV_TS:, :] = jnp.where(last, 0.0, right)
    acc = jnp.broadcast_to(b_ref[...], (CONV_TS, CONV_TC))
    base = CONV_PAD - CONV_K // 2
    for k in range(CONV_K):
        acc = acc + scr[base + k:base + k + CONV_TS, :] * w_ref[k:k + 1, :]
    o_ref[...] = _silu(acc).astype(BF16)


def _conv(u, conv_w, conv_b, seq):
    n = u.shape[0]
    tiles_per_seq = seq // CONV_TS
    halo_per_tile = CONV_TS // CONV_HALO
    n_halo = n // CONV_HALO
    col0 = U_XBC // CONV_TC
    return pl.pallas_call(
        functools.partial(_conv_kernel, tiles_per_seq=tiles_per_seq),
        grid=(n // CONV_TS, CONV_DIM // CONV_TC),
        in_specs=[
            pl.BlockSpec((CONV_TS, CONV_TC), lambda i, j: (i, col0 + j)),
            pl.BlockSpec((CONV_HALO, CONV_TC),
                         lambda i, j: (jnp.maximum(i * halo_per_tile - 1, 0), col0 + j)),
            pl.BlockSpec((CONV_HALO, CONV_TC),
                         lambda i, j: (jnp.minimum((i + 1) * halo_per_tile, n_halo - 1), col0 + j)),
            pl.BlockSpec((CONV_K, CONV_TC), lambda i, j: (0, j)),
            pl.BlockSpec((1, CONV_TC), lambda i, j: (0, j)),
        ],
        out_specs=pl.BlockSpec((CONV_TS, CONV_TC), lambda i, j: (i, j)),
        out_shape=jax.ShapeDtypeStruct((n, CONV_DIM), BF16),
        scratch_shapes=[pltpu.VMEM((CONV_TS + 2 * CONV_PAD, CONV_TC), F32)],
        compiler_params=_params(("parallel", "parallel")),
        name="conv",
    )(u, u, u, conv_w, conv_b)


def _split3(a):
    hi = a.astype(BF16)
    r1 = a - hi.astype(F32)
    mid = r1.astype(BF16)
    lo = (r1 - mid.astype(F32)).astype(BF16)
    return hi, mid, lo


def _exact_lhs01(m01, a):
    hi, mid, lo = _split3(a)
    return _dot(jnp.concatenate([m01, m01, m01], axis=1), jnp.concatenate([hi, mid, lo], axis=0))


def _split2(a):
    hi = a.astype(BF16)
    mid = (a - hi.astype(F32)).astype(BF16)
    return jnp.concatenate([hi, mid], axis=1)


PREP_CHUNKS = 8
ROWS_PER_GROUP = 4 * SSD_HPG
LOG2E = 1.4426950408889634


def _ssd_prep_kernel(dt_ref, a_ref, cum_ref, wdec_ref, ecum_ref, rows_ref):
    L = CHUNK
    H = SSD_HPG
    row = lax.broadcasted_iota(jnp.int32, (L, L), 0)
    col = lax.broadcasted_iota(jnp.int32, (L, L), 1)
    lower = (row >= col).astype(BF16)
    upper = (row <= col).astype(BF16)
    is_fwd = col < SSD_HEADS
    a = -jnp.exp(a_ref[...])
    for t in range(PREP_CHUNKS):
        sl = slice(t * L, (t + 1) * L)
        dt = dt_ref[sl, :]
        dA = dt * a
        pre = _exact_lhs01(lower, dA)
        suf = _exact_lhs01(upper, dA)
        cum = jnp.where(is_fwd, pre, suf)
        tot = jnp.where(is_fwd[:1], pre[L - 1:L, :], suf[0:1, :])
        cum_ref[sl, :] = cum * LOG2E
        wdec_ref[sl, :] = _split2(jnp.exp(tot - cum) * dt)
        ecum_ref[sl, :] = _split2(jnp.exp(cum))
        dt_t = dt.T
        src_t = (cum.T - jnp.log(dt_t)) * LOG2E
        lds_t = jnp.log2(dt_t[:SSD_HEADS] + dt_t[SSD_HEADS:])
        etot = jnp.broadcast_to(jnp.exp(tot), (H, L))
        for g in range(SSD_GROUPS):
            f = slice(g * H, (g + 1) * H)
            b = slice(SSD_HEADS + g * H, SSD_HEADS + (g + 1) * H)
            rows_ref[t, g] = jnp.concatenate([src_t[f], src_t[b], lds_t[f], etot], axis=0)


def _ssd_prep(dt, a_log):
    n = dt.shape[0]
    rows_blk = PREP_CHUNKS * CHUNK
    nh = 2 * SSD_HEADS
    return pl.pallas_call(
        _ssd_prep_kernel,
        grid=(n // rows_blk,),
        in_specs=[pl.BlockSpec((rows_blk, nh), lambda i: (i, 0)), pl.BlockSpec((1, nh), lambda i: (0, 0))],
        out_specs=[pl.BlockSpec((rows_blk, nh), lambda i: (i, 0)),
                   pl.BlockSpec((rows_blk, 2 * nh), lambda i: (i, 0)),
                   pl.BlockSpec((rows_blk, 2 * nh), lambda i: (i, 0)),
                   pl.BlockSpec((PREP_CHUNKS, SSD_GROUPS, ROWS_PER_GROUP, CHUNK), lambda i: (i, 0, 0, 0))],
        out_shape=[jax.ShapeDtypeStruct((n, nh), F32),
                   jax.ShapeDtypeStruct((n, 2 * nh), BF16),
                   jax.ShapeDtypeStruct((n, 2 * nh), BF16),
                   jax.ShapeDtypeStruct((n // CHUNK, SSD_GROUPS, ROWS_PER_GROUP, CHUNK), F32)],
        compiler_params=_params(("parallel",)),
        name="ssd_prep",
    )(dt, a_log)


SSD_T = 8


def _ssd_kernel(x_ref, b_ref, c_ref, z_ref, cum_ref, wdec_ref, ecum_ref, rows_ref, d_ref, e_ref,
                y_ref, gst_scr, h_scr):
    g = pl.program_id(1)
    phase = pl.program_id(2)
    step = pl.program_id(3)
    ns = pl.num_programs(3)
    L = CHUNK
    GX = GROUP_X
    H = SSD_HPG

    @pl.when(step == 0)
    def _():
        h_scr[...] = jnp.zeros_like(h_scr)

    def state_update(t, direction, xf, bm):
        sl = slice(t * L, (t + 1) * L)
        e_dir = e_ref[0, :, direction * GX:(direction + 1) * GX]
        w = _dot(wdec_ref[sl, :], e_dir)
        dec = _dot(_split2(rows_ref[t, 0, 3 * H:4 * H, :]), e_dir)[0:1, :]
        xdec = (w * xf).astype(BF16)
        b_t = bm.astype(F32).T.astype(BF16)
        h_scr[...] = dec * h_scr[...] + _dot(b_t, xdec)

    @pl.when(phase == 0)
    def _():
        for t in reversed(range(SSD_T)):
            sl = slice(t * L, (t + 1) * L)
            gst_scr[(ns - 1 - step) * SSD_T + t] = h_scr[...].astype(BF16)
            state_update(t, 1, x_ref[sl, :].astype(F32), b_ref[sl, :])

    @pl.when(phase == 1)
    def _():
        row = lax.broadcasted_iota(jnp.int32, (L, L), 0)
        col = lax.broadcasted_iota(jnp.int32, (L, L), 1)
        lower = row >= col
        diag = row == col
        low_lane = col < SSD_HEADDIM
        shift = (2 * SSD_HEADS - H * g) % (2 * SSD_HEADS)
        for t in range(SSD_T):
            sl = slice(t * L, (t + 1) * L)
            cm = c_ref[sl, :]
            bm = b_ref[sl, :]
            xb = x_ref[sl, :]
            xf = xb.astype(F32)
            cb = lax.dot_general(cm, bm, (((1,), (1,)), ((), ())), preferred_element_type=F32)
            cum_r = pltpu.roll(cum_ref[sl, :], shift, axis=1)
            pairs = []
            for k in range(H // 2):
                ms = []
                for half in range(2):
                    hh = 2 * k + half
                    fwd = cum_r[:, hh:hh + 1] - rows_ref[t, 0, hh:hh + 1, :]
                    bwd = cum_r[:, SSD_HEADS + hh:SSD_HEADS + hh + 1] - rows_ref[t, 0, H + hh:H + hh + 1, :]
                    arg = jnp.where(diag, rows_ref[t, 0, 2 * H + hh:2 * H + hh + 1, :],
                                    jnp.where(lower, fwd, bwd))
                    ms.append((jnp.exp2(arg) * cb).astype(BF16))
                xp = xb[:, k * L:(k + 1) * L]
                zero = jnp.zeros_like(xp)
                rhs = jnp.concatenate([jnp.where(low_lane, xp, zero), jnp.where(low_lane, zero, xp)], axis=0)
                pairs.append(_dot(jnp.concatenate(ms, axis=1), rhs))
            y = jnp.concatenate(pairs, axis=1)
            c = step * SSD_T + t
            states = jnp.concatenate([h_scr[...].astype(BF16), gst_scr[c]], axis=1)
            y_off = _dot(ecum_ref[sl, :], e_ref[0]) * _dot(cm, states)
            y = y + y_off[:, :GX] + y_off[:, GX:] + d_ref[...] * xf
            y_ref[sl, :] = (y * _silu(z_ref[sl, :].astype(F32))).astype(BF16)
            state_update(t, 0, xf, bm)


def _ssd(xc, u, cum, wdec, ecum, rows, d_exp, e01, batch, seq):
    n = xc.shape[0]
    nc = seq // CHUNK
    ns = nc // SSD_T
    tl = SSD_T * CHUNK
    nh = 2 * SSD_HEADS
    gx_blocks = D_INNER // D_STATE

    def fb_row(b, p, s):
        return b * ns + jnp.where(p == 0, ns - 1 - s, s)

    def f_row(b, p, s):
        return b * ns + jnp.where(p == 0, 0, s)

    return pl.pallas_call(
        _ssd_kernel,
        grid=(batch, SSD_GROUPS, 2, ns),
        in_specs=[
            pl.BlockSpec((tl, GROUP_X), lambda b, g, p, s: (fb_row(b, p, s), g)),
            pl.BlockSpec((tl, D_STATE), lambda b, g, p, s: (fb_row(b, p, s), gx_blocks + g)),
            pl.BlockSpec((tl, D_STATE), lambda b, g, p, s: (f_row(b, p, s), gx_blocks + SSD_GROUPS + g)),
            pl.BlockSpec((tl, GROUP_X), lambda b, g, p, s: (f_row(b, p, s), U_Z // GROUP_X + g)),
            pl.BlockSpec((tl, nh), lambda b, g, p, s: (f_row(b, p, s), 0)),
            pl.BlockSpec((tl, 2 * nh), lambda b, g, p, s: (fb_row(b, p, s), 0)),
            pl.BlockSpec((tl, 2 * nh), lambda b, g, p, s: (f_row(b, p, s), 0)),
            pl.BlockSpec((SSD_T, 1, ROWS_PER_GROUP, CHUNK), lambda b, g, p, s: (fb_row(b, p, s), g, 0, 0)),
            pl.BlockSpec((1, GROUP_X), lambda b, g, p, s: (0, g)),
            pl.BlockSpec((1, 2 * nh, 2 * GROUP_X), lambda b, g, p, s: (g, 0, 0)),
        ],
        out_specs=pl.BlockSpec((tl, GROUP_X), lambda b, g, p, s: (f_row(b, p, s), g)),
        out_shape=jax.ShapeDtypeStruct((n, D_INNER), BF16),
        scratch_shapes=[
            pltpu.VMEM((nc, D_STATE, GROUP_X), BF16),
            pltpu.VMEM((D_STATE, GROUP_X), F32),
        ],
        compiler_params=_params(("arbitrary", "arbitrary", "arbitrary", "arbitrary")),
        name="ssd",
    )(xc, xc, xc, u, cum, wdec, ecum, rows, d_exp, e01)


def _ssd_expand_matrix():
    e = np.zeros((SSD_GROUPS, 2 * SSD_HEADS, 2 * GROUP_X), np.float32)
    for g in range(SSD_GROUPS):
        for d in range(2):
            for r in range(SSD_HPG):
                lo = d * GROUP_X + r * SSD_HEADDIM
                e[g, d * SSD_HEADS + g * SSD_HPG + r, lo:lo + SSD_HEADDIM] = 1.0
    return np.concatenate([e, e], axis=1)


def _t5_bucket(rel):
    nb = N_BUCKETS // 2
    ret = (rel > 0).astype(np.int32) * nb
    n = np.abs(rel)
    max_exact = nb // 2
    large = max_exact + (np.log(np.maximum(n, 1) / max_exact) / math.log(MAX_DIST / max_exact)
                         * (nb - max_exact)).astype(np.int32)
    large = np.minimum(large, nb - 1)
    return (ret + np.where(n < max_exact, n, large)).astype(np.int32)


def _bucket_table():
    i = np.arange(ATT_BLOCK)[:, None]
    j = np.arange(3 * ATT_BLOCK)[None, :]
    rel = j - ATT_BLOCK - i
    return np.where(np.abs(rel) <= WINDOW, _t5_bucket(rel), -1).astype(np.int32)


def _bias_kernel(rel_ref, bucket_ref, o_ref):
    h = pl.program_id(0)
    bucket = bucket_ref[...]
    acc = jnp.full(bucket.shape, NEG_BIG, F32)
    for b in range(N_BUCKETS):
        acc = jnp.where(bucket == b, rel_ref[b, h], acc)
    o_ref[0] = acc


def _bias_table(rel_bias):
    bucket = jnp.asarray(_bucket_table())
    return pl.pallas_call(
        _bias_kernel,
        grid=(ATT_Q_HEADS,),
        in_specs=[
            pl.BlockSpec(memory_space=pltpu.SMEM),
            pl.BlockSpec((ATT_BLOCK, 3 * ATT_BLOCK), lambda h: (0, 0)),
        ],
        out_specs=pl.BlockSpec((1, ATT_BLOCK, 3 * ATT_BLOCK), lambda h: (h, 0, 0)),
        out_shape=jax.ShapeDtypeStruct((ATT_Q_HEADS, ATT_BLOCK, 3 * ATT_BLOCK), F32),
        compiler_params=_params(("arbitrary",)),
        name="rel_bias_table",
    )(rel_bias, bucket)


def _attn_kernel(q_ref, kp_ref, kc_ref, kn_ref, vp_ref, vc_ref, vn_ref, bias_ref,
                 qg_ref, kg_ref, sink_ref, o_ref, *, blocks_per_seq):
    i = pl.program_id(0)
    first = (i % blocks_per_seq) == 0
    last = ((i + 1) % blocks_per_seq) == 0
    T = ATT_BLOCK
    hd = ATT_HEAD_DIM
    kcol = lax.broadcasted_iota(jnp.int32, (1, 3 * T), 1)
    dead = (first & (kcol < T)) | (last & (kcol >= 2 * T))
    edge = jnp.where(dead, NEG_BIG, 0.0)
    scale = hd ** -0.5
    for g in range(ATT_KV_HEADS):
        ks = slice(g * hd, (g + 1) * hd)
        kwin = jnp.concatenate([kp_ref[:, ks], kc_ref[:, ks], kn_ref[:, ks]], axis=0).astype(F32)
        kwin = _rms(kwin, kg_ref[...]).astype(BF16)
        vwin = jnp.concatenate([vp_ref[:, ks], vc_ref[:, ks], vn_ref[:, ks]], axis=0)
        qs = jnp.concatenate(
            [q_ref[:, (g * ATT_REP + r) * hd:(g * ATT_REP + r + 1) * hd] for r in range(ATT_REP)],
            axis=0).astype(F32)
        qs = (_rms(qs, qg_ref[...]) * scale).astype(BF16)
        s = lax.dot_general(qs, kwin, (((1,), (1,)), ((), ())), preferred_element_type=F32)
        bias = jnp.concatenate([bias_ref[g * ATT_REP + r] for r in range(ATT_REP)], axis=0)
        s = s + bias + edge
        sink = jnp.concatenate(
            [jnp.full((T, 1), sink_ref[g * ATT_REP + r], F32) for r in range(ATT_REP)], axis=0)
        m = jnp.maximum(jnp.max(s, axis=-1, keepdims=True), sink)
        p = jnp.exp(s - m)
        denom = jnp.sum(p, axis=-1, keepdims=True) + jnp.exp(sink - m)
        o = _dot(p.astype(BF16), vwin) / denom
        for r in range(ATT_REP):
            hq = g * ATT_REP + r
            o_ref[:, hq * hd:(hq + 1) * hd] = o[r * T:(r + 1) * T, :].astype(BF16)


def _attn(u, bias, q_gain, k_gain, sink, seq):
    n = u.shape[0]
    nblk = n // ATT_BLOCK
    blocks_per_seq = seq // ATT_BLOCK
    kcol = U_K // KV_DIM
    vcol = U_V // KV_DIM
    prev = lambda i: jnp.maximum(i - 1, 0)
    nxt = lambda i: jnp.minimum(i + 1, nblk - 1)
    kv_spec = lambda rowf, c: pl.BlockSpec((ATT_BLOCK, KV_DIM), lambda i: (rowf(i), c))
    same = lambda i: i
    return pl.pallas_call(
        functools.partial(_attn_kernel, blocks_per_seq=blocks_per_seq),
        grid=(nblk,),
        in_specs=[
            pl.BlockSpec((ATT_BLOCK, D_MODEL), lambda i: (i, U_Q // D_MODEL)),
            kv_spec(prev, kcol), kv_spec(same, kcol), kv_spec(nxt, kcol),
            kv_spec(prev, vcol), kv_spec(same, vcol), kv_spec(nxt, vcol),
            pl.BlockSpec((ATT_Q_HEADS, ATT_BLOCK, 3 * ATT_BLOCK), lambda i: (0, 0, 0)),
            pl.BlockSpec((1, ATT_HEAD_DIM), lambda i: (0, 0)),
            pl.BlockSpec((1, ATT_HEAD_DIM), lambda i: (0, 0)),
            pl.BlockSpec(memory_space=pltpu.SMEM),
        ],
        out_specs=pl.BlockSpec((ATT_BLOCK, D_MODEL), lambda i: (i, 0)),
        out_shape=jax.ShapeDtypeStruct((n, D_MODEL), BF16),
        compiler_params=_params(("parallel",)),
        name="band_attn",
    )(u, u, u, u, u, u, u, bias, q_gain, k_gain, sink)


BR_TM = 512
BR_TN = 512


def _branch_kernel(ys_ref, ya_ref, ga_ref, gb_ref, ng_ref, ws_ref, wa_ref, o_ref, yn_scr):
    @pl.when(pl.program_id(1) == 0)
    def _():
        yn_scr[...] = _rms(ys_ref[...].astype(F32), ng_ref[...]).astype(BF16)

    a = jax.nn.sigmoid(ga_ref[...].astype(F32)) * _dot(yn_scr[...], ws_ref[...])
    b = jax.nn.sigmoid(gb_ref[...].astype(F32)) * _dot(ya_ref[...], wa_ref[...])
    o_ref[...] = (a + b).astype(BF16)


def _branch(ys, ya, u, norm_gain, w_ssd, w_att):
    n = ys.shape[0]
    return pl.pallas_call(
        _branch_kernel,
        grid=(n // BR_TM, D_MODEL // BR_TN),
        in_specs=[
            pl.BlockSpec((BR_TM, D_INNER), lambda i, j: (i, 0)),
            pl.BlockSpec((BR_TM, D_MODEL), lambda i, j: (i, 0)),
            pl.BlockSpec((BR_TM, BR_TN), lambda i, j: (i, U_GA // BR_TN + j)),
            pl.BlockSpec((BR_TM, BR_TN), lambda i, j: (i, U_GB // BR_TN + j)),
            pl.BlockSpec((1, D_INNER), lambda i, j: (0, 0)),
            pl.BlockSpec((D_INNER, BR_TN), lambda i, j: (0, j)),
            pl.BlockSpec((D_MODEL, BR_TN), lambda i, j: (0, j)),
        ],
        out_specs=pl.BlockSpec((BR_TM, BR_TN), lambda i, j: (i, j)),
        out_shape=jax.ShapeDtypeStruct((n, D_MODEL), BF16),
        scratch_shapes=[pltpu.VMEM((BR_TM, D_INNER), BF16)],
        compiler_params=_params(("parallel", "arbitrary")),
        name="branch_merge",
    )(ys, ya, u, u, norm_gain, w_ssd, w_att)


OUT_TM = 512


def _outproj_kernel(m_ref, w_ref, x_ref, o_ref):
    o_ref[...] = x_ref[...] + _dot(m_ref[...], w_ref[...])


def _outproj(merged, w_out, x1):
    n = merged.shape[0]
    return pl.pallas_call(
        _outproj_kernel,
        grid=(n // OUT_TM,),
        in_specs=[
            pl.BlockSpec((OUT_TM, D_MODEL), lambda i: (i, 0)),
            pl.BlockSpec((D_MODEL, D_MODEL), lambda i: (0, 0)),
            pl.BlockSpec((OUT_TM, D_MODEL), lambda i: (i, 0)),
        ],
        out_specs=pl.BlockSpec((OUT_TM, D_MODEL), lambda i: (i, 0)),
        out_shape=jax.ShapeDtypeStruct((n, D_MODEL), F32),
        compiler_params=_params(("parallel",)),
        name="out_proj",
    )(merged, w_out, x1)


def _prepare(ffn1_norm, ffn1_w_gate, ffn1_w_up, ffn1_w_down, mix_norm, w_in, conv_w, conv_b,
             ssd_A_log, ssd_dt_bias, ssd_D, ssd_out_norm, q_norm, k_norm, attn_sink, rel_bias,
             w_branch_ssd, w_branch_attn, w_out, ffn2_norm, ffn2_w_gate, ffn2_w_up, ffn2_w_down,
             final_norm):
    l = 0
    w = w_in[l]
    o_z = 0
    o_xbc = o_z + D_INNER
    o_dt = o_xbc + CONV_DIM
    o_q = o_dt + 2 * SSD_HEADS
    o_k = o_q + D_MODEL
    o_v = o_k + KV_DIM
    o_ga = o_v + KV_DIM
    o_gb = o_ga + D_MODEL
    w_u = jnp.concatenate(
        [w[:, o_z:o_dt], w[:, o_q:o_k], w[:, o_ga:o_gb + D_MODEL], w[:, o_k:o_ga]], axis=1).astype(BF16)
    row = lambda v: v.reshape(1, -1).astype(F32)
    return dict(
        ffn1=(row(ffn1_norm[l]), ffn1_w_gate[l].astype(BF16), ffn1_w_up[l].astype(BF16),
              ffn1_w_down[l].astype(BF16), row(mix_norm[l])),
        ffn2=(row(ffn2_norm[l]), ffn2_w_gate[l].astype(BF16), ffn2_w_up[l].astype(BF16),
              ffn2_w_down[l].astype(BF16), row(final_norm[l])),
        w_u=w_u, w_dt=w[:, o_dt:o_q].astype(BF16), dt_bias=row(ssd_dt_bias[l]),
        conv_w=conv_w[l].astype(F32), conv_b=row(conv_b[l]),
        a_log=row(ssd_A_log[l]), d_exp=row(jnp.repeat(ssd_D[l], SSD_HEADDIM)),
        e01=jnp.asarray(_ssd_expand_matrix(), BF16), out_norm=row(ssd_out_norm[l]),
        q_gain=row(q_norm[l]), k_gain=row(k_norm[l]), sink=attn_sink[l].astype(F32),
        bias=_bias_table(rel_bias.astype(F32)),
        w_ssd=w_branch_ssd[l].astype(BF16), w_att=w_branch_attn[l].astype(BF16),
        w_out=w_out[l].astype(BF16),
    )


def _layer(x, p):
    batch, seq, _ = x.shape
    x2d = x.reshape(batch * seq, D_MODEL)
    x1, h = _ffn(x2d, *p["ffn1"], final=False)
    u, dt = _inproj(h, p["w_u"], p["w_dt"], p["dt_bias"])
    xc = _conv(u, p["conv_w"], p["conv_b"], seq)
    cum, wdec, ecum, rows = _ssd_prep(dt, p["a_log"])
    ys = _ssd(xc, u, cum, wdec, ecum, rows, p["d_exp"], p["e01"], batch, seq)
    ya = _attn(u, p["bias"], p["q_gain"], p["k_gain"], p["sink"], seq)
    merged = _branch(ys, ya, u, p["out_norm"], p["w_ssd"], p["w_att"])
    x2 = _outproj(merged, p["w_out"], x1)
    (y,) = _ffn(x2, *p["ffn2"], final=True)
    return y.reshape(batch, seq, D_MODEL)


def kernel(x_prompt, x_sample, ffn1_norm, ffn1_w_gate, ffn1_w_up, ffn1_w_down, mix_norm, w_in, conv_w, conv_b, ssd_A_log, ssd_dt_bias, ssd_D, ssd_out_norm, q_norm, k_norm, attn_sink, rel_bias, w_branch_ssd, w_branch_attn, w_out, ffn2_norm, ffn2_w_gate, ffn2_w_up, ffn2_w_down, final_norm):
    p = _prepare(ffn1_norm, ffn1_w_gate, ffn1_w_up, ffn1_w_down, mix_norm, w_in, conv_w, conv_b,
                 ssd_A_log, ssd_dt_bias, ssd_D, ssd_out_norm, q_norm, k_norm, attn_sink, rel_bias,
                 w_branch_ssd, w_branch_attn, w_out, ffn2_norm, ffn2_w_gate, ffn2_w_up, ffn2_w_down,
                 final_norm)
    return (_layer(x_prompt, p), _layer(x_sample, p))
```

```python
import functools
import math

import jax
import jax.numpy as jnp
import numpy as np
from jax import lax
from jax.experimental import pallas as pl
from jax.experimental.pallas import tpu as pltpu

D_MODEL = 2048
D_INNER = 2 * D_MODEL
SSD_HEADDIM = 64
SSD_HEADS = D_INNER // SSD_HEADDIM
SSD_GROUPS = 8
SSD_HPG = SSD_HEADS // SSD_GROUPS
D_STATE = 128
CONV_K = 5
CHUNK = 128
CONV_DIM = D_INNER + 2 * SSD_GROUPS * D_STATE
ATT_HEAD_DIM = 128
ATT_Q_HEADS = D_MODEL // ATT_HEAD_DIM
ATT_KV_HEADS = 4
ATT_REP = ATT_Q_HEADS // ATT_KV_HEADS
WINDOW = 128
ATT_BLOCK = 128
N_BUCKETS = 32
MAX_DIST = 128
D_FF = ((8 * D_MODEL // 3 + 255) // 256) * 256
EPS = 1e-6
GROUP_X = SSD_HPG * SSD_HEADDIM
KV_DIM = ATT_KV_HEADS * ATT_HEAD_DIM

U_Z = 0
U_XBC = U_Z + D_INNER
U_Q = U_XBC + CONV_DIM
U_GA = U_Q + D_MODEL
U_GB = U_GA + D_MODEL
U_K = U_GB + D_MODEL
U_V = U_K + KV_DIM
U_TOTAL = U_V + KV_DIM

V7X_VMEM_LIMIT_BYTES = 56 * 1024 * 1024
NEG_BIG = -1e30

BF16 = jnp.bfloat16
F32 = jnp.float32


def _params(semantics):
    return pltpu.CompilerParams(dimension_semantics=semantics,
                                vmem_limit_bytes=V7X_VMEM_LIMIT_BYTES)


def _rms(x, gain):
    return x * lax.rsqrt(jnp.mean(x * x, axis=-1, keepdims=True) + EPS) * gain


def _silu(x):
    return x * jax.nn.sigmoid(x)


def _dot(a, b):
    return jnp.dot(a, b, preferred_element_type=F32)


FFN_TM = 512
FFN_TF = 512


def _ffn_kernel(x_ref, g_ref, wg_ref, wu_ref, wd_ref, pg_ref, *refs, final):
    if final:
        y_ref, h_scr, acc_scr = refs
    else:
        x1_ref, h_ref, h_scr, acc_scr = refs
    j = pl.program_id(1)

    @pl.when(j == 0)
    def _():
        h_scr[...] = _rms(x_ref[...], g_ref[...]).astype(BF16)
        acc_scr[...] = jnp.zeros_like(acc_scr)

    h = h_scr[...]
    t = _silu(_dot(h, wg_ref[...])) * _dot(h, wu_ref[...])
    acc_scr[...] += _dot(t.astype(BF16), wd_ref[...])

    @pl.when(j == pl.num_programs(1) - 1)
    def _():
        x1 = x_ref[...] + 0.5 * acc_scr[...]
        normed = _rms(x1, pg_ref[...])
        if final:
            y_ref[...] = normed
        else:
            x1_ref[...] = x1
            h_ref[...] = normed.astype(BF16)


def _ffn(x, gain, wg, wu, wd, post_gain, final):
    n = x.shape[0]
    row = lambda i, j: (i, 0)
    fixed = lambda i, j: (0, 0)
    out_shape = [jax.ShapeDtypeStruct((n, D_MODEL), F32)]
    out_specs = [pl.BlockSpec((FFN_TM, D_MODEL), row)]
    if not final:
        out_shape.append(jax.ShapeDtypeStruct((n, D_MODEL), BF16))
        out_specs.append(pl.BlockSpec((FFN_TM, D_MODEL), row))
    return pl.pallas_call(
        functools.partial(_ffn_kernel, final=final),
        grid=(n // FFN_TM, D_FF // FFN_TF),
        in_specs=[
            pl.BlockSpec((FFN_TM, D_MODEL), row),
            pl.BlockSpec((1, D_MODEL), fixed),
            pl.BlockSpec((D_MODEL, FFN_TF), lambda i, j: (0, j)),
            pl.BlockSpec((D_MODEL, FFN_TF), lambda i, j: (0, j)),
            pl.BlockSpec((FFN_TF, D_MODEL), lambda i, j: (j, 0)),
            pl.BlockSpec((1, D_MODEL), fixed),
        ],
        out_specs=out_specs,
        out_shape=out_shape,
        scratch_shapes=[pltpu.VMEM((FFN_TM, D_MODEL), BF16), pltpu.VMEM((FFN_TM, D_MODEL), F32)],
        compiler_params=_params(("parallel", "arbitrary")),
        name="ffn_final" if final else "ffn",
    )(x, gain, wg, wu, wd, post_gain)


INP_TM = 1024
INP_TN = 1024


def _inproj_kernel(h_ref, w_ref, wdt_ref, bdt_ref, qg_ref, kg_ref, u_ref, dt_ref):
    h = h_ref[...]
    acc = _dot(h, w_ref[...])
    j = pl.program_id(1)
    hd = ATT_HEAD_DIM
    heads_per_tile = INP_TN // hd
    is_q = (j >= U_Q // INP_TN) & (j < U_GA // INP_TN)
    is_kv = j == U_K // INP_TN

    def store_heads(gain, n_normed):
        for hh in range(heads_per_tile):
            blk = acc[:, hh * hd:(hh + 1) * hd]
            if hh < n_normed:
                blk = _rms(blk, gain)
            u_ref[:, hh * hd:(hh + 1) * hd] = blk.astype(BF16)

    @pl.when(is_q)
    def _():
        store_heads(qg_ref[...] * (hd ** -0.5 * LOG2E), heads_per_tile)

    @pl.when(is_kv)
    def _():
        store_heads(kg_ref[...], ATT_KV_HEADS)

    @pl.when(jnp.logical_not(is_q | is_kv))
    def _():
        u_ref[...] = acc.astype(BF16)

    @pl.when(j == 0)
    def _():
        raw = _dot(h, wdt_ref[...]) + bdt_ref[...]
        dt_ref[...] = jnp.maximum(raw, 0.0) + jnp.log1p(jnp.exp(-jnp.abs(raw)))


def _inproj(h, w_u, w_dt, dt_bias, q_gain, k_gain):
    assert U_Q % INP_TN == 0 and U_GA % INP_TN == 0 and U_K % INP_TN == 0 and KV_DIM * 2 == INP_TN
    n = h.shape[0]
    ndt = 2 * SSD_HEADS
    return pl.pallas_call(
        _inproj_kernel,
        grid=(n // INP_TM, U_TOTAL // INP_TN),
        in_specs=[
            pl.BlockSpec((INP_TM, D_MODEL), lambda i, j: (i, 0)),
            pl.BlockSpec((D_MODEL, INP_TN), lambda i, j: (0, j)),
            pl.BlockSpec((D_MODEL, ndt), lambda i, j: (0, 0)),
            pl.BlockSpec((1, ndt), lambda i, j: (0, 0)),
            pl.BlockSpec((1, ATT_HEAD_DIM), lambda i, j: (0, 0)),
            pl.BlockSpec((1, ATT_HEAD_DIM), lambda i, j: (0, 0)),
        ],
        out_specs=[
            pl.BlockSpec((INP_TM, INP_TN), lambda i, j: (i, j)),
            pl.BlockSpec((INP_TM, ndt), lambda i, j: (i, 0)),
        ],
        out_shape=[jax.ShapeDtypeStruct((n, U_TOTAL), BF16), jax.ShapeDtypeStruct((n, ndt), F32)],
        compiler_params=_params(("parallel", "arbitrary")),
        name="in_proj",
    )(h, w_u, w_dt, dt_bias, q_gain, k_gain)


CONV_TS = 512
CONV_TC = 512
CONV_HALO = 16
CONV_TB = 32
CONV_WIN = CONV_TB + 2 * CONV_HALO
CONV_SIDE_TAPS = tuple(k for k in range(CONV_K) if k != CONV_K // 2)


def _conv_kernel(xm_ref, xp_ref, xn_ref, w_ref, b_ref, o_ref, xs_scr, pw_scr, *, tiles_per_seq):
    i = pl.program_id(0)
    first = (i % tiles_per_seq) == 0
    last = ((i + 1) % tiles_per_seq) == 0
    H = CONV_HALO
    zero_halo = jnp.zeros((H, CONV_TC), BF16)
    xs_scr[0:H, :] = jnp.where(first, zero_halo, xp_ref[...])
    xs_scr[H:H + CONV_TS, :] = xm_ref[...]
    xs_scr[H + CONV_TS:, :] = jnp.where(last, zero_halo, xn_ref[...])
    xs = xs_scr[...]
    for n, k in enumerate(CONV_SIDE_TAPS):
        pw_scr[n] = xs * w_ref[k:k + 1, :].astype(BF16)
    r = lax.broadcasted_iota(jnp.int32, (CONV_TB, len(CONV_SIDE_TAPS) * CONV_WIN), 0)
    c = lax.broadcasted_iota(jnp.int32, (CONV_TB, len(CONV_SIDE_TAPS) * CONV_WIN), 1)
    hit = None
    for n, k in enumerate(CONV_SIDE_TAPS):
        sel = c == r + (n * CONV_WIN + H + k - CONV_K // 2)
        hit = sel if hit is None else hit | sel
    shift = jnp.where(hit, 1.0, 0.0).astype(BF16)
    w_mid = w_ref[CONV_K // 2:CONV_K // 2 + 1, :]
    bias = b_ref[...]
    for blk in range(CONV_TS // CONV_TB):
        t0 = blk * CONV_TB
        taps = jnp.concatenate([pw_scr[n, t0:t0 + CONV_WIN, :] for n in range(len(CONV_SIDE_TAPS))], axis=0)
        mid = xs_scr[H + t0:H + t0 + CONV_TB, :].astype(F32) * w_mid + bias
        o_ref[t0:t0 + CONV_TB, :] = _silu(_dot(shift, taps) + mid).astype(BF16)


def _conv(u, conv_w, conv_b, seq):
    n = u.shape[0]
    tiles_per_seq = seq // CONV_TS
    halo_per_tile = CONV_TS // CONV_HALO
    n_halo = n // CONV_HALO
    col0 = U_XBC // CONV_TC
    return pl.pallas_call(
        functools.partial(_conv_kernel, tiles_per_seq=tiles_per_seq),
        grid=(n // CONV_TS, CONV_DIM // CONV_TC),
        in_specs=[
            pl.BlockSpec((CONV_TS, CONV_TC), lambda i, j: (i, col0 + j)),
            pl.BlockSpec((CONV_HALO, CONV_TC),
                         lambda i, j: (jnp.maximum(i * halo_per_tile - 1, 0), col0 + j)),
            pl.BlockSpec((CONV_HALO, CONV_TC),
                         lambda i, j: (jnp.minimum((i + 1) * halo_per_tile, n_halo - 1), col0 + j)),
            pl.BlockSpec((CONV_K, CONV_TC), lambda i, j: (0, j)),
            pl.BlockSpec((1, CONV_TC), lambda i, j: (0, j)),
        ],
        out_specs=pl.BlockSpec((CONV_TS, CONV_TC), lambda i, j: (i, j)),
        out_shape=jax.ShapeDtypeStruct((n, CONV_DIM), BF16),
        scratch_shapes=[pltpu.VMEM((CONV_TS + 2 * CONV_HALO, CONV_TC), BF16),
                        pltpu.VMEM((len(CONV_SIDE_TAPS), CONV_TS + 2 * CONV_HALO, CONV_TC), BF16)],
        compiler_params=_params(("parallel", "parallel")),
        name="conv",
    )(u, u, u, conv_w, conv_b)


def _split3(a):
    hi = a.astype(BF16)
    r1 = a - hi.astype(F32)
    mid = r1.astype(BF16)
    lo = (r1 - mid.astype(F32)).astype(BF16)
    return hi, mid, lo


def _exact_lhs01(m01, a):
    hi, mid, lo = _split3(a)
    return _dot(jnp.concatenate([m01, m01, m01], axis=1), jnp.concatenate([hi, mid, lo], axis=0))


def _split2(a):
    hi = a.astype(BF16)
    mid = (a - hi.astype(F32)).astype(BF16)
    return jnp.concatenate([hi, mid], axis=1)


PREP_CHUNKS = 8
ROWS_PER_GROUP = 4 * SSD_HPG
LOG2E = 1.4426950408889634


def _ssd_prep_kernel(dt_ref, a_ref, cum_ref, wdec_ref, ecum_ref, rows_ref):
    L = CHUNK
    H = SSD_HPG
    row = lax.broadcasted_iota(jnp.int32, (L, L), 0)
    col = lax.broadcasted_iota(jnp.int32, (L, L), 1)
    lower = (row >= col).astype(BF16)
    upper = (row <= col).astype(BF16)
    is_fwd = col < SSD_HEADS
    a = -jnp.exp(a_ref[...])
    for t in range(PREP_CHUNKS):
        sl = slice(t * L, (t + 1) * L)
        dt = dt_ref[sl, :]
        dA = dt * a
        pre = _exact_lhs01(lower, dA)
        suf = _exact_lhs01(upper, dA)
        cum = jnp.where(is_fwd, pre, suf)
        tot = jnp.where(is_fwd[:1], pre[L - 1:L, :], suf[0:1, :])
        cum_ref[sl, :] = cum * LOG2E
        wdec_ref[sl, :] = _split2(jnp.exp(tot - cum) * dt)
        ecum_ref[sl, :] = _split2(jnp.exp(cum))
        dt_t = dt.T
        src_t = (cum.T - jnp.log(dt_t)) * LOG2E
        lds_t = jnp.log2(dt_t[:SSD_HEADS] + dt_t[SSD_HEADS:])
        etot = jnp.broadcast_to(jnp.exp(tot), (H, L))
        for g in range(SSD_GROUPS):
            f = slice(g * H, (g + 1) * H)
            b = slice(SSD_HEADS + g * H, SSD_HEADS + (g + 1) * H)
            rows_ref[t, g] = jnp.concatenate([src_t[f], src_t[b], lds_t[f], etot], axis=0)


def _ssd_prep(dt, a_log):
    n = dt.shape[0]
    rows_blk = PREP_CHUNKS * CHUNK
    nh = 2 * SSD_HEADS
    return pl.pallas_call(
        _ssd_prep_kernel,
        grid=(n // rows_blk,),
        in_specs=[pl.BlockSpec((rows_blk, nh), lambda i: (i, 0)), pl.BlockSpec((1, nh), lambda i: (0, 0))],
        out_specs=[pl.BlockSpec((rows_blk, nh), lambda i: (i, 0)),
                   pl.BlockSpec((rows_blk, 2 * nh), lambda i: (i, 0)),
                   pl.BlockSpec((rows_blk, 2 * nh), lambda i: (i, 0)),
                   pl.BlockSpec((PREP_CHUNKS, SSD_GROUPS, ROWS_PER_GROUP, CHUNK), lambda i: (i, 0, 0, 0))],
        out_shape=[jax.ShapeDtypeStruct((n, nh), F32),
                   jax.ShapeDtypeStruct((n, 2 * nh), BF16),
                   jax.ShapeDtypeStruct((n, 2 * nh), BF16),
                   jax.ShapeDtypeStruct((n // CHUNK, SSD_GROUPS, ROWS_PER_GROUP, CHUNK), F32)],
        compiler_params=_params(("parallel",)),
        name="ssd_prep",
    )(dt, a_log)


SSD_T = 8


def _ssd_kernel(x_ref, b_ref, c_ref, z_ref, cum_ref, wdec_ref, ecum_ref, rows_ref, d_ref, e_ref,
                y_ref, gst_scr, h_scr):
    g = pl.program_id(1)
    phase = pl.program_id(2)
    step = pl.program_id(3)
    ns = pl.num_programs(3)
    L = CHUNK
    GX = GROUP_X
    H = SSD_HPG

    @pl.when(step == 0)
    def _():
        h_scr[...] = jnp.zeros_like(h_scr)

    def state_update(t, direction, xf, bm):
        sl = slice(t * L, (t + 1) * L)
        e_dir = e_ref[0, :, direction * GX:(direction + 1) * GX]
        w = _dot(wdec_ref[sl, :], e_dir)
        dec = _dot(_split2(rows_ref[t, 0, 3 * H:4 * H, :]), e_dir)[0:1, :]
        xdec = (w * xf).astype(BF16)
        b_t = bm.astype(F32).T.astype(BF16)
        h_scr[...] = dec * h_scr[...] + _dot(b_t, xdec)

    @pl.when(phase == 0)
    def _():
        for t in reversed(range(SSD_T)):
            sl = slice(t * L, (t + 1) * L)
            gst_scr[(ns - 1 - step) * SSD_T + t] = h_scr[...].astype(BF16)
            state_update(t, 1, x_ref[sl, :].astype(F32), b_ref[sl, :])

    @pl.when(phase == 1)
    def _():
        row = lax.broadcasted_iota(jnp.int32, (L, L), 0)
        col = lax.broadcasted_iota(jnp.int32, (L, L), 1)
        lower = row >= col
        diag = row == col
        low_lane = col < SSD_HEADDIM
        shift = (2 * SSD_HEADS - H * g) % (2 * SSD_HEADS)
        for t in range(SSD_T):
            sl = slice(t * L, (t + 1) * L)
            cm = c_ref[sl, :]
            bm = b_ref[sl, :]
            xb = x_ref[sl, :]
            xf = xb.astype(F32)
            cb = lax.dot_general(cm, bm, (((1,), (1,)), ((), ())), preferred_element_type=F32)
            cum_r = pltpu.roll(cum_ref[sl, :], shift, axis=1)
            pairs = []
            for k in range(H // 2):
                ms = []
                for half in range(2):
                    hh = 2 * k + half
                    fwd = cum_r[:, hh:hh + 1] - rows_ref[t, 0, hh:hh + 1, :]
                    bwd = cum_r[:, SSD_HEADS + hh:SSD_HEADS + hh + 1] - rows_ref[t, 0, H + hh:H + hh + 1, :]
                    arg = jnp.where(diag, rows_ref[t, 0, 2 * H + hh:2 * H + hh + 1, :],
                                    jnp.where(lower, fwd, bwd))
                    ms.append((jnp.exp2(arg) * cb).astype(BF16))
                xp = xb[:, k * L:(k + 1) * L]
                zero = jnp.zeros_like(xp)
                rhs = jnp.concatenate([jnp.where(low_lane, xp, zero), jnp.where(low_lane, zero, xp)], axis=0)
                pairs.append(_dot(jnp.concatenate(ms, axis=1), rhs))
            y = jnp.concatenate(pairs, axis=1)
            c = step * SSD_T + t
            states = jnp.concatenate([h_scr[...].astype(BF16), gst_scr[c]], axis=1)
            y_off = _dot(ecum_ref[sl, :], e_ref[0]) * _dot(cm, states)
            y = y + y_off[:, :GX] + y_off[:, GX:] + d_ref[...] * xf
            y_ref[sl, :] = (y * _silu(z_ref[sl, :].astype(F32))).astype(BF16)
            state_update(t, 0, xf, bm)


def _ssd(xc, u, cum, wdec, ecum, rows, d_exp, e01, batch, seq):
    n = xc.shape[0]
    nc = seq // CHUNK
    ns = nc // SSD_T
    tl = SSD_T * CHUNK
    nh = 2 * SSD_HEADS
    gx_blocks = D_INNER // D_STATE

    def fb_row(b, p, s):
        return b * ns + jnp.where(p == 0, ns - 1 - s, s)

    def f_row(b, p, s):
        return b * ns + jnp.where(p == 0, 0, s)

    return pl.pallas_call(
        _ssd_kernel,
        grid=(batch, SSD_GROUPS, 2, ns),
        in_specs=[
            pl.BlockSpec((tl, GROUP_X), lambda b, g, p, s: (fb_row(b, p, s), g)),
            pl.BlockSpec((tl, D_STATE), lambda b, g, p, s: (fb_row(b, p, s), gx_blocks + g)),
            pl.BlockSpec((tl, D_STATE), lambda b, g, p, s: (f_row(b, p, s), gx_blocks + SSD_GROUPS + g)),
            pl.BlockSpec((tl, GROUP_X), lambda b, g, p, s: (f_row(b, p, s), U_Z // GROUP_X + g)),
            pl.BlockSpec((tl, nh), lambda b, g, p, s: (f_row(b, p, s), 0)),
            pl.BlockSpec((tl, 2 * nh), lambda b, g, p, s: (fb_row(b, p, s), 0)),
            pl.BlockSpec((tl, 2 * nh), lambda b, g, p, s: (f_row(b, p, s), 0)),
            pl.BlockSpec((SSD_T, 1, ROWS_PER_GROUP, CHUNK), lambda b, g, p, s: (fb_row(b, p, s), g, 0, 0)),
            pl.BlockSpec((1, GROUP_X), lambda b, g, p, s: (0, g)),
            pl.BlockSpec((1, 2 * nh, 2 * GROUP_X), lambda b, g, p, s: (g, 0, 0)),
        ],
        out_specs=pl.BlockSpec((tl, GROUP_X), lambda b, g, p, s: (f_row(b, p, s), g)),
        out_shape=jax.ShapeDtypeStruct((n, D_INNER), BF16),
        scratch_shapes=[
            pltpu.VMEM((nc, D_STATE, GROUP_X), BF16),
            pltpu.VMEM((D_STATE, GROUP_X), F32),
        ],
        compiler_params=_params(("arbitrary", "arbitrary", "arbitrary", "arbitrary")),
        name="ssd",
    )(xc, xc, xc, u, cum, wdec, ecum, rows, d_exp, e01)


def _ssd_expand_matrix():
    e = np.zeros((SSD_GROUPS, 2 * SSD_HEADS, 2 * GROUP_X), np.float32)
    for g in range(SSD_GROUPS):
        for d in range(2):
            for r in range(SSD_HPG):
                lo = d * GROUP_X + r * SSD_HEADDIM
                e[g, d * SSD_HEADS + g * SSD_HPG + r, lo:lo + SSD_HEADDIM] = 1.0
    return np.concatenate([e, e], axis=1)


def _t5_bucket(rel):
    nb = N_BUCKETS // 2
    ret = (rel > 0).astype(np.int32) * nb
    n = np.abs(rel)
    max_exact = nb // 2
    large = max_exact + (np.log(np.maximum(n, 1) / max_exact) / math.log(MAX_DIST / max_exact)
                         * (nb - max_exact)).astype(np.int32)
    large = np.minimum(large, nb - 1)
    return (ret + np.where(n < max_exact, n, large)).astype(np.int32)


def _bucket_table():
    i = np.arange(ATT_BLOCK)[:, None]
    j = np.arange(3 * ATT_BLOCK)[None, :]
    rel = j - ATT_BLOCK - i
    return np.where(np.abs(rel) <= WINDOW, _t5_bucket(rel), -1).astype(np.int32)


N_EDGE_VARIANTS = 4


def _bias_kernel(rel_ref, bucket_ref, o_ref):
    variant = pl.program_id(0)
    h = pl.program_id(1)
    bucket = bucket_ref[...]
    acc = jnp.full(bucket.shape, NEG_BIG, F32)
    for b in range(N_BUCKETS):
        acc = jnp.where(bucket == b, rel_ref[b, h] * LOG2E, acc)
    kcol = lax.broadcasted_iota(jnp.int32, bucket.shape, 1)
    dead = (((variant & 1) == 1) & (kcol < ATT_BLOCK)) | (((variant & 2) == 2) & (kcol >= 2 * ATT_BLOCK))
    o_ref[0, 0] = jnp.where(dead, NEG_BIG, acc)


def _bias_table(rel_bias):
    bucket = jnp.asarray(_bucket_table())
    return pl.pallas_call(
        _bias_kernel,
        grid=(N_EDGE_VARIANTS, ATT_Q_HEADS),
        in_specs=[
            pl.BlockSpec(memory_space=pltpu.SMEM),
            pl.BlockSpec((ATT_BLOCK, 3 * ATT_BLOCK), lambda v, h: (0, 0)),
        ],
        out_specs=pl.BlockSpec((1, 1, ATT_BLOCK, 3 * ATT_BLOCK), lambda v, h: (v, h, 0, 0)),
        out_shape=jax.ShapeDtypeStruct((N_EDGE_VARIANTS, ATT_Q_HEADS, ATT_BLOCK, 3 * ATT_BLOCK), F32),
        compiler_params=_params(("arbitrary", "arbitrary")),
        name="rel_bias_table",
    )(rel_bias, bucket)


def _attn_kernel(q_ref, kp_ref, kc_ref, kn_ref, vp_ref, vc_ref, vn_ref, bias_ref, sink_ref, o_ref):
    T = ATT_BLOCK
    hd = ATT_HEAD_DIM
    ones = jnp.ones((3 * T, hd), BF16)
    for g in range(ATT_KV_HEADS):
        ks = slice(g * hd, (g + 1) * hd)
        kwin = jnp.concatenate([kp_ref[:, ks], kc_ref[:, ks], kn_ref[:, ks]], axis=0)
        vwin = jnp.concatenate([vp_ref[:, ks], vc_ref[:, ks], vn_ref[:, ks]], axis=0)
        qs = jnp.concatenate(
            [q_ref[:, (g * ATT_REP + r) * hd:(g * ATT_REP + r + 1) * hd] for r in range(ATT_REP)],
            axis=0)
        s = lax.dot_general(qs, kwin, (((1,), (1,)), ((), ())), preferred_element_type=F32)
        s = s + jnp.concatenate([bias_ref[0, g * ATT_REP + r] for r in range(ATT_REP)], axis=0)
        sink = jnp.concatenate(
            [jnp.full((T, 1), sink_ref[g * ATT_REP + r] * LOG2E, F32) for r in range(ATT_REP)], axis=0)
        m = jnp.maximum(jnp.max(s, axis=-1, keepdims=True), sink)
        p = jnp.exp2(s - m).astype(BF16)
        ov = _dot(p, jnp.concatenate([vwin, ones], axis=1))
        o = ov[:, :hd] / (ov[:, hd:] + jnp.exp2(sink - m))
        for r in range(ATT_REP):
            hq = g * ATT_REP + r
            o_ref[:, hq * hd:(hq + 1) * hd] = o[r * T:(r + 1) * T, :].astype(BF16)


def _attn(u, bias, sink, seq):
    n = u.shape[0]
    nblk = n // ATT_BLOCK
    blocks_per_seq = seq // ATT_BLOCK
    kcol = U_K // KV_DIM
    vcol = U_V // KV_DIM
    prev = lambda i: jnp.maximum(i - 1, 0)
    nxt = lambda i: jnp.minimum(i + 1, nblk - 1)
    kv_spec = lambda rowf, c: pl.BlockSpec((ATT_BLOCK, KV_DIM), lambda i: (rowf(i), c))
    same = lambda i: i

    def edge_variant(i):
        first = (i % blocks_per_seq == 0).astype(jnp.int32)
        last = ((i + 1) % blocks_per_seq == 0).astype(jnp.int32)
        return first + 2 * last

    return pl.pallas_call(
        _attn_kernel,
        grid=(nblk,),
        in_specs=[
            pl.BlockSpec((ATT_BLOCK, D_MODEL), lambda i: (i, U_Q // D_MODEL)),
            kv_spec(prev, kcol), kv_spec(same, kcol), kv_spec(nxt, kcol),
            kv_spec(prev, vcol), kv_spec(same, vcol), kv_spec(nxt, vcol),
            pl.BlockSpec((1, ATT_Q_HEADS, ATT_BLOCK, 3 * ATT_BLOCK), lambda i: (edge_variant(i), 0, 0, 0)),
            pl.BlockSpec(memory_space=pltpu.SMEM),
        ],
        out_specs=pl.BlockSpec((ATT_BLOCK, D_MODEL), lambda i: (i, 0)),
        out_shape=jax.ShapeDtypeStruct((n, D_MODEL), BF16),
        compiler_params=_params(("parallel",)),
        name="band_attn",
    )(u, u, u, u, u, u, u, bias, sink)


BR_TM = 512
BR_TN = 512


def _branch_kernel(ys_ref, ya_ref, ga_ref, gb_ref, ng_ref, ws_ref, wa_ref, o_ref, yn_scr):
    @pl.when(pl.program_id(1) == 0)
    def _():
        yn_scr[...] = _rms(ys_ref[...].astype(F32), ng_ref[...]).astype(BF16)

    a = jax.nn.sigmoid(ga_ref[...].astype(F32)) * _dot(yn_scr[...], ws_ref[...])
    b = jax.nn.sigmoid(gb_ref[...].astype(F32)) * _dot(ya_ref[...], wa_ref[...])
    o_ref[...] = (a + b).astype(BF16)


def _branch(ys, ya, u, norm_gain, w_ssd, w_att):
    n = ys.shape[0]
    return pl.pallas_call(
        _branch_kernel,
        grid=(n // BR_TM, D_MODEL // BR_TN),
        in_specs=[
            pl.BlockSpec((BR_TM, D_INNER), lambda i, j: (i, 0)),
            pl.BlockSpec((BR_TM, D_MODEL), lambda i, j: (i, 0)),
            pl.BlockSpec((BR_TM, BR_TN), lambda i, j: (i, U_GA // BR_TN + j)),
            pl.BlockSpec((BR_TM, BR_TN), lambda i, j: (i, U_GB // BR_TN + j)),
            pl.BlockSpec((1, D_INNER), lambda i, j: (0, 0)),
            pl.BlockSpec((D_INNER, BR_TN), lambda i, j: (0, j)),
            pl.BlockSpec((D_MODEL, BR_TN), lambda i, j: (0, j)),
        ],
        out_specs=pl.BlockSpec((BR_TM, BR_TN), lambda i, j: (i, j)),
        out_shape=jax.ShapeDtypeStruct((n, D_MODEL), BF16),
        scratch_shapes=[pltpu.VMEM((BR_TM, D_INNER), BF16)],
        compiler_params=_params(("parallel", "arbitrary")),
        name="branch_merge",
    )(ys, ya, u, u, norm_gain, w_ssd, w_att)


OUT_TM = 512


def _outproj_kernel(m_ref, w_ref, x_ref, o_ref):
    o_ref[...] = x_ref[...] + _dot(m_ref[...], w_ref[...])


def _outproj(merged, w_out, x1):
    n = merged.shape[0]
    return pl.pallas_call(
        _outproj_kernel,
        grid=(n // OUT_TM,),
        in_specs=[
            pl.BlockSpec((OUT_TM, D_MODEL), lambda i: (i, 0)),
            pl.BlockSpec((D_MODEL, D_MODEL), lambda i: (0, 0)),
            pl.BlockSpec((OUT_TM, D_MODEL), lambda i: (i, 0)),
        ],
        out_specs=pl.BlockSpec((OUT_TM, D_MODEL), lambda i: (i, 0)),
        out_shape=jax.ShapeDtypeStruct((n, D_MODEL), F32),
        compiler_params=_params(("parallel",)),
        name="out_proj",
    )(merged, w_out, x1)


def _prepare(ffn1_norm, ffn1_w_gate, ffn1_w_up, ffn1_w_down, mix_norm, w_in, conv_w, conv_b,
             ssd_A_log, ssd_dt_bias, ssd_D, ssd_out_norm, q_norm, k_norm, attn_sink, rel_bias,
             w_branch_ssd, w_branch_attn, w_out, ffn2_norm, ffn2_w_gate, ffn2_w_up, ffn2_w_down,
             final_norm):
    l = 0
    w = w_in[l]
    o_z = 0
    o_xbc = o_z + D_INNER
    o_dt = o_xbc + CONV_DIM
    o_q = o_dt + 2 * SSD_HEADS
    o_k = o_q + D_MODEL
    o_v = o_k + KV_DIM
    o_ga = o_v + KV_DIM
    o_gb = o_ga + D_MODEL
    w_u = jnp.concatenate(
        [w[:, o_z:o_dt], w[:, o_q:o_k], w[:, o_ga:o_gb + D_MODEL], w[:, o_k:o_ga]], axis=1).astype(BF16)
    row = lambda v: v.reshape(1, -1).astype(F32)
    return dict(
        ffn1=(row(ffn1_norm[l]), ffn1_w_gate[l].astype(BF16), ffn1_w_up[l].astype(BF16),
              ffn1_w_down[l].astype(BF16), row(mix_norm[l])),
        ffn2=(row(ffn2_norm[l]), ffn2_w_gate[l].astype(BF16), ffn2_w_up[l].astype(BF16),
              ffn2_w_down[l].astype(BF16), row(final_norm[l])),
        w_u=w_u, w_dt=w[:, o_dt:o_q].astype(BF16), dt_bias=row(ssd_dt_bias[l]),
        conv_w=conv_w[l].astype(F32), conv_b=row(conv_b[l]),
        a_log=row(ssd_A_log[l]), d_exp=row(jnp.repeat(ssd_D[l], SSD_HEADDIM)),
        e01=jnp.asarray(_ssd_expand_matrix(), BF16), out_norm=row(ssd_out_norm[l]),
        q_gain=row(q_norm[l]), k_gain=row(k_norm[l]), sink=attn_sink[l].astype(F32),
        bias=_bias_table(rel_bias.astype(F32)),
        w_ssd=w_branch_ssd[l].astype(BF16), w_att=w_branch_attn[l].astype(BF16),
        w_out=w_out[l].astype(BF16),
    )


def _layer(x, p):
    batch, seq, _ = x.shape
    x2d = x.reshape(batch * seq, D_MODEL)
    x1, h = _ffn(x2d, *p["ffn1"], final=False)
    u, dt = _inproj(h, p["w_u"], p["w_dt"], p["dt_bias"], p["q_gain"], p["k_gain"])
    xc = _conv(u, p["conv_w"], p["conv_b"], seq)
    cum, wdec, ecum, rows = _ssd_prep(dt, p["a_log"])
    ys = _ssd(xc, u, cum, wdec, ecum, rows, p["d_exp"], p["e01"], batch, seq)
    ya = _attn(u, p["bias"], p["sink"], seq)
    merged = _branch(ys, ya, u, p["out_norm"], p["w_ssd"], p["w_att"])
    x2 = _outproj(merged, p["w_out"], x1)
    (y,) = _ffn(x2, *p["ffn2"], final=True)
    return y.reshape(batch, seq, D_MODEL)


def kernel(x_prompt, x_sample, ffn1_norm, ffn1_w_gate, ffn1_w_up, ffn1_w_down, mix_norm, w_in, conv_w, conv_b, ssd_A_log, ssd_dt_bias, ssd_D, ssd_out_norm, q_norm, k_norm, attn_sink, rel_bias, w_branch_ssd, w_branch_attn, w_out, ffn2_norm, ffn2_w_gate, ffn2_w_up, ffn2_w_down, final_norm):
    p = _prepare(ffn1_norm, ffn1_w_gate, ffn1_w_up, ffn1_w_down, mix_norm, w_in, conv_w, conv_b,
                 ssd_A_log, ssd_dt_bias, ssd_D, ssd_out_norm, q_norm, k_norm, attn_sink, rel_bias,
                 w_branch_ssd, w_branch_attn, w_out, ffn2_norm, ffn2_w_gate, ffn2_w_up, ffn2_w_down,
                 final_norm)
    return (_layer(x_prompt, p), _layer(x_sample, p))
```

```python
import functools
import math

import jax
import jax.numpy as jnp
import numpy as np
from jax import lax
from jax.experimental import pallas as pl
from jax.experimental.pallas import tpu as pltpu

D_MODEL = 2048
D_INNER = 2 * D_MODEL
SSD_HEADDIM = 64
SSD_HEADS = D_INNER // SSD_HEADDIM
SSD_GROUPS = 8
SSD_HPG = SSD_HEADS // SSD_GROUPS
D_STATE = 128
CONV_K = 5
CHUNK = 128
CONV_DIM = D_INNER + 2 * SSD_GROUPS * D_STATE
ATT_HEAD_DIM = 128
ATT_Q_HEADS = D_MODEL // ATT_HEAD_DIM
ATT_KV_HEADS = 4
ATT_REP = ATT_Q_HEADS // ATT_KV_HEADS
WINDOW = 128
ATT_BLOCK = 128
N_BUCKETS = 32
MAX_DIST = 128
D_FF = ((8 * D_MODEL // 3 + 255) // 256) * 256
EPS = 1e-6
GROUP_X = SSD_HPG * SSD_HEADDIM
KV_DIM = ATT_KV_HEADS * ATT_HEAD_DIM

U_Z = 0
U_XBC = U_Z + D_INNER
U_Q = U_XBC + CONV_DIM
U_GA = U_Q + D_MODEL
U_GB = U_GA + D_MODEL
U_K = U_GB + D_MODEL
U_V = U_K + KV_DIM
U_TOTAL = U_V + KV_DIM

V7X_VMEM_LIMIT_BYTES = 56 * 1024 * 1024
NEG_BIG = -1e30

BF16 = jnp.bfloat16
F32 = jnp.float32


def _params(semantics):
    return pltpu.CompilerParams(dimension_semantics=semantics,
                                vmem_limit_bytes=V7X_VMEM_LIMIT_BYTES)


def _rms(x, gain):
    return x * lax.rsqrt(jnp.mean(x * x, axis=-1, keepdims=True) + EPS) * gain


def _silu(x):
    return x * jax.nn.sigmoid(x)


def _dot(a, b):
    return jnp.dot(a, b, preferred_element_type=F32)


FFN_TM = 512
FFN_TF = 512


def _ffn_kernel(x_ref, g_ref, wg_ref, wu_ref, wd_ref, pg_ref, *refs, final):
    if final:
        y_ref, h_scr, acc_scr = refs
    else:
        x1_ref, h_ref, h_scr, acc_scr = refs
    j = pl.program_id(1)

    @pl.when(j == 0)
    def _():
        h_scr[...] = _rms(x_ref[...], g_ref[...]).astype(BF16)
        acc_scr[...] = jnp.zeros_like(acc_scr)

    h = h_scr[...]
    t = _silu(_dot(h, wg_ref[0])) * _dot(h, wu_ref[0])
    acc_scr[...] += _dot(t.astype(BF16), wd_ref[...])

    @pl.when(j == pl.num_programs(1) - 1)
    def _():
        x1 = x_ref[...] + 0.5 * acc_scr[...]
        normed = _rms(x1, pg_ref[...])
        if final:
            y_ref[...] = normed
        else:
            x1_ref[...] = x1
            h_ref[...] = normed.astype(BF16)


def _ffn(x, gain, wg, wu, wd, post_gain, final):
    n = x.shape[0]
    row = lambda i, j: (i, 0)
    fixed = lambda i, j: (0, 0)
    out_shape = [jax.ShapeDtypeStruct((n, D_MODEL), F32)]
    out_specs = [pl.BlockSpec((FFN_TM, D_MODEL), row)]
    if not final:
        out_shape.append(jax.ShapeDtypeStruct((n, D_MODEL), BF16))
        out_specs.append(pl.BlockSpec((FFN_TM, D_MODEL), row))
    return pl.pallas_call(
        functools.partial(_ffn_kernel, final=final),
        grid=(n // FFN_TM, D_FF // FFN_TF),
        in_specs=[
            pl.BlockSpec((FFN_TM, D_MODEL), row),
            pl.BlockSpec((1, D_MODEL), fixed),
            pl.BlockSpec((1, D_MODEL, FFN_TF), lambda i, j: (j, 0, 0)),
            pl.BlockSpec((1, D_MODEL, FFN_TF), lambda i, j: (j, 0, 0)),
            pl.BlockSpec((FFN_TF, D_MODEL), lambda i, j: (j, 0)),
            pl.BlockSpec((1, D_MODEL), fixed),
        ],
        out_specs=out_specs,
        out_shape=out_shape,
        scratch_shapes=[pltpu.VMEM((FFN_TM, D_MODEL), BF16), pltpu.VMEM((FFN_TM, D_MODEL), F32)],
        compiler_params=_params(("parallel", "arbitrary")),
        name="ffn_final" if final else "ffn",
    )(x, gain, wg, wu, wd, post_gain)


INP_TM = 2048
INP_TN = 1024


def _inproj_kernel(h_ref, w_ref, wdt_ref, bdt_ref, qg_ref, kg_ref, u_ref, dt_ref):
    h = h_ref[...]
    acc = _dot(h, w_ref[0])
    j = pl.program_id(1)
    hd = ATT_HEAD_DIM
    heads_per_tile = INP_TN // hd
    is_q = (j >= U_Q // INP_TN) & (j < U_GA // INP_TN)
    is_kv = j == U_K // INP_TN

    def store_heads(gain, n_normed):
        for hh in range(heads_per_tile):
            blk = acc[:, hh * hd:(hh + 1) * hd]
            if hh < n_normed:
                blk = _rms(blk, gain)
            u_ref[:, hh * hd:(hh + 1) * hd] = blk.astype(BF16)

    @pl.when(is_q)
    def _():
        store_heads(qg_ref[...] * (hd ** -0.5 * LOG2E), heads_per_tile)

    @pl.when(is_kv)
    def _():
        store_heads(kg_ref[...], ATT_KV_HEADS)

    @pl.when(jnp.logical_not(is_q | is_kv))
    def _():
        u_ref[...] = acc.astype(BF16)

    @pl.when(j == 0)
    def _():
        raw = _dot(h, wdt_ref[...]) + bdt_ref[...]
        dt_ref[...] = jnp.maximum(raw, 0.0) + jnp.log1p(jnp.exp(-jnp.abs(raw)))


def _inproj(h, w_u, w_dt, dt_bias, q_gain, k_gain):
    assert U_Q % INP_TN == 0 and U_GA % INP_TN == 0 and U_K % INP_TN == 0 and KV_DIM * 2 == INP_TN
    n = h.shape[0]
    ndt = 2 * SSD_HEADS
    return pl.pallas_call(
        _inproj_kernel,
        grid=(n // INP_TM, U_TOTAL // INP_TN),
        in_specs=[
            pl.BlockSpec((INP_TM, D_MODEL), lambda i, j: (i, 0)),
            pl.BlockSpec((1, D_MODEL, INP_TN), lambda i, j: (j, 0, 0)),
            pl.BlockSpec((D_MODEL, ndt), lambda i, j: (0, 0)),
            pl.BlockSpec((1, ndt), lambda i, j: (0, 0)),
            pl.BlockSpec((1, ATT_HEAD_DIM), lambda i, j: (0, 0)),
            pl.BlockSpec((1, ATT_HEAD_DIM), lambda i, j: (0, 0)),
        ],
        out_specs=[
            pl.BlockSpec((INP_TM, INP_TN), lambda i, j: (i, j)),
            pl.BlockSpec((INP_TM, ndt), lambda i, j: (i, 0)),
        ],
        out_shape=[jax.ShapeDtypeStruct((n, U_TOTAL), BF16), jax.ShapeDtypeStruct((n, ndt), F32)],
        compiler_params=_params(("parallel", "arbitrary")),
        name="in_proj",
    )(h, w_u, w_dt, dt_bias, q_gain, k_gain)


CONV_TS = 512
CONV_TC = 2048
CONV_HALO = 16
CONV_TB = 32
CONV_WIN = CONV_TB + 2 * CONV_HALO
CONV_SIDE_TAPS = tuple(k for k in range(CONV_K) if k != CONV_K // 2)


def _conv_kernel(xm_ref, xp_ref, xn_ref, w_ref, b_ref, o_ref, xs_scr, pw_scr, *, tiles_per_seq):
    i = pl.program_id(0)
    first = (i % tiles_per_seq) == 0
    last = ((i + 1) % tiles_per_seq) == 0
    H = CONV_HALO
    zero_halo = jnp.zeros((H, CONV_TC), BF16)
    xs_scr[0:H, :] = jnp.where(first, zero_halo, xp_ref[...])
    xs_scr[H:H + CONV_TS, :] = xm_ref[...]
    xs_scr[H + CONV_TS:, :] = jnp.where(last, zero_halo, xn_ref[...])
    xs = xs_scr[...]
    for n, k in enumerate(CONV_SIDE_TAPS):
        pw_scr[n] = xs * w_ref[k:k + 1, :].astype(BF16)
    r = lax.broadcasted_iota(jnp.int32, (CONV_TB, len(CONV_SIDE_TAPS) * CONV_WIN), 0)
    c = lax.broadcasted_iota(jnp.int32, (CONV_TB, len(CONV_SIDE_TAPS) * CONV_WIN), 1)
    hit = None
    for n, k in enumerate(CONV_SIDE_TAPS):
        sel = c == r + (n * CONV_WIN + H + k - CONV_K // 2)
        hit = sel if hit is None else hit | sel
    shift = jnp.where(hit, 1.0, 0.0).astype(BF16)
    w_mid = w_ref[CONV_K // 2:CONV_K // 2 + 1, :]
    bias = b_ref[...]
    for blk in range(CONV_TS // CONV_TB):
        t0 = blk * CONV_TB
        taps = jnp.concatenate([pw_scr[n, t0:t0 + CONV_WIN, :] for n in range(len(CONV_SIDE_TAPS))], axis=0)
        mid = xs_scr[H + t0:H + t0 + CONV_TB, :].astype(F32) * w_mid + bias
        o_ref[t0:t0 + CONV_TB, :] = _silu(_dot(shift, taps) + mid).astype(BF16)


def _conv(u, conv_w, conv_b, seq):
    n = u.shape[0]
    tiles_per_seq = seq // CONV_TS
    halo_per_tile = CONV_TS // CONV_HALO
    n_halo = n // CONV_HALO
    col0 = U_XBC // CONV_TC
    return pl.pallas_call(
        functools.partial(_conv_kernel, tiles_per_seq=tiles_per_seq),
        grid=(n // CONV_TS, CONV_DIM // CONV_TC),
        in_specs=[
            pl.BlockSpec((CONV_TS, CONV_TC), lambda i, j: (i, col0 + j)),
            pl.BlockSpec((CONV_HALO, CONV_TC),
                         lambda i, j: (jnp.maximum(i * halo_per_tile - 1, 0), col0 + j)),
            pl.BlockSpec((CONV_HALO, CONV_TC),
                         lambda i, j: (jnp.minimum((i + 1) * halo_per_tile, n_halo - 1), col0 + j)),
            pl.BlockSpec((CONV_K, CONV_TC), lambda i, j: (0, j)),
            pl.BlockSpec((1, CONV_TC), lambda i, j: (0, j)),
        ],
        out_specs=pl.BlockSpec((CONV_TS, CONV_TC), lambda i, j: (i, j)),
        out_shape=jax.ShapeDtypeStruct((n, CONV_DIM), BF16),
        scratch_shapes=[pltpu.VMEM((CONV_TS + 2 * CONV_HALO, CONV_TC), BF16),
                        pltpu.VMEM((len(CONV_SIDE_TAPS), CONV_TS + 2 * CONV_HALO, CONV_TC), BF16)],
        compiler_params=_params(("parallel", "parallel")),
        name="conv",
    )(u, u, u, conv_w, conv_b)


def _split3(a):
    hi = a.astype(BF16)
    r1 = a - hi.astype(F32)
    mid = r1.astype(BF16)
    lo = (r1 - mid.astype(F32)).astype(BF16)
    return hi, mid, lo


def _exact_lhs01(m01, a):
    hi, mid, lo = _split3(a)
    return _dot(jnp.concatenate([m01, m01, m01], axis=1), jnp.concatenate([hi, mid, lo], axis=0))


def _split2(a):
    hi = a.astype(BF16)
    mid = (a - hi.astype(F32)).astype(BF16)
    return jnp.concatenate([hi, mid], axis=1)


PREP_CHUNKS = 8
ROWS_PER_GROUP = 4 * SSD_HPG
LOG2E = 1.4426950408889634


def _ssd_prep_kernel(dt_ref, a_ref, cum_ref, wdec_ref, ecum_ref, rows_ref):
    L = CHUNK
    H = SSD_HPG
    row = lax.broadcasted_iota(jnp.int32, (L, L), 0)
    col = lax.broadcasted_iota(jnp.int32, (L, L), 1)
    lower = (row >= col).astype(BF16)
    upper = (row <= col).astype(BF16)
    is_fwd = col < SSD_HEADS
    a = -jnp.exp(a_ref[...])
    for t in range(PREP_CHUNKS):
        sl = slice(t * L, (t + 1) * L)
        dt = dt_ref[sl, :]
        dA = dt * a
        pre = _exact_lhs01(lower, dA)
        suf = _exact_lhs01(upper, dA)
        cum = jnp.where(is_fwd, pre, suf)
        tot = jnp.where(is_fwd[:1], pre[L - 1:L, :], suf[0:1, :])
        cum_ref[sl, :] = cum * LOG2E
        wdec_ref[sl, :] = _split2(jnp.exp(tot - cum) * dt)
        ecum_ref[sl, :] = _split2(jnp.exp(cum))
        dt_t = dt.T
        src_t = (cum.T - jnp.log(dt_t)) * LOG2E
        lds_t = jnp.log2(dt_t[:SSD_HEADS] + dt_t[SSD_HEADS:])
        etot = jnp.broadcast_to(jnp.exp(tot), (H, L))
        for g in range(SSD_GROUPS):
            f = slice(g * H, (g + 1) * H)
            b = slice(SSD_HEADS + g * H, SSD_HEADS + (g + 1) * H)
            rows_ref[t, g] = jnp.concatenate([src_t[f], src_t[b], lds_t[f], etot], axis=0)


def _ssd_prep(dt, a_log):
    n = dt.shape[0]
    rows_blk = PREP_CHUNKS * CHUNK
    nh = 2 * SSD_HEADS
    return pl.pallas_call(
        _ssd_prep_kernel,
        grid=(n // rows_blk,),
        in_specs=[pl.BlockSpec((rows_blk, nh), lambda i: (i, 0)), pl.BlockSpec((1, nh), lambda i: (0, 0))],
        out_specs=[pl.BlockSpec((rows_blk, nh), lambda i: (i, 0)),
                   pl.BlockSpec((rows_blk, 2 * nh), lambda i: (i, 0)),
                   pl.BlockSpec((rows_blk, 2 * nh), lambda i: (i, 0)),
                   pl.BlockSpec((PREP_CHUNKS, SSD_GROUPS, ROWS_PER_GROUP, CHUNK), lambda i: (i, 0, 0, 0))],
        out_shape=[jax.ShapeDtypeStruct((n, nh), F32),
                   jax.ShapeDtypeStruct((n, 2 * nh), BF16),
                   jax.ShapeDtypeStruct((n, 2 * nh), BF16),
                   jax.ShapeDtypeStruct((n // CHUNK, SSD_GROUPS, ROWS_PER_GROUP, CHUNK), F32)],
        compiler_params=_params(("parallel",)),
        name="ssd_prep",
    )(dt, a_log)


SSD_T = 8


def _ssd_kernel(x_ref, b_ref, c_ref, z_ref, cum_ref, wdec_ref, ecum_ref, rows_ref, d_ref, e_ref,
                y_ref, gst_scr, h_scr):
    g = pl.program_id(1)
    phase = pl.program_id(2)
    step = pl.program_id(3)
    ns = pl.num_programs(3)
    L = CHUNK
    GX = GROUP_X
    H = SSD_HPG

    @pl.when(step == 0)
    def _():
        h_scr[...] = jnp.zeros_like(h_scr)

    def state_update(t, direction, xf, bm):
        sl = slice(t * L, (t + 1) * L)
        e_dir = e_ref[0, :, direction * GX:(direction + 1) * GX]
        w = _dot(wdec_ref[sl, :], e_dir)
        dec = _dot(_split2(rows_ref[t, 0, 3 * H:4 * H, :]), e_dir)[0:1, :]
        xdec = (w * xf).astype(BF16)
        b_t = bm.astype(F32).T.astype(BF16)
        h_scr[...] = dec * h_scr[...] + _dot(b_t, xdec)

    @pl.when(phase == 0)
    def _():
        for t in reversed(range(SSD_T)):
            sl = slice(t * L, (t + 1) * L)
            gst_scr[(ns - 1 - step) * SSD_T + t] = h_scr[...].astype(BF16)
            state_update(t, 1, x_ref[sl, :].astype(F32), b_ref[sl, :])

    @pl.when(phase == 1)
    def _():
        row = lax.broadcasted_iota(jnp.int32, (L, L), 0)
        col = lax.broadcasted_iota(jnp.int32, (L, L), 1)
        lower = row >= col
        diag = row == col
        low_lane = col < SSD_HEADDIM
        shift = (2 * SSD_HEADS - H * g) % (2 * SSD_HEADS)
        for t in range(SSD_T):
            sl = slice(t * L, (t + 1) * L)
            cm = c_ref[sl, :]
            bm = b_ref[sl, :]
            xb = x_ref[sl, :]
            xf = xb.astype(F32)
            cb = lax.dot_general(cm, bm, (((1,), (1,)), ((), ())), preferred_element_type=F32)
            cum_r = pltpu.roll(cum_ref[sl, :], shift, axis=1)
            pairs = []
            for k in range(H // 2):
                ms = []
                for half in range(2):
                    hh = 2 * k + half
                    fwd = cum_r[:, hh:hh + 1] - rows_ref[t, 0, hh:hh + 1, :]
                    bwd = cum_r[:, SSD_HEADS + hh:SSD_HEADS + hh + 1] - rows_ref[t, 0, H + hh:H + hh + 1, :]
                    arg = jnp.where(diag, rows_ref[t, 0, 2 * H + hh:2 * H + hh + 1, :],
                                    jnp.where(lower, fwd, bwd))
                    ms.append((jnp.exp2(arg) * cb).astype(BF16))
                xp = xb[:, k * L:(k + 1) * L]
                zero = jnp.zeros_like(xp)
                rhs = jnp.concatenate([jnp.where(low_lane, xp, zero), jnp.where(low_lane, zero, xp)], axis=0)
                pairs.append(_dot(jnp.concatenate(ms, axis=1), rhs))
            y = jnp.concatenate(pairs, axis=1)
            c = step * SSD_T + t
            states = jnp.concatenate([h_scr[...].astype(BF16), gst_scr[c]], axis=1)
            y_off = _dot(ecum_ref[sl, :], e_ref[0]) * _dot(cm, states)
            y = y + y_off[:, :GX] + y_off[:, GX:] + d_ref[...] * xf
            y_ref[sl, :] = (y * _silu(z_ref[sl, :].astype(F32))).astype(BF16)
            state_update(t, 0, xf, bm)


def _ssd(xc, u, cum, wdec, ecum, rows, d_exp, e01, batch, seq):
    n = xc.shape[0]
    nc = seq // CHUNK
    ns = nc // SSD_T
    tl = SSD_T * CHUNK
    nh = 2 * SSD_HEADS
    gx_blocks = D_INNER // D_STATE

    def fb_row(b, p, s):
        return b * ns + jnp.where(p == 0, ns - 1 - s, s)

    def f_row(b, p, s):
        return b * ns + jnp.where(p == 0, 0, s)

    return pl.pallas_call(
        _ssd_kernel,
        grid=(batch, SSD_GROUPS, 2, ns),
        in_specs=[
            pl.BlockSpec((tl, GROUP_X), lambda b, g, p, s: (fb_row(b, p, s), g)),
            pl.BlockSpec((tl, D_STATE), lambda b, g, p, s: (fb_row(b, p, s), gx_blocks + g)),
            pl.BlockSpec((tl, D_STATE), lambda b, g, p, s: (f_row(b, p, s), gx_blocks + SSD_GROUPS + g)),
            pl.BlockSpec((tl, GROUP_X), lambda b, g, p, s: (f_row(b, p, s), U_Z // GROUP_X + g)),
            pl.BlockSpec((tl, nh), lambda b, g, p, s: (f_row(b, p, s), 0)),
            pl.BlockSpec((tl, 2 * nh), lambda b, g, p, s: (fb_row(b, p, s), 0)),
            pl.BlockSpec((tl, 2 * nh), lambda b, g, p, s: (f_row(b, p, s), 0)),
            pl.BlockSpec((SSD_T, 1, ROWS_PER_GROUP, CHUNK), lambda b, g, p, s: (fb_row(b, p, s), g, 0, 0)),
            pl.BlockSpec((1, GROUP_X), lambda b, g, p, s: (0, g)),
            pl.BlockSpec((1, 2 * nh, 2 * GROUP_X), lambda b, g, p, s: (g, 0, 0)),
        ],
        out_specs=pl.BlockSpec((tl, GROUP_X), lambda b, g, p, s: (f_row(b, p, s), g)),
        out_shape=jax.ShapeDtypeStruct((n, D_INNER), BF16),
        scratch_shapes=[
            pltpu.VMEM((nc, D_STATE, GROUP_X), BF16),
            pltpu.VMEM((D_STATE, GROUP_X), F32),
        ],
        compiler_params=_params(("arbitrary", "arbitrary", "arbitrary", "arbitrary")),
        name="ssd",
    )(xc, xc, xc, u, cum, wdec, ecum, rows, d_exp, e01)


def _ssd_expand_matrix():
    e = np.zeros((SSD_GROUPS, 2 * SSD_HEADS, 2 * GROUP_X), np.float32)
    for g in range(SSD_GROUPS):
        for d in range(2):
            for r in range(SSD_HPG):
                lo = d * GROUP_X + r * SSD_HEADDIM
                e[g, d * SSD_HEADS + g * SSD_HPG + r, lo:lo + SSD_HEADDIM] = 1.0
    return np.concatenate([e, e], axis=1)


def _t5_bucket(rel):
    nb = N_BUCKETS // 2
    ret = (rel > 0).astype(np.int32) * nb
    n = np.abs(rel)
    max_exact = nb // 2
    large = max_exact + (np.log(np.maximum(n, 1) / max_exact) / math.log(MAX_DIST / max_exact)
                         * (nb - max_exact)).astype(np.int32)
    large = np.minimum(large, nb - 1)
    return (ret + np.where(n < max_exact, n, large)).astype(np.int32)


def _bucket_table():
    i = np.arange(ATT_BLOCK)[:, None]
    j = np.arange(3 * ATT_BLOCK)[None, :]
    rel = j - ATT_BLOCK - i
    return np.where(np.abs(rel) <= WINDOW, _t5_bucket(rel), -1).astype(np.int32)


N_EDGE_VARIANTS = 4


def _bias_kernel(rel_ref, bucket_ref, o_ref):
    variant = pl.program_id(0)
    h = pl.program_id(1)
    bucket = bucket_ref[...]
    acc = jnp.full(bucket.shape, NEG_BIG, F32)
    for b in range(N_BUCKETS):
        acc = jnp.where(bucket == b, rel_ref[b, h] * LOG2E, acc)
    kcol = lax.broadcasted_iota(jnp.int32, bucket.shape, 1)
    dead = (((variant & 1) == 1) & (kcol < ATT_BLOCK)) | (((variant & 2) == 2) & (kcol >= 2 * ATT_BLOCK))
    o_ref[0, 0] = jnp.where(dead, NEG_BIG, acc)


def _bias_table(rel_bias):
    bucket = jnp.asarray(_bucket_table())
    return pl.pallas_call(
        _bias_kernel,
        grid=(N_EDGE_VARIANTS, ATT_Q_HEADS),
        in_specs=[
            pl.BlockSpec(memory_space=pltpu.SMEM),
            pl.BlockSpec((ATT_BLOCK, 3 * ATT_BLOCK), lambda v, h: (0, 0)),
        ],
        out_specs=pl.BlockSpec((1, 1, ATT_BLOCK, 3 * ATT_BLOCK), lambda v, h: (v, h, 0, 0)),
        out_shape=jax.ShapeDtypeStruct((N_EDGE_VARIANTS, ATT_Q_HEADS, ATT_BLOCK, 3 * ATT_BLOCK), F32),
        compiler_params=_params(("arbitrary", "arbitrary")),
        name="rel_bias_table",
    )(rel_bias, bucket)


def _attn_kernel(q_ref, kp_ref, kc_ref, kn_ref, vp_ref, vc_ref, vn_ref, bias_ref, sink_ref, o_ref):
    T = ATT_BLOCK
    hd = ATT_HEAD_DIM
    ones = jnp.ones((3 * T, hd), BF16)
    for g in range(ATT_KV_HEADS):
        ks = slice(g * hd, (g + 1) * hd)
        kwin = jnp.concatenate([kp_ref[:, ks], kc_ref[:, ks], kn_ref[:, ks]], axis=0)
        vwin = jnp.concatenate([vp_ref[:, ks], vc_ref[:, ks], vn_ref[:, ks]], axis=0)
        qs = jnp.concatenate(
            [q_ref[:, (g * ATT_REP + r) * hd:(g * ATT_REP + r + 1) * hd] for r in range(ATT_REP)],
            axis=0)
        s = lax.dot_general(qs, kwin, (((1,), (1,)), ((), ())), preferred_element_type=F32)
        s = s + jnp.concatenate([bias_ref[0, g * ATT_REP + r] for r in range(ATT_REP)], axis=0)
        sink = jnp.concatenate(
            [jnp.full((T, 1), sink_ref[g * ATT_REP + r] * LOG2E, F32) for r in range(ATT_REP)], axis=0)
        m = jnp.maximum(jnp.max(s, axis=-1, keepdims=True), sink)
        p = jnp.exp2(s - m).astype(BF16)
        ov = _dot(p, jnp.concatenate([vwin, ones], axis=1))
        o = ov[:, :hd] / (ov[:, hd:] + jnp.exp2(sink - m))
        for r in range(ATT_REP):
            hq = g * ATT_REP + r
            o_ref[:, hq * hd:(hq + 1) * hd] = o[r * T:(r + 1) * T, :].astype(BF16)


def _attn(u, bias, sink, seq):
    n = u.shape[0]
    nblk = n // ATT_BLOCK
    blocks_per_seq = seq // ATT_BLOCK
    kcol = U_K // KV_DIM
    vcol = U_V // KV_DIM
    prev = lambda i: jnp.maximum(i - 1, 0)
    nxt = lambda i: jnp.minimum(i + 1, nblk - 1)
    kv_spec = lambda rowf, c: pl.BlockSpec((ATT_BLOCK, KV_DIM), lambda i: (rowf(i), c))
    same = lambda i: i

    def edge_variant(i):
        first = (i % blocks_per_seq == 0).astype(jnp.int32)
        last = ((i + 1) % blocks_per_seq == 0).astype(jnp.int32)
        return first + 2 * last

    return pl.pallas_call(
        _attn_kernel,
        grid=(nblk,),
        in_specs=[
            pl.BlockSpec((ATT_BLOCK, D_MODEL), lambda i: (i, U_Q // D_MODEL)),
            kv_spec(prev, kcol), kv_spec(same, kcol), kv_spec(nxt, kcol),
            kv_spec(prev, vcol), kv_spec(same, vcol), kv_spec(nxt, vcol),
            pl.BlockSpec((1, ATT_Q_HEADS, ATT_BLOCK, 3 * ATT_BLOCK), lambda i: (edge_variant(i), 0, 0, 0)),
            pl.BlockSpec(memory_space=pltpu.SMEM),
        ],
        out_specs=pl.BlockSpec((ATT_BLOCK, D_MODEL), lambda i: (i, 0)),
        out_shape=jax.ShapeDtypeStruct((n, D_MODEL), BF16),
        compiler_params=_params(("parallel",)),
        name="band_attn",
    )(u, u, u, u, u, u, u, bias, sink)


BR_TM = 512
BR_TN = 512


def _branch_kernel(ys_ref, ya_ref, ga_ref, gb_ref, ng_ref, ws_ref, wa_ref, o_ref, yn_scr):
    @pl.when(pl.program_id(1) == 0)
    def _():
        yn_scr[...] = _rms(ys_ref[...].astype(F32), ng_ref[...]).astype(BF16)

    a = jax.nn.sigmoid(ga_ref[...].astype(F32)) * _dot(yn_scr[...], ws_ref[0])
    b = jax.nn.sigmoid(gb_ref[...].astype(F32)) * _dot(ya_ref[...], wa_ref[0])
    o_ref[...] = (a + b).astype(BF16)


def _branch(ys, ya, u, norm_gain, w_ssd, w_att):
    n = ys.shape[0]
    return pl.pallas_call(
        _branch_kernel,
        grid=(n // BR_TM, D_MODEL // BR_TN),
        in_specs=[
            pl.BlockSpec((BR_TM, D_INNER), lambda i, j: (i, 0)),
            pl.BlockSpec((BR_TM, D_MODEL), lambda i, j: (i, 0)),
            pl.BlockSpec((BR_TM, BR_TN), lambda i, j: (i, U_GA // BR_TN + j)),
            pl.BlockSpec((BR_TM, BR_TN), lambda i, j: (i, U_GB // BR_TN + j)),
            pl.BlockSpec((1, D_INNER), lambda i, j: (0, 0)),
            pl.BlockSpec((1, D_INNER, BR_TN), lambda i, j: (j, 0, 0)),
            pl.BlockSpec((1, D_MODEL, BR_TN), lambda i, j: (j, 0, 0)),
        ],
        out_specs=pl.BlockSpec((BR_TM, BR_TN), lambda i, j: (i, j)),
        out_shape=jax.ShapeDtypeStruct((n, D_MODEL), BF16),
        scratch_shapes=[pltpu.VMEM((BR_TM, D_INNER), BF16)],
        compiler_params=_params(("parallel", "arbitrary")),
        name="branch_merge",
    )(ys, ya, u, u, norm_gain, w_ssd, w_att)


OUT_TM = 512


def _outproj_kernel(m_ref, w_ref, x_ref, o_ref):
    o_ref[...] = x_ref[...] + _dot(m_ref[...], w_ref[...])


def _outproj(merged, w_out, x1):
    n = merged.shape[0]
    return pl.pallas_call(
        _outproj_kernel,
        grid=(n // OUT_TM,),
        in_specs=[
            pl.BlockSpec((OUT_TM, D_MODEL), lambda i: (i, 0)),
            pl.BlockSpec((D_MODEL, D_MODEL), lambda i: (0, 0)),
            pl.BlockSpec((OUT_TM, D_MODEL), lambda i: (i, 0)),
        ],
        out_specs=pl.BlockSpec((OUT_TM, D_MODEL), lambda i: (i, 0)),
        out_shape=jax.ShapeDtypeStruct((n, D_MODEL), F32),
        compiler_params=_params(("parallel",)),
        name="out_proj",
    )(merged, w_out, x1)


def _prepare(ffn1_norm, ffn1_w_gate, ffn1_w_up, ffn1_w_down, mix_norm, w_in, conv_w, conv_b,
             ssd_A_log, ssd_dt_bias, ssd_D, ssd_out_norm, q_norm, k_norm, attn_sink, rel_bias,
             w_branch_ssd, w_branch_attn, w_out, ffn2_norm, ffn2_w_gate, ffn2_w_up, ffn2_w_down,
             final_norm):
    l = 0
    w = w_in[l]
    o_z = 0
    o_xbc = o_z + D_INNER
    o_dt = o_xbc + CONV_DIM
    o_q = o_dt + 2 * SSD_HEADS
    o_k = o_q + D_MODEL
    o_v = o_k + KV_DIM
    o_ga = o_v + KV_DIM
    o_gb = o_ga + D_MODEL
    w_u = jnp.concatenate(
        [w[:, o_z:o_dt], w[:, o_q:o_k], w[:, o_ga:o_gb + D_MODEL], w[:, o_k:o_ga]], axis=1).astype(BF16)
    row = lambda v: v.reshape(1, -1).astype(F32)
    def col_tiles(m, tn):
        k, n = m.shape
        return m.astype(BF16).reshape(k, n // tn, tn).transpose(1, 0, 2)

    return dict(
        ffn1=(row(ffn1_norm[l]), col_tiles(ffn1_w_gate[l], FFN_TF), col_tiles(ffn1_w_up[l], FFN_TF),
              ffn1_w_down[l].astype(BF16), row(mix_norm[l])),
        ffn2=(row(ffn2_norm[l]), col_tiles(ffn2_w_gate[l], FFN_TF), col_tiles(ffn2_w_up[l], FFN_TF),
              ffn2_w_down[l].astype(BF16), row(final_norm[l])),
        w_u=col_tiles(w_u, INP_TN), w_dt=w[:, o_dt:o_q].astype(BF16), dt_bias=row(ssd_dt_bias[l]),
        conv_w=conv_w[l].astype(F32), conv_b=row(conv_b[l]),
        a_log=row(ssd_A_log[l]), d_exp=row(jnp.repeat(ssd_D[l], SSD_HEADDIM)),
        e01=jnp.asarray(_ssd_expand_matrix(), BF16), out_norm=row(ssd_out_norm[l]),
        q_gain=row(q_norm[l]), k_gain=row(k_norm[l]), sink=attn_sink[l].astype(F32),
        bias=_bias_table(rel_bias.astype(F32)),
        w_ssd=col_tiles(w_branch_ssd[l], BR_TN), w_att=col_tiles(w_branch_attn[l], BR_TN),
        w_out=w_out[l].astype(BF16),
    )


def _layer(x, p):
    batch, seq, _ = x.shape
    x2d = x.reshape(batch * seq, D_MODEL)
    x1, h = _ffn(x2d, *p["ffn1"], final=False)
    u, dt = _inproj(h, p["w_u"], p["w_dt"], p["dt_bias"], p["q_gain"], p["k_gain"])
    xc = _conv(u, p["conv_w"], p["conv_b"], seq)
    cum, wdec, ecum, rows = _ssd_prep(dt, p["a_log"])
    ys = _ssd(xc, u, cum, wdec, ecum, rows, p["d_exp"], p["e01"], batch, seq)
    ya = _attn(u, p["bias"], p["sink"], seq)
    merged = _branch(ys, ya, u, p["out_norm"], p["w_ssd"], p["w_att"])
    x2 = _outproj(merged, p["w_out"], x1)
    (y,) = _ffn(x2, *p["ffn2"], final=True)
    return y.reshape(batch, seq, D_MODEL)


def kernel(x_prompt, x_sample, ffn1_norm, ffn1_w_gate, ffn1_w_up, ffn1_w_down, mix_norm, w_in, conv_w, conv_b, ssd_A_log, ssd_dt_bias, ssd_D, ssd_out_norm, q_norm, k_norm, attn_sink, rel_bias, w_branch_ssd, w_branch_attn, w_out, ffn2_norm, ffn2_w_gate, ffn2_w_up, ffn2_w_down, final_norm):
    p = _prepare(ffn1_norm, ffn1_w_gate, ffn1_w_up, ffn1_w_down, mix_norm, w_in, conv_w, conv_b,
                 ssd_A_log, ssd_dt_bias, ssd_D, ssd_out_norm, q_norm, k_norm, attn_sink, rel_bias,
                 w_branch_ssd, w_branch_attn, w_out, ffn2_norm, ffn2_w_gate, ffn2_w_up, ffn2_w_down,
                 final_norm)
    return (_layer(x_prompt, p), _layer(x_sample, p))
```

```python
import functools
import math

import jax
import jax.numpy as jnp
import numpy as np
from jax import lax
from jax.experimental import pallas as pl
from jax.experimental.pallas import tpu as pltpu

D_MODEL = 2048
D_INNER = 2 * D_MODEL
SSD_HEADDIM = 64
SSD_HEADS = D_INNER // SSD_HEADDIM
SSD_GROUPS = 8
SSD_HPG = SSD_HEADS // SSD_GROUPS
D_STATE = 128
CONV_K = 5
CHUNK = 128
CONV_DIM = D_INNER + 2 * SSD_GROUPS * D_STATE
ATT_HEAD_DIM = 128
ATT_Q_HEADS = D_MODEL // ATT_HEAD_DIM
ATT_KV_HEADS = 4
ATT_REP = ATT_Q_HEADS // ATT_KV_HEADS
WINDOW = 128
ATT_BLOCK = 128
N_BUCKETS = 32
MAX_DIST = 128
D_FF = ((8 * D_MODEL // 3 + 255) // 256) * 256
EPS = 1e-6
GROUP_X = SSD_HPG * SSD_HEADDIM
KV_DIM = ATT_KV_HEADS * ATT_HEAD_DIM

U_Z = 0
U_XBC = U_Z + D_INNER
U_Q = U_XBC + CONV_DIM
U_GA = U_Q + D_MODEL
U_GB = U_GA + D_MODEL
U_K = U_GB + D_MODEL
U_V = U_K + KV_DIM
U_TOTAL = U_V + KV_DIM

V7X_VMEM_LIMIT_BYTES = 56 * 1024 * 1024
NEG_BIG = -1e30

BF16 = jnp.bfloat16
F32 = jnp.float32


def _params(semantics):
    return pltpu.CompilerParams(dimension_semantics=semantics,
                                vmem_limit_bytes=V7X_VMEM_LIMIT_BYTES)


def _rms(x, gain):
    return x * lax.rsqrt(jnp.mean(x * x, axis=-1, keepdims=True) + EPS) * gain


def _silu(x):
    return x * jax.nn.sigmoid(x)


def _dot(a, b):
    return jnp.dot(a, b, preferred_element_type=F32)


FFN_TM = 512
FFN_TF = 512


def _ffn_kernel(x_ref, g_ref, wg_ref, wu_ref, wd_ref, pg_ref, *refs, final):
    if final:
        y_ref, h_scr, acc_scr = refs
    else:
        x1_ref, h_ref, h_scr, acc_scr = refs
    j = pl.program_id(1)

    @pl.when(j == 0)
    def _():
        h_scr[...] = _rms(x_ref[...], g_ref[...]).astype(BF16)
        acc_scr[...] = jnp.zeros_like(acc_scr)

    h = h_scr[...]
    t = _silu(_dot(h, wg_ref[...])) * _dot(h, wu_ref[...])
    acc_scr[...] += _dot(t.astype(BF16), wd_ref[...])

    @pl.when(j == pl.num_programs(1) - 1)
    def _():
        x1 = x_ref[...] + 0.5 * acc_scr[...]
        normed = _rms(x1, pg_ref[...])
        if final:
            y_ref[...] = normed
        else:
            x1_ref[...] = x1
            h_ref[...] = normed.astype(BF16)


def _ffn(x, gain, wg, wu, wd, post_gain, final):
    n = x.shape[0]
    row = lambda i, j: (i, 0)
    fixed = lambda i, j: (0, 0)
    out_shape = [jax.ShapeDtypeStruct((n, D_MODEL), F32)]
    out_specs = [pl.BlockSpec((FFN_TM, D_MODEL), row)]
    if not final:
        out_shape.append(jax.ShapeDtypeStruct((n, D_MODEL), BF16))
        out_specs.append(pl.BlockSpec((FFN_TM, D_MODEL), row))
    return pl.pallas_call(
        functools.partial(_ffn_kernel, final=final),
        grid=(n // FFN_TM, D_FF // FFN_TF),
        in_specs=[
            pl.BlockSpec((FFN_TM, D_MODEL), row),
            pl.BlockSpec((1, D_MODEL), fixed),
            pl.BlockSpec((D_MODEL, FFN_TF), lambda i, j: (0, j)),
            pl.BlockSpec((D_MODEL, FFN_TF), lambda i, j: (0, j)),
            pl.BlockSpec((FFN_TF, D_MODEL), lambda i, j: (j, 0)),
            pl.BlockSpec((1, D_MODEL), fixed),
        ],
        out_specs=out_specs,
        out_shape=out_shape,
        scratch_shapes=[pltpu.VMEM((FFN_TM, D_MODEL), BF16), pltpu.VMEM((FFN_TM, D_MODEL), F32)],
        compiler_params=_params(("parallel", "arbitrary")),
        name="ffn_final" if final else "ffn",
    )(x, gain, wg, wu, wd, post_gain)


INP_TM = 2048
INP_TN = 1024


def _inproj_kernel(h_ref, w_ref, wdt_ref, bdt_ref, qg_ref, kg_ref, u_ref, dt_ref):
    h = h_ref[...]
    acc = _dot(h, w_ref[...])
    j = pl.program_id(1)
    hd = ATT_HEAD_DIM
    heads_per_tile = INP_TN // hd
    is_q = (j >= U_Q // INP_TN) & (j < U_GA // INP_TN)
    is_kv = j == U_K // INP_TN

    def store_heads(gain, n_normed):
        for hh in range(heads_per_tile):
            blk = acc[:, hh * hd:(hh + 1) * hd]
            if hh < n_normed:
                blk = _rms(blk, gain)
            u_ref[:, hh * hd:(hh + 1) * hd] = blk.astype(BF16)

    @pl.when(is_q)
    def _():
        store_heads(qg_ref[...] * (hd ** -0.5 * LOG2E), heads_per_tile)

    @pl.when(is_kv)
    def _():
        store_heads(kg_ref[...], ATT_KV_HEADS)

    @pl.when(jnp.logical_not(is_q | is_kv))
    def _():
        u_ref[...] = acc.astype(BF16)

    @pl.when(j == 0)
    def _():
        raw = _dot(h, wdt_ref[...]) + bdt_ref[...]
        dt_ref[...] = jnp.maximum(raw, 0.0) + jnp.log1p(jnp.exp(-jnp.abs(raw)))


def _inproj(h, w_u, w_dt, dt_bias, q_gain, k_gain):
    assert U_Q % INP_TN == 0 and U_GA % INP_TN == 0 and U_K % INP_TN == 0 and KV_DIM * 2 == INP_TN
    n = h.shape[0]
    ndt = 2 * SSD_HEADS
    return pl.pallas_call(
        _inproj_kernel,
        grid=(n // INP_TM, U_TOTAL // INP_TN),
        in_specs=[
            pl.BlockSpec((INP_TM, D_MODEL), lambda i, j: (i, 0)),
            pl.BlockSpec((D_MODEL, INP_TN), lambda i, j: (0, j)),
            pl.BlockSpec((D_MODEL, ndt), lambda i, j: (0, 0)),
            pl.BlockSpec((1, ndt), lambda i, j: (0, 0)),
            pl.BlockSpec((1, ATT_HEAD_DIM), lambda i, j: (0, 0)),
            pl.BlockSpec((1, ATT_HEAD_DIM), lambda i, j: (0, 0)),
        ],
        out_specs=[
            pl.BlockSpec((INP_TM, INP_TN), lambda i, j: (i, j)),
            pl.BlockSpec((INP_TM, ndt), lambda i, j: (i, 0)),
        ],
        out_shape=[jax.ShapeDtypeStruct((n, U_TOTAL), BF16), jax.ShapeDtypeStruct((n, ndt), F32)],
        compiler_params=_params(("parallel", "arbitrary")),
        name="in_proj",
    )(h, w_u, w_dt, dt_bias, q_gain, k_gain)


CONV_TS = 512
CONV_TC = 2048
CONV_HALO = 16
CONV_TB = 32
CONV_WIN = CONV_TB + 2 * CONV_HALO
CONV_SIDE_TAPS = tuple(k for k in range(CONV_K) if k != CONV_K // 2)


def _conv_kernel(xm_ref, xp_ref, xn_ref, w_ref, b_ref, o_ref, xs_scr, pw_scr, *, tiles_per_seq):
    i = pl.program_id(0)
    first = (i % tiles_per_seq) == 0
    last = ((i + 1) % tiles_per_seq) == 0
    H = CONV_HALO
    zero_halo = jnp.zeros((H, CONV_TC), BF16)
    xs_scr[0:H, :] = jnp.where(first, zero_halo, xp_ref[...])
    xs_scr[H:H + CONV_TS, :] = xm_ref[...]
    xs_scr[H + CONV_TS:, :] = jnp.where(last, zero_halo, xn_ref[...])
    xs = xs_scr[...]
    for n, k in enumerate(CONV_SIDE_TAPS):
        pw_scr[n] = xs * w_ref[k:k + 1, :].astype(BF16)
    r = lax.broadcasted_iota(jnp.int32, (CONV_TB, len(CONV_SIDE_TAPS) * CONV_WIN), 0)
    c = lax.broadcasted_iota(jnp.int32, (CONV_TB, len(CONV_SIDE_TAPS) * CONV_WIN), 1)
    hit = None
    for n, k in enumerate(CONV_SIDE_TAPS):
        sel = c == r + (n * CONV_WIN + H + k - CONV_K // 2)
        hit = sel if hit is None else hit | sel
    shift = jnp.where(hit, 1.0, 0.0).astype(BF16)
    w_mid = w_ref[CONV_K // 2:CONV_K // 2 + 1, :]
    bias = b_ref[...]
    for blk in range(CONV_TS // CONV_TB):
        t0 = blk * CONV_TB
        taps = jnp.concatenate([pw_scr[n, t0:t0 + CONV_WIN, :] for n in range(len(CONV_SIDE_TAPS))], axis=0)
        mid = xs_scr[H + t0:H + t0 + CONV_TB, :].astype(F32) * w_mid + bias
        o_ref[t0:t0 + CONV_TB, :] = _silu(_dot(shift, taps) + mid).astype(BF16)


def _conv(u, conv_w, conv_b, seq):
    n = u.shape[0]
    tiles_per_seq = seq // CONV_TS
    halo_per_tile = CONV_TS // CONV_HALO
    n_halo = n // CONV_HALO
    col0 = U_XBC // CONV_TC
    return pl.pallas_call(
        functools.partial(_conv_kernel, tiles_per_seq=tiles_per_seq),
        grid=(n // CONV_TS, CONV_DIM // CONV_TC),
        in_specs=[
            pl.BlockSpec((CONV_TS, CONV_TC), lambda i, j: (i, col0 + j)),
            pl.BlockSpec((CONV_HALO, CONV_TC),
                         lambda i, j: (jnp.maximum(i * halo_per_tile - 1, 0), col0 + j)),
            pl.BlockSpec((CONV_HALO, CONV_TC),
                         lambda i, j: (jnp.minimum((i + 1) * halo_per_tile, n_halo - 1), col0 + j)),
            pl.BlockSpec((CONV_K, CONV_TC), lambda i, j: (0, j)),
            pl.BlockSpec((1, CONV_TC), lambda i, j: (0, j)),
        ],
        out_specs=pl.BlockSpec((CONV_TS, CONV_TC), lambda i, j: (i, j)),
        out_shape=jax.ShapeDtypeStruct((n, CONV_DIM), BF16),
        scratch_shapes=[pltpu.VMEM((CONV_TS + 2 * CONV_HALO, CONV_TC), BF16),
                        pltpu.VMEM((len(CONV_SIDE_TAPS), CONV_TS + 2 * CONV_HALO, CONV_TC), BF16)],
        compiler_params=_params(("parallel", "parallel")),
        name="conv",
    )(u, u, u, conv_w, conv_b)


def _split3(a):
    hi = a.astype(BF16)
    r1 = a - hi.astype(F32)
    mid = r1.astype(BF16)
    lo = (r1 - mid.astype(F32)).astype(BF16)
    return hi, mid, lo


def _exact_lhs01(m01, a):
    hi, mid, lo = _split3(a)
    return _dot(jnp.concatenate([m01, m01, m01], axis=1), jnp.concatenate([hi, mid, lo], axis=0))


def _split2(a):
    hi = a.astype(BF16)
    mid = (a - hi.astype(F32)).astype(BF16)
    return jnp.concatenate([hi, mid], axis=1)


PREP_CHUNKS = 8
ROWS_PER_GROUP = 4 * SSD_HPG
LOG2E = 1.4426950408889634


def _ssd_prep_kernel(dt_ref, a_ref, cum_ref, wdec_ref, ecum_ref, rows_ref):
    L = CHUNK
    H = SSD_HPG
    row = lax.broadcasted_iota(jnp.int32, (L, L), 0)
    col = lax.broadcasted_iota(jnp.int32, (L, L), 1)
    lower = (row >= col).astype(BF16)
    upper = (row <= col).astype(BF16)
    is_fwd = col < SSD_HEADS
    a = -jnp.exp(a_ref[...])
    for t in range(PREP_CHUNKS):
        sl = slice(t * L, (t + 1) * L)
        dt = dt_ref[sl, :]
        dA = dt * a
        pre = _exact_lhs01(lower, dA)
        suf = _exact_lhs01(upper, dA)
        cum = jnp.where(is_fwd, pre, suf)
        tot = jnp.where(is_fwd[:1], pre[L - 1:L, :], suf[0:1, :])
        cum_ref[sl, :] = cum * LOG2E
        wdec_ref[sl, :] = _split2(jnp.exp(tot - cum) * dt)
        ecum_ref[sl, :] = _split2(jnp.exp(cum))
        dt_t = dt.T
        src_t = (cum.T - jnp.log(dt_t)) * LOG2E
        lds_t = jnp.log2(dt_t[:SSD_HEADS] + dt_t[SSD_HEADS:])
        etot = jnp.broadcast_to(jnp.exp(tot), (H, L))
        for g in range(SSD_GROUPS):
            f = slice(g * H, (g + 1) * H)
            b = slice(SSD_HEADS + g * H, SSD_HEADS + (g + 1) * H)
            rows_ref[t, g] = jnp.concatenate([src_t[f], src_t[b], lds_t[f], etot], axis=0)


def _ssd_prep(dt, a_log):
    n = dt.shape[0]
    rows_blk = PREP_CHUNKS * CHUNK
    nh = 2 * SSD_HEADS
    return pl.pallas_call(
        _ssd_prep_kernel,
        grid=(n // rows_blk,),
        in_specs=[pl.BlockSpec((rows_blk, nh), lambda i: (i, 0)), pl.BlockSpec((1, nh), lambda i: (0, 0))],
        out_specs=[pl.BlockSpec((rows_blk, nh), lambda i: (i, 0)),
                   pl.BlockSpec((rows_blk, 2 * nh), lambda i: (i, 0)),
                   pl.BlockSpec((rows_blk, 2 * nh), lambda i: (i, 0)),
                   pl.BlockSpec((PREP_CHUNKS, SSD_GROUPS, ROWS_PER_GROUP, CHUNK), lambda i: (i, 0, 0, 0))],
        out_shape=[jax.ShapeDtypeStruct((n, nh), F32),
                   jax.ShapeDtypeStruct((n, 2 * nh), BF16),
                   jax.ShapeDtypeStruct((n, 2 * nh), BF16),
                   jax.ShapeDtypeStruct((n // CHUNK, SSD_GROUPS, ROWS_PER_GROUP, CHUNK), F32)],
        compiler_params=_params(("parallel",)),
        name="ssd_prep",
    )(dt, a_log)


SSD_T = 8


def _ssd_kernel(x_ref, b_ref, c_ref, z_ref, cum_ref, wdec_ref, ecum_ref, rows_ref, d_ref, e_ref,
                y_ref, gst_scr, h_scr):
    g = pl.program_id(1)
    phase = pl.program_id(2)
    step = pl.program_id(3)
    ns = pl.num_programs(3)
    L = CHUNK
    GX = GROUP_X
    H = SSD_HPG

    @pl.when(step == 0)
    def _():
        h_scr[...] = jnp.zeros_like(h_scr)

    def state_update(t, direction, xf, bm):
        sl = slice(t * L, (t + 1) * L)
        e_dir = e_ref[0, :, direction * GX:(direction + 1) * GX]
        w = _dot(wdec_ref[sl, :], e_dir)
        dec = _dot(_split2(rows_ref[t, 0, 3 * H:4 * H, :]), e_dir)[0:1, :]
        xdec = (w * xf).astype(BF16)
        b_t = bm.astype(F32).T.astype(BF16)
        h_scr[...] = dec * h_scr[...] + _dot(b_t, xdec)

    @pl.when(phase == 0)
    def _():
        for t in reversed(range(SSD_T)):
            sl = slice(t * L, (t + 1) * L)
            gst_scr[(ns - 1 - step) * SSD_T + t] = h_scr[...].astype(BF16)
            state_update(t, 1, x_ref[sl, :].astype(F32), b_ref[sl, :])

    @pl.when(phase == 1)
    def _():
        row = lax.broadcasted_iota(jnp.int32, (L, L), 0)
        col = lax.broadcasted_iota(jnp.int32, (L, L), 1)
        lower = row >= col
        diag = row == col
        low_lane = col < SSD_HEADDIM
        shift = (2 * SSD_HEADS - H * g) % (2 * SSD_HEADS)
        for t in range(SSD_T):
            sl = slice(t * L, (t + 1) * L)
            cm = c_ref[sl, :]
            bm = b_ref[sl, :]
            xb = x_ref[sl, :]
            xf = xb.astype(F32)
            cb = lax.dot_general(cm, bm, (((1,), (1,)), ((), ())), preferred_element_type=F32)
            cum_r = pltpu.roll(cum_ref[sl, :], shift, axis=1)
            pairs = []
            for k in range(H // 2):
                ms = []
                for half in range(2):
                    hh = 2 * k + half
                    fwd = cum_r[:, hh:hh + 1] - rows_ref[t, 0, hh:hh + 1, :]
                    bwd = cum_r[:, SSD_HEADS + hh:SSD_HEADS + hh + 1] - rows_ref[t, 0, H + hh:H + hh + 1, :]
                    arg = jnp.where(diag, rows_ref[t, 0, 2 * H + hh:2 * H + hh + 1, :],
                                    jnp.where(lower, fwd, bwd))
                    ms.append((jnp.exp2(arg) * cb).astype(BF16))
                xp = xb[:, k * L:(k + 1) * L]
                zero = jnp.zeros_like(xp)
                rhs = jnp.concatenate([jnp.where(low_lane, xp, zero), jnp.where(low_lane, zero, xp)], axis=0)
                pairs.append(_dot(jnp.concatenate(ms, axis=1), rhs))
            y = jnp.concatenate(pairs, axis=1)
            c = step * SSD_T + t
            states = jnp.concatenate([h_scr[...].astype(BF16), gst_scr[c]], axis=1)
            y_off = _dot(ecum_ref[sl, :], e_ref[0]) * _dot(cm, states)
            y = y + y_off[:, :GX] + y_off[:, GX:] + d_ref[...] * xf
            y_ref[sl, :] = (y * _silu(z_ref[sl, :].astype(F32))).astype(BF16)
            state_update(t, 0, xf, bm)


def _ssd(xc, u, cum, wdec, ecum, rows, d_exp, e01, batch, seq):
    n = xc.shape[0]
    nc = seq // CHUNK
    ns = nc // SSD_T
    tl = SSD_T * CHUNK
    nh = 2 * SSD_HEADS
    gx_blocks = D_INNER // D_STATE

    def fb_row(b, p, s):
        return b * ns + jnp.where(p == 0, ns - 1 - s, s)

    def f_row(b, p, s):
        return b * ns + jnp.where(p == 0, 0, s)

    return pl.pallas_call(
        _ssd_kernel,
        grid=(batch, SSD_GROUPS, 2, ns),
        in_specs=[
            pl.BlockSpec((tl, GROUP_X), lambda b, g, p, s: (fb_row(b, p, s), g)),
            pl.BlockSpec((tl, D_STATE), lambda b, g, p, s: (fb_row(b, p, s), gx_blocks + g)),
            pl.BlockSpec((tl, D_STATE), lambda b, g, p, s: (f_row(b, p, s), gx_blocks + SSD_GROUPS + g)),
            pl.BlockSpec((tl, GROUP_X), lambda b, g, p, s: (f_row(b, p, s), U_Z // GROUP_X + g)),
            pl.BlockSpec((tl, nh), lambda b, g, p, s: (f_row(b, p, s), 0)),
            pl.BlockSpec((tl, 2 * nh), lambda b, g, p, s: (fb_row(b, p, s), 0)),
            pl.BlockSpec((tl, 2 * nh), lambda b, g, p, s: (f_row(b, p, s), 0)),
            pl.BlockSpec((SSD_T, 1, ROWS_PER_GROUP, CHUNK), lambda b, g, p, s: (fb_row(b, p, s), g, 0, 0)),
            pl.BlockSpec((1, GROUP_X), lambda b, g, p, s: (0, g)),
            pl.BlockSpec((1, 2 * nh, 2 * GROUP_X), lambda b, g, p, s: (g, 0, 0)),
        ],
        out_specs=pl.BlockSpec((tl, GROUP_X), lambda b, g, p, s: (f_row(b, p, s), g)),
        out_shape=jax.ShapeDtypeStruct((n, D_INNER), BF16),
        scratch_shapes=[
            pltpu.VMEM((nc, D_STATE, GROUP_X), BF16),
            pltpu.VMEM((D_STATE, GROUP_X), F32),
        ],
        compiler_params=_params(("arbitrary", "arbitrary", "arbitrary", "arbitrary")),
        name="ssd",
    )(xc, xc, xc, u, cum, wdec, ecum, rows, d_exp, e01)


def _ssd_expand_matrix():
    e = np.zeros((SSD_GROUPS, 2 * SSD_HEADS, 2 * GROUP_X), np.float32)
    for g in range(SSD_GROUPS):
        for d in range(2):
            for r in range(SSD_HPG):
                lo = d * GROUP_X + r * SSD_HEADDIM
                e[g, d * SSD_HEADS + g * SSD_HPG + r, lo:lo + SSD_HEADDIM] = 1.0
    return np.concatenate([e, e], axis=1)


def _t5_bucket(rel):
    nb = N_BUCKETS // 2
    ret = (rel > 0).astype(np.int32) * nb
    n = np.abs(rel)
    max_exact = nb // 2
    large = max_exact + (np.log(np.maximum(n, 1) / max_exact) / math.log(MAX_DIST / max_exact)
                         * (nb - max_exact)).astype(np.int32)
    large = np.minimum(large, nb - 1)
    return (ret + np.where(n < max_exact, n, large)).astype(np.int32)


def _bucket_table():
    i = np.arange(ATT_BLOCK)[:, None]
    j = np.arange(3 * ATT_BLOCK)[None, :]
    rel = j - ATT_BLOCK - i
    return np.where(np.abs(rel) <= WINDOW, _t5_bucket(rel), -1).astype(np.int32)


N_EDGE_VARIANTS = 4


def _bias_kernel(rel_ref, bucket_ref, o_ref):
    variant = pl.program_id(0)
    h = pl.program_id(1)
    bucket = bucket_ref[...]
    acc = jnp.full(bucket.shape, NEG_BIG, F32)
    for b in range(N_BUCKETS):
        acc = jnp.where(bucket == b, rel_ref[b, h] * LOG2E, acc)
    kcol = lax.broadcasted_iota(jnp.int32, bucket.shape, 1)
    dead = (((variant & 1) == 1) & (kcol < ATT_BLOCK)) | (((variant & 2) == 2) & (kcol >= 2 * ATT_BLOCK))
    o_ref[0, 0] = jnp.where(dead, NEG_BIG, acc)


def _bias_table(rel_bias):
    bucket = jnp.asarray(_bucket_table())
    return pl.pallas_call(
        _bias_kernel,
        grid=(N_EDGE_VARIANTS, ATT_Q_HEADS),
        in_specs=[
            pl.BlockSpec(memory_space=pltpu.SMEM),
            pl.BlockSpec((ATT_BLOCK, 3 * ATT_BLOCK), lambda v, h: (0, 0)),
        ],
        out_specs=pl.BlockSpec((1, 1, ATT_BLOCK, 3 * ATT_BLOCK), lambda v, h: (v, h, 0, 0)),
        out_shape=jax.ShapeDtypeStruct((N_EDGE_VARIANTS, ATT_Q_HEADS, ATT_BLOCK, 3 * ATT_BLOCK), F32),
        compiler_params=_params(("arbitrary", "arbitrary")),
        name="rel_bias_table",
    )(rel_bias, bucket)


def _attn_block(q_ref, kp_ref, kc_ref, kn_ref, vp_ref, vc_ref, vn_ref, bias_ref, sink_ref, store):
    T = ATT_BLOCK
    hd = ATT_HEAD_DIM
    ones = jnp.ones((3 * T, hd), BF16)
    for g in range(ATT_KV_HEADS):
        ks = slice(g * hd, (g + 1) * hd)
        kwin = jnp.concatenate([kp_ref[:, ks], kc_ref[:, ks], kn_ref[:, ks]], axis=0)
        vwin = jnp.concatenate([vp_ref[:, ks], vc_ref[:, ks], vn_ref[:, ks]], axis=0)
        qs = jnp.concatenate(
            [q_ref[:, (g * ATT_REP + r) * hd:(g * ATT_REP + r + 1) * hd] for r in range(ATT_REP)],
            axis=0)
        s = lax.dot_general(qs, kwin, (((1,), (1,)), ((), ())), preferred_element_type=F32)
        s = s + jnp.concatenate([bias_ref[0, g * ATT_REP + r] for r in range(ATT_REP)], axis=0)
        sink = jnp.concatenate(
            [jnp.full((T, 1), sink_ref[g * ATT_REP + r] * LOG2E, F32) for r in range(ATT_REP)], axis=0)
        m = jnp.maximum(jnp.max(s, axis=-1, keepdims=True), sink)
        p = jnp.exp2(s - m).astype(BF16)
        ov = _dot(p, jnp.concatenate([vwin, ones], axis=1))
        o = ov[:, :hd] / (ov[:, hd:] + jnp.exp2(sink - m))
        for r in range(ATT_REP):
            store(g * ATT_REP + r, o[r * T:(r + 1) * T, :].astype(BF16))


BR_TM = 512
BR_TN = 512
BR_STEPS = D_MODEL // BR_TN
assert BR_TM == BR_STEPS * ATT_BLOCK


def _attn_branch_kernel(q_ref, kp_ref, kc_ref, kn_ref, vp_ref, vc_ref, vn_ref, bias_ref, sink_ref,
                        ys_ref, ga_ref, gb_ref, ng_ref, ws_ref, wa_ref, o_ref, ya_even, ya_odd):
    i = pl.program_id(0)
    j = pl.program_id(1)
    hd = ATT_HEAD_DIM

    @pl.when((i == 0) & (j == 0))
    def _():
        ya_odd[...] = jnp.zeros_like(ya_odd)

    def step(fill_scr, use_scr):
        row0 = pl.multiple_of(j * ATT_BLOCK, ATT_BLOCK)

        def store(head, val):
            fill_scr[pl.ds(row0, ATT_BLOCK), head * hd:(head + 1) * hd] = val

        _attn_block(q_ref, kp_ref, kc_ref, kn_ref, vp_ref, vc_ref, vn_ref, bias_ref, sink_ref, store)
        yg = ys_ref[...]
        yf = yg.astype(F32)
        scale = lax.rsqrt(jnp.mean(yf * yf, axis=-1, keepdims=True) + EPS)
        a = jax.nn.sigmoid(ga_ref[...].astype(F32)) * (scale * _dot(yg * ng_ref[...].astype(BF16), ws_ref[...]))
        b = jax.nn.sigmoid(gb_ref[...].astype(F32)) * _dot(use_scr[...], wa_ref[...])
        o_ref[...] = (a + b).astype(BF16)

    @pl.when(i % 2 == 0)
    def _():
        step(ya_even, ya_odd)

    @pl.when(i % 2 == 1)
    def _():
        step(ya_odd, ya_even)


def _attn_branch(ys, u, bias, sink, norm_gain, w_ssd, w_att, seq):
    n = ys.shape[0]
    nblk = n // ATT_BLOCK
    ntile = n // BR_TM
    blocks_per_seq = seq // ATT_BLOCK
    kcol = U_K // KV_DIM
    vcol = U_V // KV_DIM
    blk = lambda i, j: jnp.minimum(i * BR_STEPS + j, nblk - 1)
    prev = lambda i, j: jnp.maximum(blk(i, j) - 1, 0)
    nxt = lambda i, j: jnp.minimum(blk(i, j) + 1, nblk - 1)
    tile = lambda i: jnp.maximum(i - 1, 0)
    kv_spec = lambda rowf, c: pl.BlockSpec((ATT_BLOCK, KV_DIM), lambda i, j: (rowf(i, j), c))

    def edge_variant(i, j):
        b = blk(i, j)
        first = (b % blocks_per_seq == 0).astype(jnp.int32)
        last = ((b + 1) % blocks_per_seq == 0).astype(jnp.int32)
        return first + 2 * last

    return pl.pallas_call(
        _attn_branch_kernel,
        grid=(ntile + 1, BR_STEPS),
        in_specs=[
            pl.BlockSpec((ATT_BLOCK, D_MODEL), lambda i, j: (blk(i, j), U_Q // D_MODEL)),
            kv_spec(prev, kcol), kv_spec(blk, kcol), kv_spec(nxt, kcol),
            kv_spec(prev, vcol), kv_spec(blk, vcol), kv_spec(nxt, vcol),
            pl.BlockSpec((1, ATT_Q_HEADS, ATT_BLOCK, 3 * ATT_BLOCK), lambda i, j: (edge_variant(i, j), 0, 0, 0)),
            pl.BlockSpec(memory_space=pltpu.SMEM),
            pl.BlockSpec((BR_TM, D_INNER), lambda i, j: (tile(i), 0)),
            pl.BlockSpec((BR_TM, BR_TN), lambda i, j: (tile(i), U_GA // BR_TN + j)),
            pl.BlockSpec((BR_TM, BR_TN), lambda i, j: (tile(i), U_GB // BR_TN + j)),
            pl.BlockSpec((1, D_INNER), lambda i, j: (0, 0)),
            pl.BlockSpec((D_INNER, BR_TN), lambda i, j: (0, j)),
            pl.BlockSpec((D_MODEL, BR_TN), lambda i, j: (0, j)),
        ],
        out_specs=pl.BlockSpec((BR_TM, BR_TN), lambda i, j: (i, j)),
        out_shape=jax.ShapeDtypeStruct((n + BR_TM, D_MODEL), BF16),
        scratch_shapes=[pltpu.VMEM((BR_TM, D_MODEL), BF16), pltpu.VMEM((BR_TM, D_MODEL), BF16)],
        compiler_params=_params(("arbitrary", "arbitrary")),
        name="attn_branch",
    )(u, u, u, u, u, u, u, bias, sink, ys, u, u, norm_gain, w_ssd, w_att)


OUT_TM = 512


def _outproj_kernel(m_ref, w_ref, x_ref, o_ref):
    o_ref[...] = x_ref[...] + _dot(m_ref[...], w_ref[...])


def _outproj(merged, w_out, x1):
    assert OUT_TM == BR_TM
    n = x1.shape[0]
    return pl.pallas_call(
        _outproj_kernel,
        grid=(n // OUT_TM,),
        in_specs=[
            pl.BlockSpec((OUT_TM, D_MODEL), lambda i: (i + 1, 0)),
            pl.BlockSpec((D_MODEL, D_MODEL), lambda i: (0, 0)),
            pl.BlockSpec((OUT_TM, D_MODEL), lambda i: (i, 0)),
        ],
        out_specs=pl.BlockSpec((OUT_TM, D_MODEL), lambda i: (i, 0)),
        out_shape=jax.ShapeDtypeStruct((n, D_MODEL), F32),
        compiler_params=_params(("parallel",)),
        name="out_proj",
    )(merged, w_out, x1)


def _prepare(ffn1_norm, ffn1_w_gate, ffn1_w_up, ffn1_w_down, mix_norm, w_in, conv_w, conv_b,
             ssd_A_log, ssd_dt_bias, ssd_D, ssd_out_norm, q_norm, k_norm, attn_sink, rel_bias,
             w_branch_ssd, w_branch_attn, w_out, ffn2_norm, ffn2_w_gate, ffn2_w_up, ffn2_w_down,
             final_norm):
    l = 0
    w = w_in[l]
    o_z = 0
    o_xbc = o_z + D_INNER
    o_dt = o_xbc + CONV_DIM
    o_q = o_dt + 2 * SSD_HEADS
    o_k = o_q + D_MODEL
    o_v = o_k + KV_DIM
    o_ga = o_v + KV_DIM
    o_gb = o_ga + D_MODEL
    w_u = jnp.concatenate(
        [w[:, o_z:o_dt], w[:, o_q:o_k], w[:, o_ga:o_gb + D_MODEL], w[:, o_k:o_ga]], axis=1).astype(BF16)
    row = lambda v: v.reshape(1, -1).astype(F32)
    return dict(
        ffn1=(row(ffn1_norm[l]), ffn1_w_gate[l].astype(BF16), ffn1_w_up[l].astype(BF16),
              ffn1_w_down[l].astype(BF16), row(mix_norm[l])),
        ffn2=(row(ffn2_norm[l]), ffn2_w_gate[l].astype(BF16), ffn2_w_up[l].astype(BF16),
              ffn2_w_down[l].astype(BF16), row(final_norm[l])),
        w_u=w_u, w_dt=w[:, o_dt:o_q].astype(BF16), dt_bias=row(ssd_dt_bias[l]),
        conv_w=conv_w[l].astype(F32), conv_b=row(conv_b[l]),
        a_log=row(ssd_A_log[l]), d_exp=row(jnp.repeat(ssd_D[l], SSD_HEADDIM)),
        e01=jnp.asarray(_ssd_expand_matrix(), BF16), out_norm=row(ssd_out_norm[l]),
        q_gain=row(q_norm[l]), k_gain=row(k_norm[l]), sink=attn_sink[l].astype(F32),
        bias=_bias_table(rel_bias.astype(F32)),
        w_ssd=w_branch_ssd[l].astype(BF16), w_att=w_branch_attn[l].astype(BF16),
        w_out=w_out[l].astype(BF16),
    )


def _layer(x, p):
    batch, seq, _ = x.shape
    x2d = x.reshape(batch * seq, D_MODEL)
    x1, h = _ffn(x2d, *p["ffn1"], final=False)
    u, dt = _inproj(h, p["w_u"], p["w_dt"], p["dt_bias"], p["q_gain"], p["k_gain"])
    xc = _conv(u, p["conv_w"], p["conv_b"], seq)
    cum, wdec, ecum, rows = _ssd_prep(dt, p["a_log"])
    ys = _ssd(xc, u, cum, wdec, ecum, rows, p["d_exp"], p["e01"], batch, seq)
    merged = _attn_branch(ys, u, p["bias"], p["sink"], p["out_norm"], p["w_ssd"], p["w_att"], seq)
    x2 = _outproj(merged, p["w_out"], x1)
    (y,) = _ffn(x2, *p["ffn2"], final=True)
    return y.reshape(batch, seq, D_MODEL)


def kernel(x_prompt, x_sample, ffn1_norm, ffn1_w_gate, ffn1_w_up, ffn1_w_down, mix_norm, w_in, conv_w, conv_b, ssd_A_log, ssd_dt_bias, ssd_D, ssd_out_norm, q_norm, k_norm, attn_sink, rel_bias, w_branch_ssd, w_branch_attn, w_out, ffn2_norm, ffn2_w_gate, ffn2_w_up, ffn2_w_down, final_norm):
    p = _prepare(ffn1_norm, ffn1_w_gate, ffn1_w_up, ffn1_w_down, mix_norm, w_in, conv_w, conv_b,
                 ssd_A_log, ssd_dt_bias, ssd_D, ssd_out_norm, q_norm, k_norm, attn_sink, rel_bias,
                 w_branch_ssd, w_branch_attn, w_out, ffn2_norm, ffn2_w_gate, ffn2_w_up, ffn2_w_down,
                 final_norm)
    return (_layer(x_prompt, p), _layer(x_sample, p))
```

```python
import functools
import math

import jax
import jax.numpy as jnp
import numpy as np
from jax import lax
from jax.experimental import pallas as pl
from jax.experimental.pallas import tpu as pltpu

D_MODEL = 2048
D_INNER = 2 * D_MODEL
SSD_HEADDIM = 64
SSD_HEADS = D_INNER // SSD_HEADDIM
SSD_GROUPS = 8
SSD_HPG = SSD_HEADS // SSD_GROUPS
D_STATE = 128
CONV_K = 5
CHUNK = 128
CONV_DIM = D_INNER + 2 * SSD_GROUPS * D_STATE
ATT_HEAD_DIM = 128
ATT_Q_HEADS = D_MODEL // ATT_HEAD_DIM
ATT_KV_HEADS = 4
ATT_REP = ATT_Q_HEADS // ATT_KV_HEADS
WINDOW = 128
ATT_BLOCK = 128
N_BUCKETS = 32
MAX_DIST = 128
D_FF = ((8 * D_MODEL // 3 + 255) // 256) * 256
EPS = 1e-6
GROUP_X = SSD_HPG * SSD_HEADDIM
KV_DIM = ATT_KV_HEADS * ATT_HEAD_DIM

U_Z = 0
U_XBC = U_Z + D_INNER
U_Q = U_XBC + CONV_DIM
U_GA = U_Q + D_MODEL
U_GB = U_GA + D_MODEL
U_K = U_GB + D_MODEL
U_V = U_K + KV_DIM
U_TOTAL = U_V + KV_DIM

V7X_VMEM_LIMIT_BYTES = 56 * 1024 * 1024
NEG_BIG = -1e30

BF16 = jnp.bfloat16
F32 = jnp.float32


def _params(semantics):
    return pltpu.CompilerParams(dimension_semantics=semantics,
                                vmem_limit_bytes=V7X_VMEM_LIMIT_BYTES)


def _rms(x, gain):
    return x * lax.rsqrt(jnp.mean(x * x, axis=-1, keepdims=True) + EPS) * gain


def _silu(x):
    return x * jax.nn.sigmoid(x)


def _dot(a, b):
    return jnp.dot(a, b, preferred_element_type=F32)


FFN_TM = 512
FFN_TF = 512


def _ffn_kernel(x_ref, g_ref, wg_ref, wu_ref, wd_ref, pg_ref, *refs, final):
    if final:
        y_ref, h_scr, acc_scr = refs
    else:
        x1_ref, h_ref, h_scr, acc_scr = refs
    j = pl.program_id(1)

    @pl.when(j == 0)
    def _():
        h_scr[...] = _rms(x_ref[...], g_ref[...]).astype(BF16)
        acc_scr[...] = jnp.zeros_like(acc_scr)

    h = h_scr[...]
    t = _silu(_dot(h, wg_ref[...])) * _dot(h, wu_ref[...])
    acc_scr[...] += _dot(t.astype(BF16), wd_ref[...])

    @pl.when(j == pl.num_programs(1) - 1)
    def _():
        x1 = x_ref[...] + 0.5 * acc_scr[...]
        normed = _rms(x1, pg_ref[...])
        if final:
            y_ref[...] = normed
        else:
            x1_ref[...] = x1
            h_ref[...] = normed.astype(BF16)


def _ffn(x, gain, wg, wu, wd, post_gain, final):
    n = x.shape[0]
    row = lambda i, j: (i, 0)
    fixed = lambda i, j: (0, 0)
    out_shape = [jax.ShapeDtypeStruct((n, D_MODEL), F32)]
    out_specs = [pl.BlockSpec((FFN_TM, D_MODEL), row)]
    if not final:
        out_shape.append(jax.ShapeDtypeStruct((n, D_MODEL), BF16))
        out_specs.append(pl.BlockSpec((FFN_TM, D_MODEL), row))
    return pl.pallas_call(
        functools.partial(_ffn_kernel, final=final),
        grid=(n // FFN_TM, D_FF // FFN_TF),
        in_specs=[
            pl.BlockSpec((FFN_TM, D_MODEL), row),
            pl.BlockSpec((1, D_MODEL), fixed),
            pl.BlockSpec((D_MODEL, FFN_TF), lambda i, j: (0, j)),
            pl.BlockSpec((D_MODEL, FFN_TF), lambda i, j: (0, j)),
            pl.BlockSpec((FFN_TF, D_MODEL), lambda i, j: (j, 0)),
            pl.BlockSpec((1, D_MODEL), fixed),
        ],
        out_specs=out_specs,
        out_shape=out_shape,
        scratch_shapes=[pltpu.VMEM((FFN_TM, D_MODEL), BF16), pltpu.VMEM((FFN_TM, D_MODEL), F32)],
        compiler_params=_params(("parallel", "arbitrary")),
        name="ffn_final" if final else "ffn",
    )(x, gain, wg, wu, wd, post_gain)


INP_TM = 2048
INP_TN = 1024


def _inproj_kernel(h_ref, w_ref, wdt_ref, bdt_ref, qg_ref, kg_ref, u_ref, dt_ref):
    h = h_ref[...]
    acc = _dot(h, w_ref[...])
    j = pl.program_id(1)
    hd = ATT_HEAD_DIM
    heads_per_tile = INP_TN // hd
    is_q = (j >= U_Q // INP_TN) & (j < U_GA // INP_TN)
    is_kv = j == U_K // INP_TN

    u_ref[...] = acc.astype(BF16)

    def norm_heads(gain, n_normed):
        for hh in range(n_normed):
            cols = slice(hh * hd, (hh + 1) * hd)
            u_ref[:, cols] = _rms(u_ref[:, cols].astype(F32), gain).astype(BF16)

    @pl.when(is_q)
    def _():
        norm_heads(qg_ref[...] * (hd ** -0.5 * LOG2E), heads_per_tile)

    @pl.when(is_kv)
    def _():
        norm_heads(kg_ref[...], ATT_KV_HEADS)

    @pl.when(j == 0)
    def _():
        raw = _dot(h, wdt_ref[...]) + bdt_ref[...]
        dt_ref[...] = jnp.maximum(raw, 0.0) + jnp.log1p(jnp.exp(-jnp.abs(raw)))


def _inproj(h, w_u, w_dt, dt_bias, q_gain, k_gain):
    assert U_Q % INP_TN == 0 and U_GA % INP_TN == 0 and U_K % INP_TN == 0 and KV_DIM * 2 == INP_TN
    n = h.shape[0]
    ndt = 2 * SSD_HEADS
    return pl.pallas_call(
        _inproj_kernel,
        grid=(n // INP_TM, U_TOTAL // INP_TN),
        in_specs=[
            pl.BlockSpec((INP_TM, D_MODEL), lambda i, j: (i, 0)),
            pl.BlockSpec((D_MODEL, INP_TN), lambda i, j: (0, j)),
            pl.BlockSpec((D_MODEL, ndt), lambda i, j: (0, 0)),
            pl.BlockSpec((1, ndt), lambda i, j: (0, 0)),
            pl.BlockSpec((1, ATT_HEAD_DIM), lambda i, j: (0, 0)),
            pl.BlockSpec((1, ATT_HEAD_DIM), lambda i, j: (0, 0)),
        ],
        out_specs=[
            pl.BlockSpec((INP_TM, INP_TN), lambda i, j: (i, j)),
            pl.BlockSpec((INP_TM, ndt), lambda i, j: (i, 0)),
        ],
        out_shape=[jax.ShapeDtypeStruct((n, U_TOTAL), BF16), jax.ShapeDtypeStruct((n, ndt), F32)],
        compiler_params=_params(("parallel", "arbitrary")),
        name="in_proj",
    )(h, w_u, w_dt, dt_bias, q_gain, k_gain)


CONV_TS = 512
CONV_TC = 2048
CONV_HALO = 16
CONV_TB = 32
CONV_WIN = CONV_TB + 2 * CONV_HALO
CONV_SIDE_TAPS = tuple(k for k in range(CONV_K) if k != CONV_K // 2)


def _conv_kernel(xm_ref, xp_ref, xn_ref, w_ref, b_ref, o_ref, xs_scr, pw_scr, *, tiles_per_seq):
    i = pl.program_id(0)
    first = (i % tiles_per_seq) == 0
    last = ((i + 1) % tiles_per_seq) == 0
    H = CONV_HALO
    zero_halo = jnp.zeros((H, CONV_TC), BF16)
    xs_scr[0:H, :] = jnp.where(first, zero_halo, xp_ref[...])
    xs_scr[H:H + CONV_TS, :] = xm_ref[...]
    xs_scr[H + CONV_TS:, :] = jnp.where(last, zero_halo, xn_ref[...])
    xs = xs_scr[...]
    for n, k in enumerate(CONV_SIDE_TAPS):
        pw_scr[n] = xs * (0.5 * w_ref[k:k + 1, :]).astype(BF16)
    r = lax.broadcasted_iota(jnp.int32, (CONV_TB, len(CONV_SIDE_TAPS) * CONV_WIN), 0)
    c = lax.broadcasted_iota(jnp.int32, (CONV_TB, len(CONV_SIDE_TAPS) * CONV_WIN), 1)
    hit = None
    for n, k in enumerate(CONV_SIDE_TAPS):
        sel = c == r + (n * CONV_WIN + H + k - CONV_K // 2)
        hit = sel if hit is None else hit | sel
    shift = jnp.where(hit, 1.0, 0.0).astype(BF16)
    w_mid = 0.5 * w_ref[CONV_K // 2:CONV_K // 2 + 1, :]
    bias = 0.5 * b_ref[...]
    for blk in range(CONV_TS // CONV_TB):
        t0 = blk * CONV_TB
        taps = jnp.concatenate([pw_scr[n, t0:t0 + CONV_WIN, :] for n in range(len(CONV_SIDE_TAPS))], axis=0)
        mid = xs_scr[H + t0:H + t0 + CONV_TB, :].astype(F32) * w_mid + bias
        half = _dot(shift, taps) + mid
        o_ref[t0:t0 + CONV_TB, :] = (half + half * jnp.tanh(half)).astype(BF16)


def _conv(u, conv_w, conv_b, seq):
    n = u.shape[0]
    tiles_per_seq = seq // CONV_TS
    halo_per_tile = CONV_TS // CONV_HALO
    n_halo = n // CONV_HALO
    col0 = U_XBC // CONV_TC
    return pl.pallas_call(
        functools.partial(_conv_kernel, tiles_per_seq=tiles_per_seq),
        grid=(n // CONV_TS, CONV_DIM // CONV_TC),
        in_specs=[
            pl.BlockSpec((CONV_TS, CONV_TC), lambda i, j: (i, col0 + j)),
            pl.BlockSpec((CONV_HALO, CONV_TC),
                         lambda i, j: (jnp.maximum(i * halo_per_tile - 1, 0), col0 + j)),
            pl.BlockSpec((CONV_HALO, CONV_TC),
                         lambda i, j: (jnp.minimum((i + 1) * halo_per_tile, n_halo - 1), col0 + j)),
            pl.BlockSpec((CONV_K, CONV_TC), lambda i, j: (0, j)),
            pl.BlockSpec((1, CONV_TC), lambda i, j: (0, j)),
        ],
        out_specs=pl.BlockSpec((CONV_TS, CONV_TC), lambda i, j: (i, j)),
        out_shape=jax.ShapeDtypeStruct((n, CONV_DIM), BF16),
        scratch_shapes=[pltpu.VMEM((CONV_TS + 2 * CONV_HALO, CONV_TC), BF16),
                        pltpu.VMEM((len(CONV_SIDE_TAPS), CONV_TS + 2 * CONV_HALO, CONV_TC), BF16)],
        compiler_params=_params(("parallel", "parallel")),
        name="conv",
    )(u, u, u, conv_w, conv_b)


def _split3(a):
    hi = a.astype(BF16)
    r1 = a - hi.astype(F32)
    mid = r1.astype(BF16)
    lo = (r1 - mid.astype(F32)).astype(BF16)
    return hi, mid, lo


def _exact_lhs01(m01, a):
    hi, mid, lo = _split3(a)
    return _dot(jnp.concatenate([m01, m01, m01], axis=1), jnp.concatenate([hi, mid, lo], axis=0))


def _split2(a):
    hi = a.astype(BF16)
    mid = (a - hi.astype(F32)).astype(BF16)
    return jnp.concatenate([hi, mid], axis=1)


PREP_CHUNKS = 8
ROWS_PER_GROUP = 4 * SSD_HPG
LOG2E = 1.4426950408889634


def _ssd_prep_kernel(dt_ref, a_ref, cum_ref, wdec_ref, ecum_ref, rows_ref):
    L = CHUNK
    H = SSD_HPG
    row = lax.broadcasted_iota(jnp.int32, (L, L), 0)
    col = lax.broadcasted_iota(jnp.int32, (L, L), 1)
    lower = (row >= col).astype(BF16)
    upper = (row <= col).astype(BF16)
    is_fwd = col < SSD_HEADS
    a = -jnp.exp(a_ref[...])
    for t in range(PREP_CHUNKS):
        sl = slice(t * L, (t + 1) * L)
        dt = dt_ref[sl, :]
        dA = dt * a
        pre = _exact_lhs01(lower, dA)
        suf = _exact_lhs01(upper, dA)
        cum = jnp.where(is_fwd, pre, suf)
        tot = jnp.where(is_fwd[:1], pre[L - 1:L, :], suf[0:1, :])
        cum_ref[sl, :] = cum * LOG2E
        wdec_ref[sl, :] = _split2(jnp.exp(tot - cum) * dt)
        ecum_ref[sl, :] = _split2(jnp.exp(cum))
        dt_t = dt.T
        src_t = (cum.T - jnp.log(dt_t)) * LOG2E
        lds_t = jnp.log2(dt_t[:SSD_HEADS] + dt_t[SSD_HEADS:])
        etot = jnp.broadcast_to(jnp.exp(tot), (H, L))
        for g in range(SSD_GROUPS):
            f = slice(g * H, (g + 1) * H)
            b = slice(SSD_HEADS + g * H, SSD_HEADS + (g + 1) * H)
            rows_ref[t, g] = jnp.concatenate([src_t[f], src_t[b], lds_t[f], etot], axis=0)


def _ssd_prep(dt, a_log):
    n = dt.shape[0]
    rows_blk = PREP_CHUNKS * CHUNK
    nh = 2 * SSD_HEADS
    return pl.pallas_call(
        _ssd_prep_kernel,
        grid=(n // rows_blk,),
        in_specs=[pl.BlockSpec((rows_blk, nh), lambda i: (i, 0)), pl.BlockSpec((1, nh), lambda i: (0, 0))],
        out_specs=[pl.BlockSpec((rows_blk, nh), lambda i: (i, 0)),
                   pl.BlockSpec((rows_blk, 2 * nh), lambda i: (i, 0)),
                   pl.BlockSpec((rows_blk, 2 * nh), lambda i: (i, 0)),
                   pl.BlockSpec((PREP_CHUNKS, SSD_GROUPS, ROWS_PER_GROUP, CHUNK), lambda i: (i, 0, 0, 0))],
        out_shape=[jax.ShapeDtypeStruct((n, nh), F32),
                   jax.ShapeDtypeStruct((n, 2 * nh), BF16),
                   jax.ShapeDtypeStruct((n, 2 * nh), BF16),
                   jax.ShapeDtypeStruct((n // CHUNK, SSD_GROUPS, ROWS_PER_GROUP, CHUNK), F32)],
        compiler_params=_params(("parallel",)),
        name="ssd_prep",
    )(dt, a_log)


SSD_T = 16


def _ssd_kernel(x_ref, b_ref, c_ref, z_ref, cum_ref, wdec_ref, ecum_ref, rows_ref, d_ref, e_ref,
                y_ref, gst_scr, h_scr):
    g = pl.program_id(1)
    phase = pl.program_id(2)
    step = pl.program_id(3)
    ns = pl.num_programs(3)
    L = CHUNK
    GX = GROUP_X
    H = SSD_HPG

    @pl.when(step == 0)
    def _():
        h_scr[...] = jnp.zeros_like(h_scr)

    def state_update(t, direction, xf, bm):
        sl = slice(t * L, (t + 1) * L)
        e_dir = e_ref[0, :, direction * GX:(direction + 1) * GX]
        w = _dot(wdec_ref[sl, :], e_dir)
        dec = _dot(_split2(rows_ref[t, 0, 3 * H:4 * H, :]), e_dir)[0:1, :]
        xdec = (w * xf).astype(BF16)
        b_t = bm.astype(F32).T.astype(BF16)
        h_scr[...] = dec * h_scr[...] + _dot(b_t, xdec)

    @pl.when(phase == 0)
    def _():
        for t in reversed(range(SSD_T)):
            sl = slice(t * L, (t + 1) * L)
            gst_scr[(ns - 1 - step) * SSD_T + t] = h_scr[...].astype(BF16)
            state_update(t, 1, x_ref[sl, :].astype(F32), b_ref[sl, :])

    @pl.when(phase == 1)
    def _():
        row = lax.broadcasted_iota(jnp.int32, (L, L), 0)
        col = lax.broadcasted_iota(jnp.int32, (L, L), 1)
        lower = row >= col
        diag = row == col
        low_lane = col < SSD_HEADDIM
        shift = (2 * SSD_HEADS - H * g) % (2 * SSD_HEADS)
        for t in range(SSD_T):
            sl = slice(t * L, (t + 1) * L)
            cm = c_ref[sl, :]
            bm = b_ref[sl, :]
            xb = x_ref[sl, :]
            xf = xb.astype(F32)
            cb = lax.dot_general(cm, bm, (((1,), (1,)), ((), ())), preferred_element_type=F32)
            cum_r = pltpu.roll(cum_ref[sl, :], shift, axis=1)
            pairs = []
            for k in range(H // 2):
                ms = []
                for half in range(2):
                    hh = 2 * k + half
                    fwd = cum_r[:, hh:hh + 1] - rows_ref[t, 0, hh:hh + 1, :]
                    bwd = cum_r[:, SSD_HEADS + hh:SSD_HEADS + hh + 1] - rows_ref[t, 0, H + hh:H + hh + 1, :]
                    arg = jnp.where(diag, rows_ref[t, 0, 2 * H + hh:2 * H + hh + 1, :],
                                    jnp.where(lower, fwd, bwd))
                    ms.append((jnp.exp2(arg) * cb).astype(BF16))
                xp = xb[:, k * L:(k + 1) * L]
                zero = jnp.zeros_like(xp)
                rhs = jnp.concatenate([jnp.where(low_lane, xp, zero), jnp.where(low_lane, zero, xp)], axis=0)
                pairs.append(_dot(jnp.concatenate(ms, axis=1), rhs))
            y = jnp.concatenate(pairs, axis=1)
            c = step * SSD_T + t
            states = jnp.concatenate([h_scr[...].astype(BF16), gst_scr[c]], axis=1)
            y_off = _dot(ecum_ref[sl, :], e_ref[0]) * _dot(cm, states)
            y = y + y_off[:, :GX] + y_off[:, GX:] + d_ref[...] * xf
            y_ref[sl, :] = (y * _silu(z_ref[sl, :].astype(F32))).astype(BF16)
            state_update(t, 0, xf, bm)


def _ssd(xc, u, cum, wdec, ecum, rows, d_exp, e01, batch, seq):
    n = xc.shape[0]
    nc = seq // CHUNK
    ns = nc // SSD_T
    tl = SSD_T * CHUNK
    nh = 2 * SSD_HEADS
    gx_blocks = D_INNER // D_STATE

    def fb_row(b, p, s):
        return b * ns + jnp.where(p == 0, ns - 1 - s, s)

    def f_row(b, p, s):
        return b * ns + jnp.where(p == 0, 0, s)

    return pl.pallas_call(
        _ssd_kernel,
        grid=(batch, SSD_GROUPS, 2, ns),
        in_specs=[
            pl.BlockSpec((tl, GROUP_X), lambda b, g, p, s: (fb_row(b, p, s), g)),
            pl.BlockSpec((tl, D_STATE), lambda b, g, p, s: (fb_row(b, p, s), gx_blocks + g)),
            pl.BlockSpec((tl, D_STATE), lambda b, g, p, s: (f_row(b, p, s), gx_blocks + SSD_GROUPS + g)),
            pl.BlockSpec((tl, GROUP_X), lambda b, g, p, s: (f_row(b, p, s), U_Z // GROUP_X + g)),
            pl.BlockSpec((tl, nh), lambda b, g, p, s: (f_row(b, p, s), 0)),
            pl.BlockSpec((tl, 2 * nh), lambda b, g, p, s: (fb_row(b, p, s), 0)),
            pl.BlockSpec((tl, 2 * nh), lambda b, g, p, s: (f_row(b, p, s), 0)),
            pl.BlockSpec((SSD_T, 1, ROWS_PER_GROUP, CHUNK), lambda b, g, p, s: (fb_row(b, p, s), g, 0, 0)),
            pl.BlockSpec((1, GROUP_X), lambda b, g, p, s: (0, g)),
            pl.BlockSpec((1, 2 * nh, 2 * GROUP_X), lambda b, g, p, s: (g, 0, 0)),
        ],
        out_specs=pl.BlockSpec((tl, GROUP_X), lambda b, g, p, s: (f_row(b, p, s), g)),
        out_shape=jax.ShapeDtypeStruct((n, D_INNER), BF16),
        scratch_shapes=[
            pltpu.VMEM((nc, D_STATE, GROUP_X), BF16),
            pltpu.VMEM((D_STATE, GROUP_X), F32),
        ],
        compiler_params=_params(("arbitrary", "arbitrary", "arbitrary", "arbitrary")),
        name="ssd",
    )(xc, xc, xc, u, cum, wdec, ecum, rows, d_exp, e01)


def _ssd_expand_matrix():
    e = np.zeros((SSD_GROUPS, 2 * SSD_HEADS, 2 * GROUP_X), np.float32)
    for g in range(SSD_GROUPS):
        for d in range(2):
            for r in range(SSD_HPG):
                lo = d * GROUP_X + r * SSD_HEADDIM
                e[g, d * SSD_HEADS + g * SSD_HPG + r, lo:lo + SSD_HEADDIM] = 1.0
    return np.concatenate([e, e], axis=1)


def _t5_bucket(rel):
    nb = N_BUCKETS // 2
    ret = (rel > 0).astype(np.int32) * nb
    n = np.abs(rel)
    max_exact = nb // 2
    large = max_exact + (np.log(np.maximum(n, 1) / max_exact) / math.log(MAX_DIST / max_exact)
                         * (nb - max_exact)).astype(np.int32)
    large = np.minimum(large, nb - 1)
    return (ret + np.where(n < max_exact, n, large)).astype(np.int32)


def _bucket_table():
    i = np.arange(ATT_BLOCK)[:, None]
    j = np.arange(3 * ATT_BLOCK)[None, :]
    rel = j - ATT_BLOCK - i
    return np.where(np.abs(rel) <= WINDOW, _t5_bucket(rel), -1).astype(np.int32)


N_EDGE_VARIANTS = 4


def _bias_kernel(rel_ref, bucket_ref, o_ref):
    variant = pl.program_id(0)
    bucket = bucket_ref[...]
    kcol = lax.broadcasted_iota(jnp.int32, bucket.shape, 1)
    dead = (((variant & 1) == 1) & (kcol < ATT_BLOCK)) | (((variant & 2) == 2) & (kcol >= 2 * ATT_BLOCK))
    for h in range(ATT_Q_HEADS):
        acc = jnp.full(bucket.shape, NEG_BIG, F32)
        for b in range(N_BUCKETS):
            acc = jnp.where(bucket == b, rel_ref[b, h] * LOG2E, acc)
        o_ref[0, h] = jnp.where(dead, NEG_BIG, acc)


def _bias_table(rel_bias):
    bucket = jnp.asarray(_bucket_table())
    return pl.pallas_call(
        _bias_kernel,
        grid=(N_EDGE_VARIANTS,),
        in_specs=[
            pl.BlockSpec(memory_space=pltpu.SMEM),
            pl.BlockSpec((ATT_BLOCK, 3 * ATT_BLOCK), lambda v: (0, 0)),
        ],
        out_specs=pl.BlockSpec((1, ATT_Q_HEADS, ATT_BLOCK, 3 * ATT_BLOCK), lambda v: (v, 0, 0, 0)),
        out_shape=jax.ShapeDtypeStruct((N_EDGE_VARIANTS, ATT_Q_HEADS, ATT_BLOCK, 3 * ATT_BLOCK), F32),
        compiler_params=_params(("arbitrary",)),
        name="rel_bias_table",
    )(rel_bias, bucket)


def _attn_block(q_ref, kp_ref, kc_ref, kn_ref, vp_ref, vc_ref, vn_ref, bias_ref, sink_ref, store):
    T = ATT_BLOCK
    hd = ATT_HEAD_DIM
    ones = jnp.ones((3 * T, hd), BF16)
    for g in range(ATT_KV_HEADS):
        ks = slice(g * hd, (g + 1) * hd)
        kwin = jnp.concatenate([kp_ref[:, ks], kc_ref[:, ks], kn_ref[:, ks]], axis=0)
        vwin = jnp.concatenate([vp_ref[:, ks], vc_ref[:, ks], vn_ref[:, ks]], axis=0)
        qs = jnp.concatenate(
            [q_ref[:, (g * ATT_REP + r) * hd:(g * ATT_REP + r + 1) * hd] for r in range(ATT_REP)],
            axis=0)
        s = lax.dot_general(qs, kwin, (((1,), (1,)), ((), ())), preferred_element_type=F32)
        s = s + jnp.concatenate([bias_ref[0, g * ATT_REP + r] for r in range(ATT_REP)], axis=0)
        sink = jnp.concatenate(
            [jnp.full((T, 1), sink_ref[g * ATT_REP + r] * LOG2E, F32) for r in range(ATT_REP)], axis=0)
        m = jnp.maximum(jnp.max(s, axis=-1, keepdims=True), sink)
        p = jnp.exp2(s - m).astype(BF16)
        ov = _dot(p, jnp.concatenate([vwin, ones], axis=1))
        o = ov[:, :hd] / (ov[:, hd:] + jnp.exp2(sink - m))
        for r in range(ATT_REP):
            store(g * ATT_REP + r, o[r * T:(r + 1) * T, :].astype(BF16))


BR_TM = 512
BR_TN = 512
BR_STEPS = D_MODEL // BR_TN
assert BR_TM == BR_STEPS * ATT_BLOCK


def _attn_branch_kernel(q_ref, kp_ref, kc_ref, kn_ref, vp_ref, vc_ref, vn_ref, bias_ref, sink_ref,
                        ys_ref, ga_ref, gb_ref, ng_ref, ws_ref, wa_ref, o_ref, ya_even, ya_odd):
    i = pl.program_id(0)
    j = pl.program_id(1)
    hd = ATT_HEAD_DIM

    @pl.when((i == 0) & (j == 0))
    def _():
        ya_odd[...] = jnp.zeros_like(ya_odd)

    def step(fill_scr, use_scr):
        row0 = pl.multiple_of(j * ATT_BLOCK, ATT_BLOCK)

        def store(head, val):
            fill_scr[pl.ds(row0, ATT_BLOCK), head * hd:(head + 1) * hd] = val

        _attn_block(q_ref, kp_ref, kc_ref, kn_ref, vp_ref, vc_ref, vn_ref, bias_ref, sink_ref, store)
        yg = ys_ref[...]
        yf = yg.astype(F32)
        scale = lax.rsqrt(jnp.mean(yf * yf, axis=-1, keepdims=True) + EPS)
        a = jax.nn.sigmoid(ga_ref[...].astype(F32)) * (scale * _dot(yg * ng_ref[...].astype(BF16), ws_ref[...]))
        b = jax.nn.sigmoid(gb_ref[...].astype(F32)) * _dot(use_scr[...], wa_ref[...])
        o_ref[...] = (a + b).astype(BF16)

    @pl.when(i % 2 == 0)
    def _():
        step(ya_even, ya_odd)

    @pl.when(i % 2 == 1)
    def _():
        step(ya_odd, ya_even)


def _attn_branch(ys, u, bias, sink, norm_gain, w_ssd, w_att, seq):
    n = ys.shape[0]
    nblk = n // ATT_BLOCK
    ntile = n // BR_TM
    blocks_per_seq = seq // ATT_BLOCK
    kcol = U_K // KV_DIM
    vcol = U_V // KV_DIM
    blk = lambda i, j: jnp.minimum(i * BR_STEPS + j, nblk - 1)
    prev = lambda i, j: jnp.maximum(blk(i, j) - 1, 0)
    nxt = lambda i, j: jnp.minimum(blk(i, j) + 1, nblk - 1)
    tile = lambda i: jnp.maximum(i - 1, 0)
    kv_spec = lambda rowf, c: pl.BlockSpec((ATT_BLOCK, KV_DIM), lambda i, j: (rowf(i, j), c))

    def edge_variant(i, j):
        b = blk(i, j)
        first = (b % blocks_per_seq == 0).astype(jnp.int32)
        last = ((b + 1) % blocks_per_seq == 0).astype(jnp.int32)
        return first + 2 * last

    return pl.pallas_call(
        _attn_branch_kernel,
        grid=(ntile + 1, BR_STEPS),
        in_specs=[
            pl.BlockSpec((ATT_BLOCK, D_MODEL), lambda i, j: (blk(i, j), U_Q // D_MODEL)),
            kv_spec(prev, kcol), kv_spec(blk, kcol), kv_spec(nxt, kcol),
            kv_spec(prev, vcol), kv_spec(blk, vcol), kv_spec(nxt, vcol),
            pl.BlockSpec((1, ATT_Q_HEADS, ATT_BLOCK, 3 * ATT_BLOCK), lambda i, j: (edge_variant(i, j), 0, 0, 0)),
            pl.BlockSpec(memory_space=pltpu.SMEM),
            pl.BlockSpec((BR_TM, D_INNER), lambda i, j: (tile(i), 0)),
            pl.BlockSpec((BR_TM, BR_TN), lambda i, j: (tile(i), U_GA // BR_TN + j)),
            pl.BlockSpec((BR_TM, BR_TN), lambda i, j: (tile(i), U_GB // BR_TN + j)),
            pl.BlockSpec((1, D_INNER), lambda i, j: (0, 0)),
            pl.BlockSpec((D_INNER, BR_TN), lambda i, j: (0, j)),
            pl.BlockSpec((D_MODEL, BR_TN), lambda i, j: (0, j)),
        ],
        out_specs=pl.BlockSpec((BR_TM, BR_TN), lambda i, j: (i, j)),
        out_shape=jax.ShapeDtypeStruct((n + BR_TM, D_MODEL), BF16),
        scratch_shapes=[pltpu.VMEM((BR_TM, D_MODEL), BF16), pltpu.VMEM((BR_TM, D_MODEL), BF16)],
        compiler_params=_params(("arbitrary", "arbitrary")),
        name="attn_branch",
    )(u, u, u, u, u, u, u, bias, sink, ys, u, u, norm_gain, w_ssd, w_att)


OUT_TM = 512


def _outproj_kernel(m_ref, w_ref, x_ref, o_ref):
    o_ref[...] = x_ref[...] + _dot(m_ref[...], w_ref[...])


def _outproj(merged, w_out, x1):
    assert OUT_TM == BR_TM
    n = x1.shape[0]
    return pl.pallas_call(
        _outproj_kernel,
        grid=(n // OUT_TM,),
        in_specs=[
            pl.BlockSpec((OUT_TM, D_MODEL), lambda i: (i + 1, 0)),
            pl.BlockSpec((D_MODEL, D_MODEL), lambda i: (0, 0)),
            pl.BlockSpec((OUT_TM, D_MODEL), lambda i: (i, 0)),
        ],
        out_specs=pl.BlockSpec((OUT_TM, D_MODEL), lambda i: (i, 0)),
        out_shape=jax.ShapeDtypeStruct((n, D_MODEL), F32),
        compiler_params=_params(("parallel",)),
        name="out_proj",
    )(merged, w_out, x1)


def _prepare(ffn1_norm, ffn1_w_gate, ffn1_w_up, ffn1_w_down, mix_norm, w_in, conv_w, conv_b,
             ssd_A_log, ssd_dt_bias, ssd_D, ssd_out_norm, q_norm, k_norm, attn_sink, rel_bias,
             w_branch_ssd, w_branch_attn, w_out, ffn2_norm, ffn2_w_gate, ffn2_w_up, ffn2_w_down,
             final_norm):
    l = 0
    w = w_in[l]
    o_z = 0
    o_xbc = o_z + D_INNER
    o_dt = o_xbc + CONV_DIM
    o_q = o_dt + 2 * SSD_HEADS
    o_k = o_q + D_MODEL
    o_v = o_k + KV_DIM
    o_ga = o_v + KV_DIM
    o_gb = o_ga + D_MODEL
    w_u = jnp.concatenate(
        [w[:, o_z:o_dt], w[:, o_q:o_k], w[:, o_ga:o_gb + D_MODEL], w[:, o_k:o_ga]], axis=1).astype(BF16)
    row = lambda v: v.reshape(1, -1).astype(F32)
    return dict(
        ffn1=(row(ffn1_norm[l]), ffn1_w_gate[l].astype(BF16), ffn1_w_up[l].astype(BF16),
              ffn1_w_down[l].astype(BF16), row(mix_norm[l])),
        ffn2=(row(ffn2_norm[l]), ffn2_w_gate[l].astype(BF16), ffn2_w_up[l].astype(BF16),
              ffn2_w_down[l].astype(BF16), row(final_norm[l])),
        w_u=w_u, w_dt=w[:, o_dt:o_q].astype(BF16), dt_bias=row(ssd_dt_bias[l]),
        conv_w=conv_w[l].astype(F32), conv_b=row(conv_b[l]),
        a_log=row(ssd_A_log[l]), d_exp=row(jnp.repeat(ssd_D[l], SSD_HEADDIM)),
        e01=jnp.asarray(_ssd_expand_matrix(), BF16), out_norm=row(ssd_out_norm[l]),
        q_gain=row(q_norm[l]), k_gain=row(k_norm[l]), sink=attn_sink[l].astype(F32),
        bias=_bias_table(rel_bias.astype(F32)),
        w_ssd=w_branch_ssd[l].astype(BF16), w_att=w_branch_attn[l].astype(BF16),
        w_out=w_out[l].astype(BF16),
    )


def _layer(x, p):
    batch, seq, _ = x.shape
    x2d = x.reshape(batch * seq, D_MODEL)
    x1, h = _ffn(x2d, *p["ffn1"], final=False)
    u, dt = _inproj(h, p["w_u"], p["w_dt"], p["dt_bias"], p["q_gain"], p["k_gain"])
    xc = _conv(u, p["conv_w"], p["conv_b"], seq)
    cum, wdec, ecum, rows = _ssd_prep(dt, p["a_log"])
    ys = _ssd(xc, u, cum, wdec, ecum, rows, p["d_exp"], p["e01"], batch, seq)
    merged = _attn_branch(ys, u, p["bias"], p["sink"], p["out_norm"], p["w_ssd"], p["w_att"], seq)
    x2 = _outproj(merged, p["w_out"], x1)
    (y,) = _ffn(x2, *p["ffn2"], final=True)
    return y.reshape(batch, seq, D_MODEL)


def kernel(x_prompt, x_sample, ffn1_norm, ffn1_w_gate, ffn1_w_up, ffn1_w_down, mix_norm, w_in, conv_w, conv_b, ssd_A_log, ssd_dt_bias, ssd_D, ssd_out_norm, q_norm, k_norm, attn_sink, rel_bias, w_branch_ssd, w_branch_attn, w_out, ffn2_norm, ffn2_w_gate, ffn2_w_up, ffn2_w_down, final_norm):
    p = _prepare(ffn1_norm, ffn1_w_gate, ffn1_w_up, ffn1_w_down, mix_norm, w_in, conv_w, conv_b,
                 ssd_A_log, ssd_dt_bias, ssd_D, ssd_out_norm, q_norm, k_norm, attn_sink, rel_bias,
                 w_branch_ssd, w_branch_attn, w_out, ffn2_norm, ffn2_w_gate, ffn2_w_up, ffn2_w_down,
                 final_norm)
    return (_layer(x_prompt, p), _layer(x_sample, p))
```

```python
import functools
import math

import jax
import jax.numpy as jnp
import numpy as np
from jax import lax
from jax.experimental import pallas as pl
from jax.experimental.pallas import tpu as pltpu

D_MODEL = 2048
D_INNER = 2 * D_MODEL
SSD_HEADDIM = 64
SSD_HEADS = D_INNER // SSD_HEADDIM
SSD_GROUPS = 8
SSD_HPG = SSD_HEADS // SSD_GROUPS
D_STATE = 128
CONV_K = 5
CHUNK = 128
CONV_DIM = D_INNER + 2 * SSD_GROUPS * D_STATE
ATT_HEAD_DIM = 128
ATT_Q_HEADS = D_MODEL // ATT_HEAD_DIM
ATT_KV_HEADS = 4
ATT_REP = ATT_Q_HEADS // ATT_KV_HEADS
WINDOW = 128
ATT_BLOCK = 128
N_BUCKETS = 32
MAX_DIST = 128
D_FF = ((8 * D_MODEL // 3 + 255) // 256) * 256
EPS = 1e-6
GROUP_X = SSD_HPG * SSD_HEADDIM
KV_DIM = ATT_KV_HEADS * ATT_HEAD_DIM

U_Z = 0
U_XBC = U_Z + D_INNER
U_Q = U_XBC + CONV_DIM
U_GA = U_Q + D_MODEL
U_GB = U_GA + D_MODEL
U_K = U_GB + D_MODEL
U_V = U_K + KV_DIM
U_TOTAL = U_V + KV_DIM

V7X_VMEM_LIMIT_BYTES = 56 * 1024 * 1024
NEG_BIG = -1e30

BF16 = jnp.bfloat16
F32 = jnp.float32


def _params(semantics):
    return pltpu.CompilerParams(dimension_semantics=semantics,
                                vmem_limit_bytes=V7X_VMEM_LIMIT_BYTES)


def _rms(x, gain):
    return x * lax.rsqrt(jnp.mean(x * x, axis=-1, keepdims=True) + EPS) * gain


def _silu(x):
    return x * jax.nn.sigmoid(x)


def _dot(a, b):
    return jnp.dot(a, b, preferred_element_type=F32)


FFN_TM = 512
FFN_TF = 512


def _ffn_kernel(x_ref, g_ref, wg_ref, wu_ref, wd_ref, pg_ref, *refs, final):
    if final:
        y_ref, h_scr, acc_scr = refs
    else:
        x1_ref, h_ref, h_scr, acc_scr = refs
    j = pl.program_id(1)

    @pl.when(j == 0)
    def _():
        h_scr[...] = _rms(x_ref[...], g_ref[...]).astype(BF16)
        acc_scr[...] = jnp.zeros_like(acc_scr)

    h = h_scr[...]
    t = _silu(_dot(h, wg_ref[...])) * _dot(h, wu_ref[...])
    acc_scr[...] += _dot(t.astype(BF16), wd_ref[...])

    @pl.when(j == pl.num_programs(1) - 1)
    def _():
        x1 = x_ref[...] + 0.5 * acc_scr[...]
        normed = _rms(x1, pg_ref[...])
        if final:
            y_ref[...] = normed
        else:
            x1_ref[...] = x1
            h_ref[...] = normed.astype(BF16)


def _ffn(x, gain, wg, wu, wd, post_gain, final):
    n = x.shape[0]
    row = lambda i, j: (i, 0)
    fixed = lambda i, j: (0, 0)
    out_shape = [jax.ShapeDtypeStruct((n, D_MODEL), F32)]
    out_specs = [pl.BlockSpec((FFN_TM, D_MODEL), row)]
    if not final:
        out_shape.append(jax.ShapeDtypeStruct((n, D_MODEL), BF16))
        out_specs.append(pl.BlockSpec((FFN_TM, D_MODEL), row))
    return pl.pallas_call(
        functools.partial(_ffn_kernel, final=final),
        grid=(n // FFN_TM, D_FF // FFN_TF),
        in_specs=[
            pl.BlockSpec((FFN_TM, D_MODEL), row),
            pl.BlockSpec((1, D_MODEL), fixed),
            pl.BlockSpec((D_MODEL, FFN_TF), lambda i, j: (0, j)),
            pl.BlockSpec((D_MODEL, FFN_TF), lambda i, j: (0, j)),
            pl.BlockSpec((FFN_TF, D_MODEL), lambda i, j: (j, 0)),
            pl.BlockSpec((1, D_MODEL), fixed),
        ],
        out_specs=out_specs,
        out_shape=out_shape,
        scratch_shapes=[pltpu.VMEM((FFN_TM, D_MODEL), BF16), pltpu.VMEM((FFN_TM, D_MODEL), F32)],
        compiler_params=_params(("parallel", "arbitrary")),
        name="ffn_final" if final else "ffn",
    )(x, gain, wg, wu, wd, post_gain)


INP_TM = 2048
INP_TN = 1024


def _inproj_kernel(h_ref, w_ref, wdt_ref, bdt_ref, qg_ref, kg_ref, u_ref, dt_ref):
    h = h_ref[...]
    acc = _dot(h, w_ref[...])
    j = pl.program_id(1)
    hd = ATT_HEAD_DIM
    heads_per_tile = INP_TN // hd
    is_q = (j >= U_Q // INP_TN) & (j < U_GA // INP_TN)
    is_kv = j == U_K // INP_TN

    u_ref[...] = acc.astype(BF16)

    def norm_heads(gain, n_normed):
        for hh in range(n_normed):
            cols = slice(hh * hd, (hh + 1) * hd)
            u_ref[:, cols] = _rms(u_ref[:, cols].astype(F32), gain).astype(BF16)

    @pl.when(is_q)
    def _():
        norm_heads(qg_ref[...] * (hd ** -0.5 * LOG2E), heads_per_tile)

    @pl.when(is_kv)
    def _():
        norm_heads(kg_ref[...], ATT_KV_HEADS)

    @pl.when(j == 0)
    def _():
        raw = _dot(h, wdt_ref[...]) + bdt_ref[...]
        dt_ref[...] = jnp.maximum(raw, 0.0) + jnp.log1p(jnp.exp(-jnp.abs(raw)))


def _inproj(h, w_u, w_dt, dt_bias, q_gain, k_gain):
    assert U_Q % INP_TN == 0 and U_GA % INP_TN == 0 and U_K % INP_TN == 0 and KV_DIM * 2 == INP_TN
    n = h.shape[0]
    ndt = 2 * SSD_HEADS
    return pl.pallas_call(
        _inproj_kernel,
        grid=(n // INP_TM, U_TOTAL // INP_TN),
        in_specs=[
            pl.BlockSpec((INP_TM, D_MODEL), lambda i, j: (i, 0)),
            pl.BlockSpec((D_MODEL, INP_TN), lambda i, j: (0, j)),
            pl.BlockSpec((D_MODEL, ndt), lambda i, j: (0, 0)),
            pl.BlockSpec((1, ndt), lambda i, j: (0, 0)),
            pl.BlockSpec((1, ATT_HEAD_DIM), lambda i, j: (0, 0)),
            pl.BlockSpec((1, ATT_HEAD_DIM), lambda i, j: (0, 0)),
        ],
        out_specs=[
            pl.BlockSpec((INP_TM, INP_TN), lambda i, j: (i, j)),
            pl.BlockSpec((INP_TM, ndt), lambda i, j: (i, 0)),
        ],
        out_shape=[jax.ShapeDtypeStruct((n, U_TOTAL), BF16), jax.ShapeDtypeStruct((n, ndt), F32)],
        compiler_params=_params(("parallel", "arbitrary")),
        name="in_proj",
    )(h, w_u, w_dt, dt_bias, q_gain, k_gain)


CONV_TS = 512
CONV_TC = 2048
CONV_HALO = 16
CONV_TB = 32
CONV_WIN = CONV_TB + 2 * CONV_HALO
CONV_SIDE_TAPS = tuple(k for k in range(CONV_K) if k != CONV_K // 2)


def _conv_kernel(xm_ref, xp_ref, xn_ref, w_ref, b_ref, o_ref, xs_scr, pw_scr, *, tiles_per_seq):
    i = pl.program_id(0)
    first = (i % tiles_per_seq) == 0
    last = ((i + 1) % tiles_per_seq) == 0
    H = CONV_HALO
    zero_halo = jnp.zeros((H, CONV_TC), BF16)
    xs_scr[0:H, :] = jnp.where(first, zero_halo, xp_ref[...])
    xs_scr[H:H + CONV_TS, :] = xm_ref[...]
    xs_scr[H + CONV_TS:, :] = jnp.where(last, zero_halo, xn_ref[...])
    xs = xs_scr[...]
    for n, k in enumerate(CONV_SIDE_TAPS):
        pw_scr[n] = xs * (0.5 * w_ref[k:k + 1, :]).astype(BF16)
    r = lax.broadcasted_iota(jnp.int32, (CONV_TB, len(CONV_SIDE_TAPS) * CONV_WIN), 0)
    c = lax.broadcasted_iota(jnp.int32, (CONV_TB, len(CONV_SIDE_TAPS) * CONV_WIN), 1)
    hit = None
    for n, k in enumerate(CONV_SIDE_TAPS):
        sel = c == r + (n * CONV_WIN + H + k - CONV_K // 2)
        hit = sel if hit is None else hit | sel
    shift = jnp.where(hit, 1.0, 0.0).astype(BF16)
    w_mid = 0.5 * w_ref[CONV_K // 2:CONV_K // 2 + 1, :]
    bias = 0.5 * b_ref[...]
    for blk in range(CONV_TS // CONV_TB):
        t0 = blk * CONV_TB
        taps = jnp.concatenate([pw_scr[n, t0:t0 + CONV_WIN, :] for n in range(len(CONV_SIDE_TAPS))], axis=0)
        mid = xs_scr[H + t0:H + t0 + CONV_TB, :].astype(F32) * w_mid + bias
        half = _dot(shift, taps) + mid
        o_ref[t0:t0 + CONV_TB, :] = (half + half * jnp.tanh(half)).astype(BF16)


def _conv(u, conv_w, conv_b, seq):
    n = u.shape[0]
    tiles_per_seq = seq // CONV_TS
    halo_per_tile = CONV_TS // CONV_HALO
    n_halo = n // CONV_HALO
    col0 = U_XBC // CONV_TC
    return pl.pallas_call(
        functools.partial(_conv_kernel, tiles_per_seq=tiles_per_seq),
        grid=(n // CONV_TS, CONV_DIM // CONV_TC),
        in_specs=[
            pl.BlockSpec((CONV_TS, CONV_TC), lambda i, j: (i, col0 + j)),
            pl.BlockSpec((CONV_HALO, CONV_TC),
                         lambda i, j: (jnp.maximum(i * halo_per_tile - 1, 0), col0 + j)),
            pl.BlockSpec((CONV_HALO, CONV_TC),
                         lambda i, j: (jnp.minimum((i + 1) * halo_per_tile, n_halo - 1), col0 + j)),
            pl.BlockSpec((CONV_K, CONV_TC), lambda i, j: (0, j)),
            pl.BlockSpec((1, CONV_TC), lambda i, j: (0, j)),
        ],
        out_specs=pl.BlockSpec((CONV_TS, CONV_TC), lambda i, j: (i, j)),
        out_shape=jax.ShapeDtypeStruct((n, CONV_DIM), BF16),
        scratch_shapes=[pltpu.VMEM((CONV_TS + 2 * CONV_HALO, CONV_TC), BF16),
                        pltpu.VMEM((len(CONV_SIDE_TAPS), CONV_TS + 2 * CONV_HALO, CONV_TC), BF16)],
        compiler_params=_params(("parallel", "parallel")),
        name="conv",
    )(u, u, u, conv_w, conv_b)


def _split3(a):
    hi = a.astype(BF16)
    r1 = a - hi.astype(F32)
    mid = r1.astype(BF16)
    lo = (r1 - mid.astype(F32)).astype(BF16)
    return hi, mid, lo


def _exact_lhs01(m01, a):
    hi, mid, lo = _split3(a)
    return _dot(jnp.concatenate([m01, m01, m01], axis=1), jnp.concatenate([hi, mid, lo], axis=0))


def _split2(a):
    hi = a.astype(BF16)
    mid = (a - hi.astype(F32)).astype(BF16)
    return jnp.concatenate([hi, mid], axis=1)


PREP_CHUNKS = 8
ROWS_PER_GROUP = 4 * SSD_HPG
LOG2E = 1.4426950408889634


def _ssd_prep_kernel(dt_ref, a_ref, cum_ref, wdec_ref, ecum_ref, rows_ref):
    L = CHUNK
    H = SSD_HPG
    row = lax.broadcasted_iota(jnp.int32, (L, L), 0)
    col = lax.broadcasted_iota(jnp.int32, (L, L), 1)
    lower = (row >= col).astype(BF16)
    upper = (row <= col).astype(BF16)
    is_fwd = col < SSD_HEADS
    a = -jnp.exp(a_ref[...])
    for t in range(PREP_CHUNKS):
        sl = slice(t * L, (t + 1) * L)
        dt = dt_ref[sl, :]
        dA = dt * a
        pre = _exact_lhs01(lower, dA)
        suf = _exact_lhs01(upper, dA)
        cum = jnp.where(is_fwd, pre, suf)
        tot = jnp.where(is_fwd[:1], pre[L - 1:L, :], suf[0:1, :])
        cum_ref[sl, :] = cum * LOG2E
        wdec_ref[sl, :] = _split2(jnp.exp(tot - cum) * dt)
        ecum_ref[sl, :] = _split2(jnp.exp(cum))
        dt_t = dt.T
        src_t = (cum.T - jnp.log(dt_t)) * LOG2E
        lds_t = jnp.log2(dt_t[:SSD_HEADS] + dt_t[SSD_HEADS:])
        etot = jnp.broadcast_to(jnp.exp(tot), (H, L))
        for g in range(SSD_GROUPS):
            f = slice(g * H, (g + 1) * H)
            b = slice(SSD_HEADS + g * H, SSD_HEADS + (g + 1) * H)
            rows_ref[t, g] = jnp.concatenate([src_t[f], src_t[b], lds_t[f], etot], axis=0)


def _ssd_prep(dt, a_log):
    n = dt.shape[0]
    rows_blk = PREP_CHUNKS * CHUNK
    nh = 2 * SSD_HEADS
    return pl.pallas_call(
        _ssd_prep_kernel,
        grid=(n // rows_blk,),
        in_specs=[pl.BlockSpec((rows_blk, nh), lambda i: (i, 0)), pl.BlockSpec((1, nh), lambda i: (0, 0))],
        out_specs=[pl.BlockSpec((rows_blk, nh), lambda i: (i, 0)),
                   pl.BlockSpec((rows_blk, 2 * nh), lambda i: (i, 0)),
                   pl.BlockSpec((rows_blk, 2 * nh), lambda i: (i, 0)),
                   pl.BlockSpec((PREP_CHUNKS, SSD_GROUPS, ROWS_PER_GROUP, CHUNK), lambda i: (i, 0, 0, 0))],
        out_shape=[jax.ShapeDtypeStruct((n, nh), F32),
                   jax.ShapeDtypeStruct((n, 2 * nh), BF16),
                   jax.ShapeDtypeStruct((n, 2 * nh), BF16),
                   jax.ShapeDtypeStruct((n // CHUNK, SSD_GROUPS, ROWS_PER_GROUP, CHUNK), F32)],
        compiler_params=_params(("parallel",)),
        name="ssd_prep",
    )(dt, a_log)


SSD_T = 16


def _ssd_kernel(x_ref, b_ref, c_ref, z_ref, cum_ref, wdec_ref, ecum_ref, rows_ref, d_ref, e_ref,
                y_ref, gst_scr, h_scr):
    g = pl.program_id(1)
    phase = pl.program_id(2)
    step = pl.program_id(3)
    ns = pl.num_programs(3)
    L = CHUNK
    GX = GROUP_X
    H = SSD_HPG

    @pl.when(step == 0)
    def _():
        h_scr[...] = jnp.zeros_like(h_scr)

    def state_update(t, direction, xf, bm):
        sl = slice(t * L, (t + 1) * L)
        e_dir = e_ref[0, :, direction * GX:(direction + 1) * GX]
        w = _dot(wdec_ref[sl, :], e_dir)
        dec = _dot(_split2(rows_ref[t, 0, 3 * H:4 * H, :]), e_dir)[0:1, :]
        xdec = (w * xf).astype(BF16)
        b_t = bm.astype(F32).T.astype(BF16)
        h_scr[...] = dec * h_scr[...] + _dot(b_t, xdec)

    @pl.when(phase == 0)
    def _():
        for t in reversed(range(SSD_T)):
            sl = slice(t * L, (t + 1) * L)
            gst_scr[(ns - 1 - step) * SSD_T + t] = h_scr[...].astype(BF16)
            state_update(t, 1, x_ref[sl, :].astype(F32), b_ref[sl, :])

    @pl.when(phase == 1)
    def _():
        row = lax.broadcasted_iota(jnp.int32, (L, L), 0)
        col = lax.broadcasted_iota(jnp.int32, (L, L), 1)
        lower = row >= col
        diag = row == col
        low_lane = col < SSD_HEADDIM
        shift = (2 * SSD_HEADS - H * g) % (2 * SSD_HEADS)
        for t in range(SSD_T):
            sl = slice(t * L, (t + 1) * L)
            cm = c_ref[sl, :]
            bm = b_ref[sl, :]
            xb = x_ref[sl, :]
            xf = xb.astype(F32)
            cb = lax.dot_general(cm, bm, (((1,), (1,)), ((), ())), preferred_element_type=F32)
            cum_r = pltpu.roll(cum_ref[sl, :], shift, axis=1)
            pairs = []
            for k in range(H // 2):
                ms = []
                for half in range(2):
                    hh = 2 * k + half
                    fwd = cum_r[:, hh:hh + 1] - rows_ref[t, 0, hh:hh + 1, :]
                    bwd = cum_r[:, SSD_HEADS + hh:SSD_HEADS + hh + 1] - rows_ref[t, 0, H + hh:H + hh + 1, :]
                    arg = jnp.where(diag, rows_ref[t, 0, 2 * H + hh:2 * H + hh + 1, :],
                                    jnp.where(lower, fwd, bwd))
                    ms.append((jnp.exp2(arg) * cb).astype(BF16))
                xp = xb[:, k * L:(k + 1) * L]
                zero = jnp.zeros_like(xp)
                rhs = jnp.concatenate([jnp.where(low_lane, xp, zero), jnp.where(low_lane, zero, xp)], axis=0)
                pairs.append(_dot(jnp.concatenate(ms, axis=1), rhs))
            y = jnp.concatenate(pairs, axis=1)
            c = step * SSD_T + t
            states = jnp.concatenate([h_scr[...].astype(BF16), gst_scr[c]], axis=1)
            y_off = _dot(ecum_ref[sl, :], e_ref[0]) * _dot(cm, states)
            y = y + y_off[:, :GX] + y_off[:, GX:] + d_ref[...] * xf
            y_ref[sl, :] = (y * _silu(z_ref[sl, :].astype(F32))).astype(BF16)
            state_update(t, 0, xf, bm)


def _ssd(xc, u, cum, wdec, ecum, rows, d_exp, e01, batch, seq):
    n = xc.shape[0]
    nc = seq // CHUNK
    ns = nc // SSD_T
    tl = SSD_T * CHUNK
    nh = 2 * SSD_HEADS
    gx_blocks = D_INNER // D_STATE

    def fb_row(b, p, s):
        return b * ns + jnp.where(p == 0, ns - 1 - s, s)

    def f_row(b, p, s):
        return b * ns + jnp.where(p == 0, 0, s)

    return pl.pallas_call(
        _ssd_kernel,
        grid=(batch, SSD_GROUPS, 2, ns),
        in_specs=[
            pl.BlockSpec((tl, GROUP_X), lambda b, g, p, s: (fb_row(b, p, s), g)),
            pl.BlockSpec((tl, D_STATE), lambda b, g, p, s: (fb_row(b, p, s), gx_blocks + g)),
            pl.BlockSpec((tl, D_STATE), lambda b, g, p, s: (f_row(b, p, s), gx_blocks + SSD_GROUPS + g)),
            pl.BlockSpec((tl, GROUP_X), lambda b, g, p, s: (f_row(b, p, s), U_Z // GROUP_X + g)),
            pl.BlockSpec((tl, nh), lambda b, g, p, s: (f_row(b, p, s), 0)),
            pl.BlockSpec((tl, 2 * nh), lambda b, g, p, s: (fb_row(b, p, s), 0)),
            pl.BlockSpec((tl, 2 * nh), lambda b, g, p, s: (f_row(b, p, s), 0)),
            pl.BlockSpec((SSD_T, 1, ROWS_PER_GROUP, CHUNK), lambda b, g, p, s: (fb_row(b, p, s), g, 0, 0)),
            pl.BlockSpec((1, GROUP_X), lambda b, g, p, s: (0, g)),
            pl.BlockSpec((1, 2 * nh, 2 * GROUP_X), lambda b, g, p, s: (g, 0, 0)),
        ],
        out_specs=pl.BlockSpec((tl, GROUP_X), lambda b, g, p, s: (f_row(b, p, s), g)),
        out_shape=jax.ShapeDtypeStruct((n, D_INNER), BF16),
        scratch_shapes=[
            pltpu.VMEM((nc, D_STATE, GROUP_X), BF16),
            pltpu.VMEM((D_STATE, GROUP_X), F32),
        ],
        compiler_params=_params(("arbitrary", "arbitrary", "arbitrary", "arbitrary")),
        name="ssd",
    )(xc, xc, xc, u, cum, wdec, ecum, rows, d_exp, e01)


def _ssd_expand_matrix():
    e = np.zeros((SSD_GROUPS, 2 * SSD_HEADS, 2 * GROUP_X), np.float32)
    for g in range(SSD_GROUPS):
        for d in range(2):
            for r in range(SSD_HPG):
                lo = d * GROUP_X + r * SSD_HEADDIM
                e[g, d * SSD_HEADS + g * SSD_HPG + r, lo:lo + SSD_HEADDIM] = 1.0
    return np.concatenate([e, e], axis=1)


def _t5_bucket(rel):
    nb = N_BUCKETS // 2
    ret = (rel > 0).astype(np.int32) * nb
    n = np.abs(rel)
    max_exact = nb // 2
    large = max_exact + (np.log(np.maximum(n, 1) / max_exact) / math.log(MAX_DIST / max_exact)
                         * (nb - max_exact)).astype(np.int32)
    large = np.minimum(large, nb - 1)
    return (ret + np.where(n < max_exact, n, large)).astype(np.int32)


def _bucket_table():
    i = np.arange(ATT_BLOCK)[:, None]
    j = np.arange(3 * ATT_BLOCK)[None, :]
    rel = j - ATT_BLOCK - i
    return np.where(np.abs(rel) <= WINDOW, _t5_bucket(rel), -1).astype(np.int32)


N_EDGE_VARIANTS = 4


def _bias_kernel(rel_ref, bucket_ref, o_ref):
    variant = pl.program_id(0)
    bucket = bucket_ref[...]
    kcol = lax.broadcasted_iota(jnp.int32, bucket.shape, 1)
    dead = (((variant & 1) == 1) & (kcol < ATT_BLOCK)) | (((variant & 2) == 2) & (kcol >= 2 * ATT_BLOCK))
    for h in range(ATT_Q_HEADS):
        acc = jnp.full(bucket.shape, NEG_BIG, F32)
        for b in range(N_BUCKETS):
            acc = jnp.where(bucket == b, rel_ref[b, h] * LOG2E, acc)
        o_ref[0, h] = jnp.where(dead, NEG_BIG, acc)


def _bias_table(rel_bias):
    bucket = jnp.asarray(_bucket_table())
    return pl.pallas_call(
        _bias_kernel,
        grid=(N_EDGE_VARIANTS,),
        in_specs=[
            pl.BlockSpec(memory_space=pltpu.SMEM),
            pl.BlockSpec((ATT_BLOCK, 3 * ATT_BLOCK), lambda v: (0, 0)),
        ],
        out_specs=pl.BlockSpec((1, ATT_Q_HEADS, ATT_BLOCK, 3 * ATT_BLOCK), lambda v: (v, 0, 0, 0)),
        out_shape=jax.ShapeDtypeStruct((N_EDGE_VARIANTS, ATT_Q_HEADS, ATT_BLOCK, 3 * ATT_BLOCK), F32),
        compiler_params=_params(("arbitrary",)),
        name="rel_bias_table",
    )(rel_bias, bucket)


def _attn_kv_head(g, q_ref, kp_ref, kc_ref, kn_ref, vp_ref, vc_ref, vn_ref, bias_ref, sink_ref, store):
    T = ATT_BLOCK
    hd = ATT_HEAD_DIM
    ones = jnp.ones((3 * T, hd), BF16)
    ks = slice(g * hd, (g + 1) * hd)
    kwin = jnp.concatenate([kp_ref[:, ks], kc_ref[:, ks], kn_ref[:, ks]], axis=0)
    vwin = jnp.concatenate([vp_ref[:, ks], vc_ref[:, ks], vn_ref[:, ks]], axis=0)
    qs = jnp.concatenate(
        [q_ref[:, (g * ATT_REP + r) * hd:(g * ATT_REP + r + 1) * hd] for r in range(ATT_REP)],
        axis=0)
    s = lax.dot_general(qs, kwin, (((1,), (1,)), ((), ())), preferred_element_type=F32)
    s = s + jnp.concatenate([bias_ref[0, g * ATT_REP + r] for r in range(ATT_REP)], axis=0)
    sink = jnp.concatenate(
        [jnp.full((T, 1), sink_ref[g * ATT_REP + r] * LOG2E, F32) for r in range(ATT_REP)], axis=0)
    m = jnp.maximum(jnp.max(s, axis=-1, keepdims=True), sink)
    p = jnp.exp2(s - m).astype(BF16)
    ov = _dot(p, jnp.concatenate([vwin, ones], axis=1))
    o = ov[:, :hd] / (ov[:, hd:] + jnp.exp2(sink - m))
    for r in range(ATT_REP):
        store(g * ATT_REP + r, o[r * T:(r + 1) * T, :].astype(BF16))


BR_TM = 512
BR_TN = 512
BR_STEPS = D_MODEL // BR_TN
assert BR_TM == BR_STEPS * ATT_BLOCK


def _attn_branch_kernel(q_ref, kp_ref, kc_ref, kn_ref, vp_ref, vc_ref, vn_ref, bias_ref, sink_ref,
                        ys_ref, ga_ref, gb_ref, ng_ref, ws_ref, wa_ref, o_ref, ya_even, ya_odd):
    i = pl.program_id(0)
    j = pl.program_id(1)
    hd = ATT_HEAD_DIM

    @pl.when((i == 0) & (j == 0))
    def _():
        ya_odd[...] = jnp.zeros_like(ya_odd)

    def step(fill_scr, use_scr):
        row0 = pl.multiple_of(j * ATT_BLOCK, ATT_BLOCK)

        def store(head, val):
            fill_scr[pl.ds(row0, ATT_BLOCK), head * hd:(head + 1) * hd] = val

        n_slices = ATT_KV_HEADS
        ka = D_INNER // n_slices
        kb = D_MODEL // n_slices
        acc = [None, None, None]

        def matmul_slice(c):
            yg = ys_ref[:, c * ka:(c + 1) * ka]
            yf = yg.astype(F32)
            pq = jnp.sum(yf * yf, axis=-1, keepdims=True)
            pa = _dot(yg * ng_ref[:, c * ka:(c + 1) * ka].astype(BF16), ws_ref[c * ka:(c + 1) * ka, :])
            pb = _dot(use_scr[:, c * kb:(c + 1) * kb], wa_ref[c * kb:(c + 1) * kb, :])
            acc[:] = (pa, pb, pq) if acc[0] is None else (acc[0] + pa, acc[1] + pb, acc[2] + pq)

        for g in range(ATT_KV_HEADS):
            _attn_kv_head(g, q_ref, kp_ref, kc_ref, kn_ref, vp_ref, vc_ref, vn_ref, bias_ref, sink_ref, store)
            matmul_slice(g)
        a, b, ssq = acc
        scale = lax.rsqrt(ssq * (1.0 / D_INNER) + EPS)
        a = jax.nn.sigmoid(ga_ref[...].astype(F32)) * (scale * a)
        b = jax.nn.sigmoid(gb_ref[...].astype(F32)) * b
        o_ref[...] = (a + b).astype(BF16)

    @pl.when(i % 2 == 0)
    def _():
        step(ya_even, ya_odd)

    @pl.when(i % 2 == 1)
    def _():
        step(ya_odd, ya_even)


def _attn_branch(ys, u, bias, sink, norm_gain, w_ssd, w_att, seq):
    n = ys.shape[0]
    nblk = n // ATT_BLOCK
    ntile = n // BR_TM
    blocks_per_seq = seq // ATT_BLOCK
    kcol = U_K // KV_DIM
    vcol = U_V // KV_DIM
    blk = lambda i, j: jnp.minimum(i * BR_STEPS + j, nblk - 1)
    prev = lambda i, j: jnp.maximum(blk(i, j) - 1, 0)
    nxt = lambda i, j: jnp.minimum(blk(i, j) + 1, nblk - 1)
    tile = lambda i: jnp.maximum(i - 1, 0)
    kv_spec = lambda rowf, c: pl.BlockSpec((ATT_BLOCK, KV_DIM), lambda i, j: (rowf(i, j), c))

    def edge_variant(i, j):
        b = blk(i, j)
        first = (b % blocks_per_seq == 0).astype(jnp.int32)
        last = ((b + 1) % blocks_per_seq == 0).astype(jnp.int32)
        return first + 2 * last

    return pl.pallas_call(
        _attn_branch_kernel,
        grid=(ntile + 1, BR_STEPS),
        in_specs=[
            pl.BlockSpec((ATT_BLOCK, D_MODEL), lambda i, j: (blk(i, j), U_Q // D_MODEL)),
            kv_spec(prev, kcol), kv_spec(blk, kcol), kv_spec(nxt, kcol),
            kv_spec(prev, vcol), kv_spec(blk, vcol), kv_spec(nxt, vcol),
            pl.BlockSpec((1, ATT_Q_HEADS, ATT_BLOCK, 3 * ATT_BLOCK), lambda i, j: (edge_variant(i, j), 0, 0, 0)),
            pl.BlockSpec(memory_space=pltpu.SMEM),
            pl.BlockSpec((BR_TM, D_INNER), lambda i, j: (tile(i), 0)),
            pl.BlockSpec((BR_TM, BR_TN), lambda i, j: (tile(i), U_GA // BR_TN + j)),
            pl.BlockSpec((BR_TM, BR_TN), lambda i, j: (tile(i), U_GB // BR_TN + j)),
            pl.BlockSpec((1, D_INNER), lambda i, j: (0, 0)),
            pl.BlockSpec((D_INNER, BR_TN), lambda i, j: (0, j)),
            pl.BlockSpec((D_MODEL, BR_TN), lambda i, j: (0, j)),
        ],
        out_specs=pl.BlockSpec((BR_TM, BR_TN), lambda i, j: (i, j)),
        out_shape=jax.ShapeDtypeStruct((n + BR_TM, D_MODEL), BF16),
        scratch_shapes=[pltpu.VMEM((BR_TM, D_MODEL), BF16), pltpu.VMEM((BR_TM, D_MODEL), BF16)],
        compiler_params=_params(("arbitrary", "arbitrary")),
        name="attn_branch",
    )(u, u, u, u, u, u, u, bias, sink, ys, u, u, norm_gain, w_ssd, w_att)


OUT_TM = 512


def _outproj_kernel(m_ref, w_ref, x_ref, o_ref):
    o_ref[...] = x_ref[...] + _dot(m_ref[...], w_ref[...])


def _outproj(merged, w_out, x1):
    assert OUT_TM == BR_TM
    n = x1.shape[0]
    return pl.pallas_call(
        _outproj_kernel,
        grid=(n // OUT_TM,),
        in_specs=[
            pl.BlockSpec((OUT_TM, D_MODEL), lambda i: (i + 1, 0)),
            pl.BlockSpec((D_MODEL, D_MODEL), lambda i: (0, 0)),
            pl.BlockSpec((OUT_TM, D_MODEL), lambda i: (i, 0)),
        ],
        out_specs=pl.BlockSpec((OUT_TM, D_MODEL), lambda i: (i, 0)),
        out_shape=jax.ShapeDtypeStruct((n, D_MODEL), F32),
        compiler_params=_params(("parallel",)),
        name="out_proj",
    )(merged, w_out, x1)


def _prepare(ffn1_norm, ffn1_w_gate, ffn1_w_up, ffn1_w_down, mix_norm, w_in, conv_w, conv_b,
             ssd_A_log, ssd_dt_bias, ssd_D, ssd_out_norm, q_norm, k_norm, attn_sink, rel_bias,
             w_branch_ssd, w_branch_attn, w_out, ffn2_norm, ffn2_w_gate, ffn2_w_up, ffn2_w_down,
             final_norm):
    l = 0
    w = w_in[l]
    o_z = 0
    o_xbc = o_z + D_INNER
    o_dt = o_xbc + CONV_DIM
    o_q = o_dt + 2 * SSD_HEADS
    o_k = o_q + D_MODEL
    o_v = o_k + KV_DIM
    o_ga = o_v + KV_DIM
    o_gb = o_ga + D_MODEL
    w_u = jnp.concatenate(
        [w[:, o_z:o_dt], w[:, o_q:o_k], w[:, o_ga:o_gb + D_MODEL], w[:, o_k:o_ga]], axis=1).astype(BF16)
    row = lambda v: v.reshape(1, -1).astype(F32)
    return dict(
        ffn1=(row(ffn1_norm[l]), ffn1_w_gate[l].astype(BF16), ffn1_w_up[l].astype(BF16),
              ffn1_w_down[l].astype(BF16), row(mix_norm[l])),
        ffn2=(row(ffn2_norm[l]), ffn2_w_gate[l].astype(BF16), ffn2_w_up[l].astype(BF16),
              ffn2_w_down[l].astype(BF16), row(final_norm[l])),
        w_u=w_u, w_dt=w[:, o_dt:o_q].astype(BF16), dt_bias=row(ssd_dt_bias[l]),
        conv_w=conv_w[l].astype(F32), conv_b=row(conv_b[l]),
        a_log=row(ssd_A_log[l]), d_exp=row(jnp.repeat(ssd_D[l], SSD_HEADDIM)),
        e01=jnp.asarray(_ssd_expand_matrix(), BF16), out_norm=row(ssd_out_norm[l]),
        q_gain=row(q_norm[l]), k_gain=row(k_norm[l]), sink=attn_sink[l].astype(F32),
        bias=_bias_table(rel_bias.astype(F32)),
        w_ssd=w_branch_ssd[l].astype(BF16), w_att=w_branch_attn[l].astype(BF16),
        w_out=w_out[l].astype(BF16),
    )


def _layer(x, p):
    batch, seq, _ = x.shape
    x2d = x.reshape(batch * seq, D_MODEL)
    x1, h = _ffn(x2d, *p["ffn1"], final=False)
    u, dt = _inproj(h, p["w_u"], p["w_dt"], p["dt_bias"], p["q_gain"], p["k_gain"])
    xc = _conv(u, p["conv_w"], p["conv_b"], seq)
    cum, wdec, ecum, rows = _ssd_prep(dt, p["a_log"])
    ys = _ssd(xc, u, cum, wdec, ecum, rows, p["d_exp"], p["e01"], batch, seq)
    merged = _attn_branch(ys, u, p["bias"], p["sink"], p["out_norm"], p["w_ssd"], p["w_att"], seq)
    x2 = _outproj(merged, p["w_out"], x1)
    (y,) = _ffn(x2, *p["ffn2"], final=True)
    return y.reshape(batch, seq, D_MODEL)


def kernel(x_prompt, x_sample, ffn1_norm, ffn1_w_gate, ffn1_w_up, ffn1_w_down, mix_norm, w_in, conv_w, conv_b, ssd_A_log, ssd_dt_bias, ssd_D, ssd_out_norm, q_norm, k_norm, attn_sink, rel_bias, w_branch_ssd, w_branch_attn, w_out, ffn2_norm, ffn2_w_gate, ffn2_w_up, ffn2_w_down, final_norm):
    p = _prepare(ffn1_norm, ffn1_w_gate, ffn1_w_up, ffn1_w_down, mix_norm, w_in, conv_w, conv_b,
                 ssd_A_log, ssd_dt_bias, ssd_D, ssd_out_norm, q_norm, k_norm, attn_sink, rel_bias,
                 w_branch_ssd, w_branch_attn, w_out, ffn2_norm, ffn2_w_gate, ffn2_w_up, ffn2_w_down,
                 final_norm)
    return (_layer(x_prompt, p), _layer(x_sample, p))
```

```python
import functools
import math

import jax
import jax.numpy as jnp
import numpy as np
from jax import lax
from jax.experimental import pallas as pl
from jax.experimental.pallas import tpu as pltpu

D_MODEL = 2048
D_INNER = 2 * D_MODEL
SSD_HEADDIM = 64
SSD_HEADS = D_INNER // SSD_HEADDIM
SSD_GROUPS = 8
SSD_HPG = SSD_HEADS // SSD_GROUPS
D_STATE = 128
CONV_K = 5
CHUNK = 128
CONV_DIM = D_INNER + 2 * SSD_GROUPS * D_STATE
ATT_HEAD_DIM = 128
ATT_Q_HEADS = D_MODEL // ATT_HEAD_DIM
ATT_KV_HEADS = 4
ATT_REP = ATT_Q_HEADS // ATT_KV_HEADS
WINDOW = 128
ATT_BLOCK = 128
N_BUCKETS = 32
MAX_DIST = 128
D_FF = ((8 * D_MODEL // 3 + 255) // 256) * 256
EPS = 1e-6
GROUP_X = SSD_HPG * SSD_HEADDIM
KV_DIM = ATT_KV_HEADS * ATT_HEAD_DIM

U_Z = 0
U_XBC = U_Z + D_INNER
U_Q = U_XBC + CONV_DIM
U_GA = U_Q + D_MODEL
U_GB = U_GA + D_MODEL
U_K = U_GB + D_MODEL
U_V = U_K + KV_DIM
U_TOTAL = U_V + KV_DIM

V7X_VMEM_LIMIT_BYTES = 56 * 1024 * 1024
NEG_BIG = -1e30

BF16 = jnp.bfloat16
F32 = jnp.float32


def _params(semantics):
    return pltpu.CompilerParams(dimension_semantics=semantics,
                                vmem_limit_bytes=V7X_VMEM_LIMIT_BYTES)


def _rms(x, gain):
    return x * lax.rsqrt(jnp.mean(x * x, axis=-1, keepdims=True) + EPS) * gain


def _silu(x):
    return x * jax.nn.sigmoid(x)


def _dot(a, b):
    return jnp.dot(a, b, preferred_element_type=F32)


FFN_TM = 512
FFN_TF = 512


def _ffn_kernel(x_ref, g_ref, wgu_ref, wd_ref, pg_ref, *refs, final):
    if final:
        y_ref, h_scr, acc_scr = refs
    else:
        x1_ref, h_ref, h_scr, acc_scr = refs
    j = pl.program_id(1)

    @pl.when(j == 0)
    def _():
        h_scr[...] = _rms(x_ref[...], g_ref[...]).astype(BF16)
        acc_scr[...] = jnp.zeros_like(acc_scr)

    h = h_scr[...]
    t = _silu(_dot(h, wgu_ref[0])) * _dot(h, wgu_ref[1])
    acc_scr[...] += _dot(t.astype(BF16), wd_ref[...])

    @pl.when(j == pl.num_programs(1) - 1)
    def _():
        x1 = x_ref[...] + 0.5 * acc_scr[...]
        normed = _rms(x1, pg_ref[...])
        if final:
            y_ref[...] = normed
        else:
            x1_ref[...] = x1
            h_ref[...] = normed.astype(BF16)


def _ffn(x, gain, wgu, wd, post_gain, final):
    n = x.shape[0]
    row = lambda i, j: (i, 0)
    fixed = lambda i, j: (0, 0)
    out_shape = [jax.ShapeDtypeStruct((n, D_MODEL), F32)]
    out_specs = [pl.BlockSpec((FFN_TM, D_MODEL), row)]
    if not final:
        out_shape.append(jax.ShapeDtypeStruct((n, D_MODEL), BF16))
        out_specs.append(pl.BlockSpec((FFN_TM, D_MODEL), row))
    return pl.pallas_call(
        functools.partial(_ffn_kernel, final=final),
        grid=(n // FFN_TM, D_FF // FFN_TF),
        in_specs=[
            pl.BlockSpec((FFN_TM, D_MODEL), row),
            pl.BlockSpec((1, D_MODEL), fixed),
            pl.BlockSpec((2, D_MODEL, FFN_TF), lambda i, j: (0, 0, j)),
            pl.BlockSpec((FFN_TF, D_MODEL), lambda i, j: (j, 0)),
            pl.BlockSpec((1, D_MODEL), fixed),
        ],
        out_specs=out_specs,
        out_shape=out_shape,
        scratch_shapes=[pltpu.VMEM((FFN_TM, D_MODEL), BF16), pltpu.VMEM((FFN_TM, D_MODEL), F32)],
        compiler_params=_params(("parallel", "arbitrary")),
        name="ffn_final" if final else "ffn",
    )(x, gain, wgu, wd, post_gain)


INP_TM = 2048
INP_TN = 1024


def _inproj_kernel(h_ref, w_ref, wdt_ref, bdt_ref, qg_ref, kg_ref, u_ref, dt_ref):
    h = h_ref[...]
    acc = _dot(h, w_ref[...])
    j = pl.program_id(1)
    hd = ATT_HEAD_DIM
    heads_per_tile = INP_TN // hd
    is_q = (j >= U_Q // INP_TN) & (j < U_GA // INP_TN)
    is_kv = j == U_K // INP_TN

    u_ref[...] = acc.astype(BF16)

    def norm_heads(gain, n_normed):
        for hh in range(n_normed):
            cols = slice(hh * hd, (hh + 1) * hd)
            u_ref[:, cols] = _rms(u_ref[:, cols].astype(F32), gain).astype(BF16)

    @pl.when(is_q)
    def _():
        norm_heads(qg_ref[...] * (hd ** -0.5 * LOG2E), heads_per_tile)

    @pl.when(is_kv)
    def _():
        norm_heads(kg_ref[...], ATT_KV_HEADS)

    @pl.when(j == 0)
    def _():
        raw = _dot(h, wdt_ref[...]) + bdt_ref[...]
        dt_ref[...] = jnp.maximum(raw, 0.0) + jnp.log1p(jnp.exp(-jnp.abs(raw)))


def _inproj(h, w_u, w_dt, dt_bias, q_gain, k_gain):
    assert U_Q % INP_TN == 0 and U_GA % INP_TN == 0 and U_K % INP_TN == 0 and KV_DIM * 2 == INP_TN
    n = h.shape[0]
    ndt = 2 * SSD_HEADS
    return pl.pallas_call(
        _inproj_kernel,
        grid=(n // INP_TM, U_TOTAL // INP_TN),
        in_specs=[
            pl.BlockSpec((INP_TM, D_MODEL), lambda i, j: (i, 0)),
            pl.BlockSpec((D_MODEL, INP_TN), lambda i, j: (0, j)),
            pl.BlockSpec((D_MODEL, ndt), lambda i, j: (0, 0)),
            pl.BlockSpec((1, ndt), lambda i, j: (0, 0)),
            pl.BlockSpec((1, ATT_HEAD_DIM), lambda i, j: (0, 0)),
            pl.BlockSpec((1, ATT_HEAD_DIM), lambda i, j: (0, 0)),
        ],
        out_specs=[
            pl.BlockSpec((INP_TM, INP_TN), lambda i, j: (i, j)),
            pl.BlockSpec((INP_TM, ndt), lambda i, j: (i, 0)),
        ],
        out_shape=[jax.ShapeDtypeStruct((n, U_TOTAL), BF16), jax.ShapeDtypeStruct((n, ndt), F32)],
        compiler_params=_params(("parallel", "arbitrary")),
        name="in_proj",
    )(h, w_u, w_dt, dt_bias, q_gain, k_gain)


CONV_TS = 512
CONV_TC = 2048
CONV_HALO = 16
CONV_TB = 32
CONV_WIN = CONV_TB + 2 * CONV_HALO
CONV_SIDE_TAPS = tuple(k for k in range(CONV_K) if k != CONV_K // 2)


def _conv_kernel(xm_ref, xp_ref, xn_ref, w_ref, b_ref, o_ref, xs_scr, pw_scr, *, tiles_per_seq):
    i = pl.program_id(0)
    first = (i % tiles_per_seq) == 0
    last = ((i + 1) % tiles_per_seq) == 0
    H = CONV_HALO
    zero_halo = jnp.zeros((H, CONV_TC), BF16)
    xs_scr[0:H, :] = jnp.where(first, zero_halo, xp_ref[...])
    xs_scr[H:H + CONV_TS, :] = xm_ref[...]
    xs_scr[H + CONV_TS:, :] = jnp.where(last, zero_halo, xn_ref[...])
    xs = xs_scr[...]
    for n, k in enumerate(CONV_SIDE_TAPS):
        pw_scr[n] = xs * (0.5 * w_ref[k:k + 1, :]).astype(BF16)
    r = lax.broadcasted_iota(jnp.int32, (CONV_TB, len(CONV_SIDE_TAPS) * CONV_WIN), 0)
    c = lax.broadcasted_iota(jnp.int32, (CONV_TB, len(CONV_SIDE_TAPS) * CONV_WIN), 1)
    hit = None
    for n, k in enumerate(CONV_SIDE_TAPS):
        sel = c == r + (n * CONV_WIN + H + k - CONV_K // 2)
        hit = sel if hit is None else hit | sel
    shift = jnp.where(hit, 1.0, 0.0).astype(BF16)
    w_mid = 0.5 * w_ref[CONV_K // 2:CONV_K // 2 + 1, :]
    bias = 0.5 * b_ref[...]
    for blk in range(CONV_TS // CONV_TB):
        t0 = blk * CONV_TB
        taps = jnp.concatenate([pw_scr[n, t0:t0 + CONV_WIN, :] for n in range(len(CONV_SIDE_TAPS))], axis=0)
        mid = xs_scr[H + t0:H + t0 + CONV_TB, :].astype(F32) * w_mid + bias
        half = _dot(shift, taps) + mid
        o_ref[t0:t0 + CONV_TB, :] = (half + half * jnp.tanh(half)).astype(BF16)


def _conv(u, conv_w, conv_b, seq):
    n = u.shape[0]
    tiles_per_seq = seq // CONV_TS
    halo_per_tile = CONV_TS // CONV_HALO
    n_halo = n // CONV_HALO
    col0 = U_XBC // CONV_TC
    return pl.pallas_call(
        functools.partial(_conv_kernel, tiles_per_seq=tiles_per_seq),
        grid=(n // CONV_TS, CONV_DIM // CONV_TC),
        in_specs=[
            pl.BlockSpec((CONV_TS, CONV_TC), lambda i, j: (i, col0 + j)),
            pl.BlockSpec((CONV_HALO, CONV_TC),
                         lambda i, j: (jnp.maximum(i * halo_per_tile - 1, 0), col0 + j)),
            pl.BlockSpec((CONV_HALO, CONV_TC),
                         lambda i, j: (jnp.minimum((i + 1) * halo_per_tile, n_halo - 1), col0 + j)),
            pl.BlockSpec((CONV_K, CONV_TC), lambda i, j: (0, j)),
            pl.BlockSpec((1, CONV_TC), lambda i, j: (0, j)),
        ],
        out_specs=pl.BlockSpec((CONV_TS, CONV_TC), lambda i, j: (i, j)),
        out_shape=jax.ShapeDtypeStruct((n, CONV_DIM), BF16),
        scratch_shapes=[pltpu.VMEM((CONV_TS + 2 * CONV_HALO, CONV_TC), BF16),
                        pltpu.VMEM((len(CONV_SIDE_TAPS), CONV_TS + 2 * CONV_HALO, CONV_TC), BF16)],
        compiler_params=_params(("parallel", "parallel")),
        name="conv",
    )(u, u, u, conv_w, conv_b)


def _split3(a):
    hi = a.astype(BF16)
    r1 = a - hi.astype(F32)
    mid = r1.astype(BF16)
    lo = (r1 - mid.astype(F32)).astype(BF16)
    return hi, mid, lo


def _exact_lhs01(m01, a):
    hi, mid, lo = _split3(a)
    return _dot(jnp.concatenate([m01, m01, m01], axis=1), jnp.concatenate([hi, mid, lo], axis=0))


def _split2(a):
    hi = a.astype(BF16)
    mid = (a - hi.astype(F32)).astype(BF16)
    return jnp.concatenate([hi, mid], axis=1)


PREP_CHUNKS = 8
ROWS_PER_GROUP = 4 * SSD_HPG
LOG2E = 1.4426950408889634


def _ssd_prep_kernel(dt_ref, a_ref, cum_ref, wdec_ref, ecum_ref, rows_ref):
    L = CHUNK
    H = SSD_HPG
    row = lax.broadcasted_iota(jnp.int32, (L, L), 0)
    col = lax.broadcasted_iota(jnp.int32, (L, L), 1)
    lower = (row >= col).astype(BF16)
    upper = (row <= col).astype(BF16)
    is_fwd = col < SSD_HEADS
    a = -jnp.exp(a_ref[...])
    for t in range(PREP_CHUNKS):
        sl = slice(t * L, (t + 1) * L)
        dt = dt_ref[sl, :]
        dA = dt * a
        pre = _exact_lhs01(lower, dA)
        suf = _exact_lhs01(upper, dA)
        cum = jnp.where(is_fwd, pre, suf)
        tot = jnp.where(is_fwd[:1], pre[L - 1:L, :], suf[0:1, :])
        cum_ref[sl, :] = cum * LOG2E
        wdec_ref[sl, :] = _split2(jnp.exp(tot - cum) * dt)
        ecum_ref[sl, :] = _split2(jnp.exp(cum))
        dt_t = dt.T
        src_t = (cum.T - jnp.log(dt_t)) * LOG2E
        lds_t = jnp.log2(dt_t[:SSD_HEADS] + dt_t[SSD_HEADS:])
        etot = jnp.broadcast_to(jnp.exp(tot), (H, L))
        for g in range(SSD_GROUPS):
            f = slice(g * H, (g + 1) * H)
            b = slice(SSD_HEADS + g * H, SSD_HEADS + (g + 1) * H)
            rows_ref[t, g] = jnp.concatenate([src_t[f], src_t[b], lds_t[f], etot], axis=0)


def _ssd_prep(dt, a_log):
    n = dt.shape[0]
    rows_blk = PREP_CHUNKS * CHUNK
    nh = 2 * SSD_HEADS
    return pl.pallas_call(
        _ssd_prep_kernel,
        grid=(n // rows_blk,),
        in_specs=[pl.BlockSpec((rows_blk, nh), lambda i: (i, 0)), pl.BlockSpec((1, nh), lambda i: (0, 0))],
        out_specs=[pl.BlockSpec((rows_blk, nh), lambda i: (i, 0)),
                   pl.BlockSpec((rows_blk, 2 * nh), lambda i: (i, 0)),
                   pl.BlockSpec((rows_blk, 2 * nh), lambda i: (i, 0)),
                   pl.BlockSpec((PREP_CHUNKS, SSD_GROUPS, ROWS_PER_GROUP, CHUNK), lambda i: (i, 0, 0, 0))],
        out_shape=[jax.ShapeDtypeStruct((n, nh), F32),
                   jax.ShapeDtypeStruct((n, 2 * nh), BF16),
                   jax.ShapeDtypeStruct((n, 2 * nh), BF16),
                   jax.ShapeDtypeStruct((n // CHUNK, SSD_GROUPS, ROWS_PER_GROUP, CHUNK), F32)],
        compiler_params=_params(("parallel",)),
        name="ssd_prep",
    )(dt, a_log)


SSD_T = 16


def _ssd_kernel(x_ref, b_ref, c_ref, z_ref, cum_ref, wdec_ref, ecum_ref, rows_ref, d_ref, e_ref,
                y_ref, gst_scr, h_scr):
    g = pl.program_id(1)
    phase = pl.program_id(2)
    step = pl.program_id(3)
    ns = pl.num_programs(3)
    L = CHUNK
    GX = GROUP_X
    H = SSD_HPG

    @pl.when(step == 0)
    def _():
        h_scr[...] = jnp.zeros_like(h_scr)

    def state_update(t, direction, xb, bm):
        sl = slice(t * L, (t + 1) * L)
        e_dir = e_ref[0, :, direction * GX:(direction + 1) * GX]
        w = _dot(wdec_ref[sl, :], e_dir)
        dec = _dot(_split2(rows_ref[t, 0, 3 * H:4 * H, :]), e_dir)[0:1, :]
        xdec = w.astype(BF16) * xb
        b_t = bm.astype(F32).T.astype(BF16)
        h_scr[...] = dec * h_scr[...] + _dot(b_t, xdec)

    @pl.when(phase == 0)
    def _():
        for t in reversed(range(SSD_T)):
            sl = slice(t * L, (t + 1) * L)
            gst_scr[(ns - 1 - step) * SSD_T + t] = h_scr[...].astype(BF16)
            state_update(t, 1, x_ref[sl, :], b_ref[sl, :])

    @pl.when(phase == 1)
    def _():
        row = lax.broadcasted_iota(jnp.int32, (L, L), 0)
        col = lax.broadcasted_iota(jnp.int32, (L, L), 1)
        lower = row >= col
        diag = row == col
        low_lane = col < SSD_HEADDIM
        shift = (2 * SSD_HEADS - H * g) % (2 * SSD_HEADS)
        for t in range(SSD_T):
            sl = slice(t * L, (t + 1) * L)
            cm = c_ref[sl, :]
            bm = b_ref[sl, :]
            xb = x_ref[sl, :]
            xf = xb.astype(F32)
            cb = lax.dot_general(cm, bm, (((1,), (1,)), ((), ())), preferred_element_type=F32)
            cum_r = pltpu.roll(cum_ref[sl, :], shift, axis=1)
            pairs = []
            for k in range(H // 2):
                ms = []
                for half in range(2):
                    hh = 2 * k + half
                    fwd = cum_r[:, hh:hh + 1] - rows_ref[t, 0, hh:hh + 1, :]
                    bwd = cum_r[:, SSD_HEADS + hh:SSD_HEADS + hh + 1] - rows_ref[t, 0, H + hh:H + hh + 1, :]
                    arg = jnp.where(diag, rows_ref[t, 0, 2 * H + hh:2 * H + hh + 1, :],
                                    jnp.where(lower, fwd, bwd))
                    ms.append((jnp.exp2(arg) * cb).astype(BF16))
                xp = xb[:, k * L:(k + 1) * L]
                zero = jnp.zeros_like(xp)
                rhs = jnp.concatenate([jnp.where(low_lane, xp, zero), jnp.where(low_lane, zero, xp)], axis=0)
                pairs.append(_dot(jnp.concatenate(ms, axis=1), rhs))
            y = jnp.concatenate(pairs, axis=1)
            c = step * SSD_T + t
            states = jnp.concatenate([h_scr[...].astype(BF16), gst_scr[c]], axis=1)
            y_off = _dot(ecum_ref[sl, :], e_ref[0]) * _dot(cm, states)
            y = y + y_off[:, :GX] + y_off[:, GX:] + d_ref[...] * xf
            y_ref[sl, :] = (y * _silu(z_ref[sl, :].astype(F32))).astype(BF16)
            state_update(t, 0, xb, bm)


def _ssd(xc, u, cum, wdec, ecum, rows, d_exp, e01, batch, seq):
    n = xc.shape[0]
    nc = seq // CHUNK
    ns = nc // SSD_T
    tl = SSD_T * CHUNK
    nh = 2 * SSD_HEADS
    gx_blocks = D_INNER // D_STATE

    def fb_row(b, p, s):
        return b * ns + jnp.where(p == 0, ns - 1 - s, s)

    def f_row(b, p, s):
        return b * ns + jnp.where(p == 0, 0, s)

    return pl.pallas_call(
        _ssd_kernel,
        grid=(batch, SSD_GROUPS, 2, ns),
        in_specs=[
            pl.BlockSpec((tl, GROUP_X), lambda b, g, p, s: (fb_row(b, p, s), g)),
            pl.BlockSpec((tl, D_STATE), lambda b, g, p, s: (fb_row(b, p, s), gx_blocks + g)),
            pl.BlockSpec((tl, D_STATE), lambda b, g, p, s: (f_row(b, p, s), gx_blocks + SSD_GROUPS + g)),
            pl.BlockSpec((tl, GROUP_X), lambda b, g, p, s: (f_row(b, p, s), U_Z // GROUP_X + g)),
            pl.BlockSpec((tl, nh), lambda b, g, p, s: (f_row(b, p, s), 0)),
            pl.BlockSpec((tl, 2 * nh), lambda b, g, p, s: (fb_row(b, p, s), 0)),
            pl.BlockSpec((tl, 2 * nh), lambda b, g, p, s: (f_row(b, p, s), 0)),
            pl.BlockSpec((SSD_T, 1, ROWS_PER_GROUP, CHUNK), lambda b, g, p, s: (fb_row(b, p, s), g, 0, 0)),
            pl.BlockSpec((1, GROUP_X), lambda b, g, p, s: (0, g)),
            pl.BlockSpec((1, 2 * nh, 2 * GROUP_X), lambda b, g, p, s: (g, 0, 0)),
        ],
        out_specs=pl.BlockSpec((tl, GROUP_X), lambda b, g, p, s: (f_row(b, p, s), g)),
        out_shape=jax.ShapeDtypeStruct((n, D_INNER), BF16),
        scratch_shapes=[
            pltpu.VMEM((nc, D_STATE, GROUP_X), BF16),
            pltpu.VMEM((D_STATE, GROUP_X), F32),
        ],
        compiler_params=_params(("arbitrary", "arbitrary", "arbitrary", "arbitrary")),
        name="ssd",
    )(xc, xc, xc, u, cum, wdec, ecum, rows, d_exp, e01)


def _ssd_expand_matrix():
    e = np.zeros((SSD_GROUPS, 2 * SSD_HEADS, 2 * GROUP_X), np.float32)
    for g in range(SSD_GROUPS):
        for d in range(2):
            for r in range(SSD_HPG):
                lo = d * GROUP_X + r * SSD_HEADDIM
                e[g, d * SSD_HEADS + g * SSD_HPG + r, lo:lo + SSD_HEADDIM] = 1.0
    return np.concatenate([e, e], axis=1)


def _t5_bucket(rel):
    nb = N_BUCKETS // 2
    ret = (rel > 0).astype(np.int32) * nb
    n = np.abs(rel)
    max_exact = nb // 2
    large = max_exact + (np.log(np.maximum(n, 1) / max_exact) / math.log(MAX_DIST / max_exact)
                         * (nb - max_exact)).astype(np.int32)
    large = np.minimum(large, nb - 1)
    return (ret + np.where(n < max_exact, n, large)).astype(np.int32)


def _bucket_table():
    i = np.arange(ATT_BLOCK)[:, None]
    j = np.arange(3 * ATT_BLOCK)[None, :]
    rel = j - ATT_BLOCK - i
    return np.where(np.abs(rel) <= WINDOW, _t5_bucket(rel), -1).astype(np.int32)


N_EDGE_VARIANTS = 4


def _bias_kernel(rel_ref, bucket_ref, o_ref):
    variant = pl.program_id(0)
    bucket = bucket_ref[...]
    kcol = lax.broadcasted_iota(jnp.int32, bucket.shape, 1)
    dead = (((variant & 1) == 1) & (kcol < ATT_BLOCK)) | (((variant & 2) == 2) & (kcol >= 2 * ATT_BLOCK))
    for h in range(ATT_Q_HEADS):
        acc = jnp.full(bucket.shape, NEG_BIG, F32)
        for b in range(N_BUCKETS):
            acc = jnp.where(bucket == b, rel_ref[b, h] * LOG2E, acc)
        o_ref[0, h] = jnp.where(dead, NEG_BIG, acc)


def _bias_table(rel_bias):
    bucket = jnp.asarray(_bucket_table())
    return pl.pallas_call(
        _bias_kernel,
        grid=(N_EDGE_VARIANTS,),
        in_specs=[
            pl.BlockSpec(memory_space=pltpu.SMEM),
            pl.BlockSpec((ATT_BLOCK, 3 * ATT_BLOCK), lambda v: (0, 0)),
        ],
        out_specs=pl.BlockSpec((1, ATT_Q_HEADS, ATT_BLOCK, 3 * ATT_BLOCK), lambda v: (v, 0, 0, 0)),
        out_shape=jax.ShapeDtypeStruct((N_EDGE_VARIANTS, ATT_Q_HEADS, ATT_BLOCK, 3 * ATT_BLOCK), F32),
        compiler_params=_params(("arbitrary",)),
        name="rel_bias_table",
    )(rel_bias, bucket)


def _attn_kv_head(g, q_ref, kp_ref, kc_ref, kn_ref, vp_ref, vc_ref, vn_ref, bias_ref, sink_ref, store):
    T = ATT_BLOCK
    hd = ATT_HEAD_DIM
    ones = jnp.ones((3 * T, hd), BF16)
    ks = slice(g * hd, (g + 1) * hd)
    kwin = jnp.concatenate([kp_ref[:, ks], kc_ref[:, ks], kn_ref[:, ks]], axis=0)
    vwin = jnp.concatenate([vp_ref[:, ks], vc_ref[:, ks], vn_ref[:, ks]], axis=0)
    qs = jnp.concatenate(
        [q_ref[:, (g * ATT_REP + r) * hd:(g * ATT_REP + r + 1) * hd] for r in range(ATT_REP)],
        axis=0)
    s = lax.dot_general(qs, kwin, (((1,), (1,)), ((), ())), preferred_element_type=F32)
    s = s + jnp.concatenate([bias_ref[0, g * ATT_REP + r] for r in range(ATT_REP)], axis=0)
    sink = jnp.concatenate(
        [jnp.full((T, 1), sink_ref[g * ATT_REP + r] * LOG2E, F32) for r in range(ATT_REP)], axis=0)
    m = jnp.maximum(jnp.max(s, axis=-1, keepdims=True), sink)
    p = jnp.exp2(s - m).astype(BF16)
    ov = _dot(p, jnp.concatenate([vwin, ones], axis=1))
    o = ov[:, :hd] / (ov[:, hd:] + jnp.exp2(sink - m))
    for r in range(ATT_REP):
        store(g * ATT_REP + r, o[r * T:(r + 1) * T, :].astype(BF16))


BR_TM = 512
BR_TN = 512
BR_STEPS = D_MODEL // BR_TN
assert BR_TM == BR_STEPS * ATT_BLOCK


def _attn_branch_kernel(q_ref, kp_ref, kc_ref, kn_ref, vp_ref, vc_ref, vn_ref, bias_ref, sink_ref,
                        ys_ref, ga_ref, gb_ref, ng_ref, ws_ref, wa_ref, o_ref, ya_even, ya_odd):
    i = pl.program_id(0)
    j = pl.program_id(1)
    hd = ATT_HEAD_DIM

    @pl.when((i == 0) & (j == 0))
    def _():
        ya_odd[...] = jnp.zeros_like(ya_odd)

    def step(fill_scr, use_scr):
        row0 = pl.multiple_of(j * ATT_BLOCK, ATT_BLOCK)

        def store(head, val):
            fill_scr[pl.ds(row0, ATT_BLOCK), head * hd:(head + 1) * hd] = val

        n_slices = ATT_KV_HEADS
        ka = D_INNER // n_slices
        kb = D_MODEL // n_slices
        acc = [None, None, None]

        def matmul_slice(c):
            yg = ys_ref[:, c * ka:(c + 1) * ka]
            yf = yg.astype(F32)
            pq = jnp.sum(yf * yf, axis=-1, keepdims=True)
            pa = _dot(yg * ng_ref[:, c * ka:(c + 1) * ka].astype(BF16), ws_ref[c * ka:(c + 1) * ka, :])
            pb = _dot(use_scr[:, c * kb:(c + 1) * kb], wa_ref[c * kb:(c + 1) * kb, :])
            acc[:] = (pa, pb, pq) if acc[0] is None else (acc[0] + pa, acc[1] + pb, acc[2] + pq)

        for g in range(ATT_KV_HEADS):
            _attn_kv_head(g, q_ref, kp_ref, kc_ref, kn_ref, vp_ref, vc_ref, vn_ref, bias_ref, sink_ref, store)
            matmul_slice(g)
        a, b, ssq = acc
        scale = lax.rsqrt(ssq * (1.0 / D_INNER) + EPS)
        a = jax.nn.sigmoid(ga_ref[...].astype(F32)) * (scale * a)
        b = jax.nn.sigmoid(gb_ref[...].astype(F32)) * b
        o_ref[...] = (a + b).astype(BF16)

    @pl.when(i % 2 == 0)
    def _():
        step(ya_even, ya_odd)

    @pl.when(i % 2 == 1)
    def _():
        step(ya_odd, ya_even)


def _attn_branch(ys, u, bias, sink, norm_gain, w_ssd, w_att, seq):
    n = ys.shape[0]
    nblk = n // ATT_BLOCK
    ntile = n // BR_TM
    blocks_per_seq = seq // ATT_BLOCK
    kcol = U_K // KV_DIM
    vcol = U_V // KV_DIM
    blk = lambda i, j: jnp.minimum(i * BR_STEPS + j, nblk - 1)
    prev = lambda i, j: jnp.maximum(blk(i, j) - 1, 0)
    nxt = lambda i, j: jnp.minimum(blk(i, j) + 1, nblk - 1)
    tile = lambda i: jnp.maximum(i - 1, 0)
    kv_spec = lambda rowf, c: pl.BlockSpec((ATT_BLOCK, KV_DIM), lambda i, j: (rowf(i, j), c))

    def edge_variant(i, j):
        b = blk(i, j)
        first = (b % blocks_per_seq == 0).astype(jnp.int32)
        last = ((b + 1) % blocks_per_seq == 0).astype(jnp.int32)
        return first + 2 * last

    return pl.pallas_call(
        _attn_branch_kernel,
        grid=(ntile + 1, BR_STEPS),
        in_specs=[
            pl.BlockSpec((ATT_BLOCK, D_MODEL), lambda i, j: (blk(i, j), U_Q // D_MODEL)),
            kv_spec(prev, kcol), kv_spec(blk, kcol), kv_spec(nxt, kcol),
            kv_spec(prev, vcol), kv_spec(blk, vcol), kv_spec(nxt, vcol),
            pl.BlockSpec((1, ATT_Q_HEADS, ATT_BLOCK, 3 * ATT_BLOCK), lambda i, j: (edge_variant(i, j), 0, 0, 0)),
            pl.BlockSpec(memory_space=pltpu.SMEM),
            pl.BlockSpec((BR_TM, D_INNER), lambda i, j: (tile(i), 0)),
            pl.BlockSpec((BR_TM, BR_TN), lambda i, j: (tile(i), U_GA // BR_TN + j)),
            pl.BlockSpec((BR_TM, BR_TN), lambda i, j: (tile(i), U_GB // BR_TN + j)),
            pl.BlockSpec((1, D_INNER), lambda i, j: (0, 0)),
            pl.BlockSpec((D_INNER, BR_TN), lambda i, j: (0, j)),
            pl.BlockSpec((D_MODEL, BR_TN), lambda i, j: (0, j)),
        ],
        out_specs=pl.BlockSpec((BR_TM, BR_TN), lambda i, j: (i, j)),
        out_shape=jax.ShapeDtypeStruct((n + BR_TM, D_MODEL), BF16),
        scratch_shapes=[pltpu.VMEM((BR_TM, D_MODEL), BF16), pltpu.VMEM((BR_TM, D_MODEL), BF16)],
        compiler_params=_params(("arbitrary", "arbitrary")),
        name="attn_branch",
    )(u, u, u, u, u, u, u, bias, sink, ys, u, u, norm_gain, w_ssd, w_att)


OUT_TM = 512


def _outproj_kernel(m_ref, w_ref, x_ref, o_ref):
    o_ref[...] = x_ref[...] + _dot(m_ref[...], w_ref[...])


def _outproj(merged, w_out, x1):
    assert OUT_TM == BR_TM
    n = x1.shape[0]
    return pl.pallas_call(
        _outproj_kernel,
        grid=(n // OUT_TM,),
        in_specs=[
            pl.BlockSpec((OUT_TM, D_MODEL), lambda i: (i + 1, 0)),
            pl.BlockSpec((D_MODEL, D_MODEL), lambda i: (0, 0)),
            pl.BlockSpec((OUT_TM, D_MODEL), lambda i: (i, 0)),
        ],
        out_specs=pl.BlockSpec((OUT_TM, D_MODEL), lambda i: (i, 0)),
        out_shape=jax.ShapeDtypeStruct((n, D_MODEL), F32),
        compiler_params=_params(("parallel",)),
        name="out_proj",
    )(merged, w_out, x1)


def _prepare(ffn1_norm, ffn1_w_gate, ffn1_w_up, ffn1_w_down, mix_norm, w_in, conv_w, conv_b,
             ssd_A_log, ssd_dt_bias, ssd_D, ssd_out_norm, q_norm, k_norm, attn_sink, rel_bias,
             w_branch_ssd, w_branch_attn, w_out, ffn2_norm, ffn2_w_gate, ffn2_w_up, ffn2_w_down,
             final_norm):
    l = 0
    w = w_in[l]
    o_z = 0
    o_xbc = o_z + D_INNER
    o_dt = o_xbc + CONV_DIM
    o_q = o_dt + 2 * SSD_HEADS
    o_k = o_q + D_MODEL
    o_v = o_k + KV_DIM
    o_ga = o_v + KV_DIM
    o_gb = o_ga + D_MODEL
    w_u = jnp.concatenate(
        [w[:, o_z:o_dt], w[:, o_q:o_k], w[:, o_ga:o_gb + D_MODEL], w[:, o_k:o_ga]], axis=1).astype(BF16)
    row = lambda v: v.reshape(1, -1).astype(F32)
    return dict(
        ffn1=(row(ffn1_norm[l]), jnp.stack([ffn1_w_gate[l], ffn1_w_up[l]]).astype(BF16),
              ffn1_w_down[l].astype(BF16), row(mix_norm[l])),
        ffn2=(row(ffn2_norm[l]), jnp.stack([ffn2_w_gate[l], ffn2_w_up[l]]).astype(BF16),
              ffn2_w_down[l].astype(BF16), row(final_norm[l])),
        w_u=w_u, w_dt=w[:, o_dt:o_q].astype(BF16), dt_bias=row(ssd_dt_bias[l]),
        conv_w=conv_w[l].astype(F32), conv_b=row(conv_b[l]),
        a_log=row(ssd_A_log[l]), d_exp=row(jnp.repeat(ssd_D[l], SSD_HEADDIM)),
        e01=jnp.asarray(_ssd_expand_matrix(), BF16), out_norm=row(ssd_out_norm[l]),
        q_gain=row(q_norm[l]), k_gain=row(k_norm[l]), sink=attn_sink[l].astype(F32),
        bias=_bias_table(rel_bias.astype(F32)),
        w_ssd=w_branch_ssd[l].astype(BF16), w_att=w_branch_attn[l].astype(BF16),
        w_out=w_out[l].astype(BF16),
    )


def _layer(x, p):
    batch, seq, _ = x.shape
    x2d = x.reshape(batch * seq, D_MODEL)
    x1, h = _ffn(x2d, *p["ffn1"], final=False)
    u, dt = _inproj(h, p["w_u"], p["w_dt"], p["dt_bias"], p["q_gain"], p["k_gain"])
    xc = _conv(u, p["conv_w"], p["conv_b"], seq)
    cum, wdec, ecum, rows = _ssd_prep(dt, p["a_log"])
    ys = _ssd(xc, u, cum, wdec, ecum, rows, p["d_exp"], p["e01"], batch, seq)
    merged = _attn_branch(ys, u, p["bias"], p["sink"], p["out_norm"], p["w_ssd"], p["w_att"], seq)
    x2 = _outproj(merged, p["w_out"], x1)
    (y,) = _ffn(x2, *p["ffn2"], final=True)
    return y.reshape(batch, seq, D_MODEL)


def kernel(x_prompt, x_sample, ffn1_norm, ffn1_w_gate, ffn1_w_up, ffn1_w_down, mix_norm, w_in, conv_w, conv_b, ssd_A_log, ssd_dt_bias, ssd_D, ssd_out_norm, q_norm, k_norm, attn_sink, rel_bias, w_branch_ssd, w_branch_attn, w_out, ffn2_norm, ffn2_w_gate, ffn2_w_up, ffn2_w_down, final_norm):
    p = _prepare(ffn1_norm, ffn1_w_gate, ffn1_w_up, ffn1_w_down, mix_norm, w_in, conv_w, conv_b,
                 ssd_A_log, ssd_dt_bias, ssd_D, ssd_out_norm, q_norm, k_norm, attn_sink, rel_bias,
                 w_branch_ssd, w_branch_attn, w_out, ffn2_norm, ffn2_w_gate, ffn2_w_up, ffn2_w_down,
                 final_norm)
    return (_layer(x_prompt, p), _layer(x_sample, p))
```

```python
import functools
import math

import jax
import jax.numpy as jnp
import numpy as np
from jax import lax
from jax.experimental import pallas as pl
from jax.experimental.pallas import tpu as pltpu

D_MODEL = 2048
D_INNER = 2 * D_MODEL
SSD_HEADDIM = 64
SSD_HEADS = D_INNER // SSD_HEADDIM
SSD_GROUPS = 8
SSD_HPG = SSD_HEADS // SSD_GROUPS
D_STATE = 128
CONV_K = 5
CHUNK = 128
CONV_DIM = D_INNER + 2 * SSD_GROUPS * D_STATE
ATT_HEAD_DIM = 128
ATT_Q_HEADS = D_MODEL // ATT_HEAD_DIM
ATT_KV_HEADS = 4
ATT_REP = ATT_Q_HEADS // ATT_KV_HEADS
WINDOW = 128
ATT_BLOCK = 128
N_BUCKETS = 32
MAX_DIST = 128
D_FF = ((8 * D_MODEL // 3 + 255) // 256) * 256
EPS = 1e-6
GROUP_X = SSD_HPG * SSD_HEADDIM
KV_DIM = ATT_KV_HEADS * ATT_HEAD_DIM

U_Z = 0
U_XBC = U_Z + D_INNER
U_Q = U_XBC + CONV_DIM
U_GA = U_Q + D_MODEL
U_GB = U_GA + D_MODEL
U_K = U_GB + D_MODEL
U_V = U_K + KV_DIM
U_TOTAL = U_V + KV_DIM

V7X_VMEM_LIMIT_BYTES = 56 * 1024 * 1024
NEG_BIG = -1e30

BF16 = jnp.bfloat16
F32 = jnp.float32


def _params(semantics):
    return pltpu.CompilerParams(dimension_semantics=semantics,
                                vmem_limit_bytes=V7X_VMEM_LIMIT_BYTES)


def _rms(x, gain):
    return x * lax.rsqrt(jnp.mean(x * x, axis=-1, keepdims=True) + EPS) * gain


def _silu(x):
    return x * jax.nn.sigmoid(x)


def _dot(a, b):
    return jnp.dot(a, b, preferred_element_type=F32)


FFN_TM = 512
FFN_TF = 512


def _ffn_kernel(x_ref, g_ref, wg_ref, wu_ref, wd_ref, pg_ref, *refs, final):
    if final:
        y_ref, h_scr, acc_scr = refs
    else:
        x1_ref, h_ref, h_scr, acc_scr = refs
    j = pl.program_id(1)

    @pl.when(j == 0)
    def _():
        h_scr[...] = _rms(x_ref[...], g_ref[...]).astype(BF16)
        acc_scr[...] = jnp.zeros_like(acc_scr)

    h = h_scr[...]
    t = _silu(_dot(h, wg_ref[...])) * _dot(h, wu_ref[...])
    acc_scr[...] += _dot(t.astype(BF16), wd_ref[...])

    @pl.when(j == pl.num_programs(1) - 1)
    def _():
        x1 = x_ref[...] + 0.5 * acc_scr[...]
        normed = _rms(x1, pg_ref[...])
        if final:
            y_ref[...] = normed
        else:
            x1_ref[...] = x1
            h_ref[...] = normed.astype(BF16)


def _ffn(x, gain, wg, wu, wd, post_gain, final):
    n = x.shape[0]
    row = lambda i, j: (i, 0)
    fixed = lambda i, j: (0, 0)
    out_shape = [jax.ShapeDtypeStruct((n, D_MODEL), F32)]
    out_specs = [pl.BlockSpec((FFN_TM, D_MODEL), row)]
    if not final:
        out_shape.append(jax.ShapeDtypeStruct((n, D_MODEL), BF16))
        out_specs.append(pl.BlockSpec((FFN_TM, D_MODEL), row))
    return pl.pallas_call(
        functools.partial(_ffn_kernel, final=final),
        grid=(n // FFN_TM, D_FF // FFN_TF),
        in_specs=[
            pl.BlockSpec((FFN_TM, D_MODEL), row),
            pl.BlockSpec((1, D_MODEL), fixed),
            pl.BlockSpec((D_MODEL, FFN_TF), lambda i, j: (0, j)),
            pl.BlockSpec((D_MODEL, FFN_TF), lambda i, j: (0, j)),
            pl.BlockSpec((FFN_TF, D_MODEL), lambda i, j: (j, 0)),
            pl.BlockSpec((1, D_MODEL), fixed),
        ],
        out_specs=out_specs,
        out_shape=out_shape,
        scratch_shapes=[pltpu.VMEM((FFN_TM, D_MODEL), BF16), pltpu.VMEM((FFN_TM, D_MODEL), F32)],
        compiler_params=_params(("parallel", "arbitrary")),
        name="ffn_final" if final else "ffn",
    )(x, gain, wg, wu, wd, post_gain)


INP_TM = 2048
INP_TN = 1024


def _inproj_kernel(h_ref, w_ref, wdt_ref, bdt_ref, qg_ref, kg_ref, u_ref, dt_ref):
    h = h_ref[...]
    acc = _dot(h, w_ref[...])
    j = pl.program_id(1)
    hd = ATT_HEAD_DIM
    heads_per_tile = INP_TN // hd
    is_q = (j >= U_Q // INP_TN) & (j < U_GA // INP_TN)
    is_kv = j == U_K // INP_TN

    u_ref[...] = acc.astype(BF16)

    def norm_heads(gain, n_normed):
        for hh in range(n_normed):
            cols = slice(hh * hd, (hh + 1) * hd)
            u_ref[:, cols] = _rms(u_ref[:, cols].astype(F32), gain).astype(BF16)

    @pl.when(is_q)
    def _():
        norm_heads(qg_ref[...] * (hd ** -0.5 * LOG2E), heads_per_tile)

    @pl.when(is_kv)
    def _():
        norm_heads(kg_ref[...], ATT_KV_HEADS)

    @pl.when(j == 0)
    def _():
        raw = _dot(h, wdt_ref[...]) + bdt_ref[...]
        dt_ref[...] = jnp.maximum(raw, 0.0) + jnp.log1p(jnp.exp(-jnp.abs(raw)))


def _inproj(h, w_u, w_dt, dt_bias, q_gain, k_gain):
    assert U_Q % INP_TN == 0 and U_GA % INP_TN == 0 and U_K % INP_TN == 0 and KV_DIM * 2 == INP_TN
    n = h.shape[0]
    ndt = 2 * SSD_HEADS
    return pl.pallas_call(
        _inproj_kernel,
        grid=(n // INP_TM, U_TOTAL // INP_TN),
        in_specs=[
            pl.BlockSpec((INP_TM, D_MODEL), lambda i, j: (i, 0)),
            pl.BlockSpec((D_MODEL, INP_TN), lambda i, j: (0, j)),
            pl.BlockSpec((D_MODEL, ndt), lambda i, j: (0, 0)),
            pl.BlockSpec((1, ndt), lambda i, j: (0, 0)),
            pl.BlockSpec((1, ATT_HEAD_DIM), lambda i, j: (0, 0)),
            pl.BlockSpec((1, ATT_HEAD_DIM), lambda i, j: (0, 0)),
        ],
        out_specs=[
            pl.BlockSpec((INP_TM, INP_TN), lambda i, j: (i, j)),
            pl.BlockSpec((INP_TM, ndt), lambda i, j: (i, 0)),
        ],
        out_shape=[jax.ShapeDtypeStruct((n, U_TOTAL), BF16), jax.ShapeDtypeStruct((n, ndt), F32)],
        compiler_params=_params(("parallel", "arbitrary")),
        name="in_proj",
    )(h, w_u, w_dt, dt_bias, q_gain, k_gain)


CONV_TS = 1024
CONV_TC = 2048
CONV_HALO = 16
CONV_TB = 32
CONV_WIN = CONV_TB + 2 * CONV_HALO
CONV_SIDE_TAPS = tuple(k for k in range(CONV_K) if k != CONV_K // 2)


def _conv_kernel(xm_ref, xp_ref, xn_ref, w_ref, b_ref, o_ref, xs_scr, pw_scr, *, tiles_per_seq):
    i = pl.program_id(0)
    first = (i % tiles_per_seq) == 0
    last = ((i + 1) % tiles_per_seq) == 0
    H = CONV_HALO
    zero_halo = jnp.zeros((H, CONV_TC), BF16)
    xs_scr[0:H, :] = jnp.where(first, zero_halo, xp_ref[...])
    xs_scr[H:H + CONV_TS, :] = xm_ref[...]
    xs_scr[H + CONV_TS:, :] = jnp.where(last, zero_halo, xn_ref[...])
    xs = xs_scr[...]
    for n, k in enumerate(CONV_SIDE_TAPS):
        pw_scr[n] = xs * (0.5 * w_ref[k:k + 1, :]).astype(BF16)
    r = lax.broadcasted_iota(jnp.int32, (CONV_TB, len(CONV_SIDE_TAPS) * CONV_WIN), 0)
    c = lax.broadcasted_iota(jnp.int32, (CONV_TB, len(CONV_SIDE_TAPS) * CONV_WIN), 1)
    hit = None
    for n, k in enumerate(CONV_SIDE_TAPS):
        sel = c == r + (n * CONV_WIN + H + k - CONV_K // 2)
        hit = sel if hit is None else hit | sel
    shift = jnp.where(hit, 1.0, 0.0).astype(BF16)
    w_mid = 0.5 * w_ref[CONV_K // 2:CONV_K // 2 + 1, :]
    bias = 0.5 * b_ref[...]
    for blk in range(CONV_TS // CONV_TB):
        t0 = blk * CONV_TB
        taps = jnp.concatenate([pw_scr[n, t0:t0 + CONV_WIN, :] for n in range(len(CONV_SIDE_TAPS))], axis=0)
        mid = xs_scr[H + t0:H + t0 + CONV_TB, :].astype(F32) * w_mid + bias
        half = _dot(shift, taps) + mid
        o_ref[t0:t0 + CONV_TB, :] = (half + half * jnp.tanh(half)).astype(BF16)


def _conv(u, conv_w, conv_b, seq):
    n = u.shape[0]
    tiles_per_seq = seq // CONV_TS
    halo_per_tile = CONV_TS // CONV_HALO
    n_halo = n // CONV_HALO
    col0 = U_XBC // CONV_TC
    return pl.pallas_call(
        functools.partial(_conv_kernel, tiles_per_seq=tiles_per_seq),
        grid=(n // CONV_TS, CONV_DIM // CONV_TC),
        in_specs=[
            pl.BlockSpec((CONV_TS, CONV_TC), lambda i, j: (i, col0 + j)),
            pl.BlockSpec((CONV_HALO, CONV_TC),
                         lambda i, j: (jnp.maximum(i * halo_per_tile - 1, 0), col0 + j)),
            pl.BlockSpec((CONV_HALO, CONV_TC),
                         lambda i, j: (jnp.minimum((i + 1) * halo_per_tile, n_halo - 1), col0 + j)),
            pl.BlockSpec((CONV_K, CONV_TC), lambda i, j: (0, j)),
            pl.BlockSpec((1, CONV_TC), lambda i, j: (0, j)),
        ],
        out_specs=pl.BlockSpec((CONV_TS, CONV_TC), lambda i, j: (i, j)),
        out_shape=jax.ShapeDtypeStruct((n, CONV_DIM), BF16),
        scratch_shapes=[pltpu.VMEM((CONV_TS + 2 * CONV_HALO, CONV_TC), BF16),
                        pltpu.VMEM((len(CONV_SIDE_TAPS), CONV_TS + 2 * CONV_HALO, CONV_TC), BF16)],
        compiler_params=_params(("parallel", "parallel")),
        name="conv",
    )(u, u, u, conv_w, conv_b)


def _split3(a):
    hi = a.astype(BF16)
    r1 = a - hi.astype(F32)
    mid = r1.astype(BF16)
    lo = (r1 - mid.astype(F32)).astype(BF16)
    return hi, mid, lo


def _exact_lhs01(m01, a):
    hi, mid, lo = _split3(a)
    return _dot(jnp.concatenate([m01, m01, m01], axis=1), jnp.concatenate([hi, mid, lo], axis=0))


def _split2(a):
    hi = a.astype(BF16)
    mid = (a - hi.astype(F32)).astype(BF16)
    return jnp.concatenate([hi, mid], axis=1)


PREP_CHUNKS = 8
ROWS_PER_GROUP = 4 * SSD_HPG
LOG2E = 1.4426950408889634


def _ssd_prep_kernel(dt_ref, a_ref, cum_ref, wdec_ref, ecum_ref, rows_ref):
    L = CHUNK
    H = SSD_HPG
    row = lax.broadcasted_iota(jnp.int32, (L, L), 0)
    col = lax.broadcasted_iota(jnp.int32, (L, L), 1)
    lower = (row >= col).astype(BF16)
    upper = (row <= col).astype(BF16)
    is_fwd = col < SSD_HEADS
    a = -jnp.exp(a_ref[...])
    for t in range(PREP_CHUNKS):
        sl = slice(t * L, (t + 1) * L)
        dt = dt_ref[sl, :]
        dA = dt * a
        pre = _exact_lhs01(lower, dA)
        suf = _exact_lhs01(upper, dA)
        cum = jnp.where(is_fwd, pre, suf)
        tot = jnp.where(is_fwd[:1], pre[L - 1:L, :], suf[0:1, :])
        cum_ref[sl, :] = cum * LOG2E
        wdec_ref[sl, :] = _split2(jnp.exp(tot - cum) * dt)
        ecum_ref[sl, :] = _split2(jnp.exp(cum))
        dt_t = dt.T
        src_t = (cum.T - jnp.log(dt_t)) * LOG2E
        lds_t = jnp.log2(dt_t[:SSD_HEADS] + dt_t[SSD_HEADS:])
        etot = jnp.broadcast_to(jnp.exp(tot), (H, L))
        for g in range(SSD_GROUPS):
            f = slice(g * H, (g + 1) * H)
            b = slice(SSD_HEADS + g * H, SSD_HEADS + (g + 1) * H)
            rows_ref[t, g] = jnp.concatenate([src_t[f], src_t[b], lds_t[f], etot], axis=0)


def _ssd_prep(dt, a_log):
    n = dt.shape[0]
    rows_blk = PREP_CHUNKS * CHUNK
    nh = 2 * SSD_HEADS
    return pl.pallas_call(
        _ssd_prep_kernel,
        grid=(n // rows_blk,),
        in_specs=[pl.BlockSpec((rows_blk, nh), lambda i: (i, 0)), pl.BlockSpec((1, nh), lambda i: (0, 0))],
        out_specs=[pl.BlockSpec((rows_blk, nh), lambda i: (i, 0)),
                   pl.BlockSpec((rows_blk, 2 * nh), lambda i: (i, 0)),
                   pl.BlockSpec((rows_blk, 2 * nh), lambda i: (i, 0)),
                   pl.BlockSpec((PREP_CHUNKS, SSD_GROUPS, ROWS_PER_GROUP, CHUNK), lambda i: (i, 0, 0, 0))],
        out_shape=[jax.ShapeDtypeStruct((n, nh), F32),
                   jax.ShapeDtypeStruct((n, 2 * nh), BF16),
                   jax.ShapeDtypeStruct((n, 2 * nh), BF16),
                   jax.ShapeDtypeStruct((n // CHUNK, SSD_GROUPS, ROWS_PER_GROUP, CHUNK), F32)],
        compiler_params=_params(("parallel",)),
        name="ssd_prep",
    )(dt, a_log)


SSD_T = 16


def _ssd_kernel(x_ref, b_ref, c_ref, z_ref, cum_ref, wdec_ref, ecum_ref, rows_ref, d_ref, e_ref,
                y_ref, gst_scr, h_scr):
    g = pl.program_id(1)
    phase = pl.program_id(2)
    step = pl.program_id(3)
    ns = pl.num_programs(3)
    L = CHUNK
    GX = GROUP_X
    H = SSD_HPG

    @pl.when(step == 0)
    def _():
        h_scr[...] = jnp.zeros_like(h_scr)

    def state_update(t, direction, xb, bm):
        sl = slice(t * L, (t + 1) * L)
        e_dir = e_ref[0, :, direction * GX:(direction + 1) * GX]
        w = _dot(wdec_ref[sl, :], e_dir)
        dec = _dot(_split2(rows_ref[t, 0, 3 * H:4 * H, :]), e_dir)[0:1, :]
        xdec = w.astype(BF16) * xb
        b_t = bm.astype(F32).T.astype(BF16)
        h_scr[...] = dec * h_scr[...] + _dot(b_t, xdec)

    @pl.when(phase == 0)
    def _():
        for t in reversed(range(SSD_T)):
            sl = slice(t * L, (t + 1) * L)
            gst_scr[(ns - 1 - step) * SSD_T + t] = h_scr[...].astype(BF16)
            state_update(t, 1, x_ref[sl, :], b_ref[sl, :])

    @pl.when(phase == 1)
    def _():
        row = lax.broadcasted_iota(jnp.int32, (L, L), 0)
        col = lax.broadcasted_iota(jnp.int32, (L, L), 1)
        lower = row >= col
        diag = row == col
        low_lane = col < SSD_HEADDIM
        shift = (2 * SSD_HEADS - H * g) % (2 * SSD_HEADS)
        for t in range(SSD_T):
            sl = slice(t * L, (t + 1) * L)
            cm = c_ref[sl, :]
            bm = b_ref[sl, :]
            xb = x_ref[sl, :]
            xf = xb.astype(F32)
            cb = lax.dot_general(cm, bm, (((1,), (1,)), ((), ())), preferred_element_type=F32)
            cum_r = pltpu.roll(cum_ref[sl, :], shift, axis=1)
            pairs = []
            for k in range(H // 2):
                ms = []
                for half in range(2):
                    hh = 2 * k + half
                    fwd = cum_r[:, hh:hh + 1] - rows_ref[t, 0, hh:hh + 1, :]
                    bwd = cum_r[:, SSD_HEADS + hh:SSD_HEADS + hh + 1] - rows_ref[t, 0, H + hh:H + hh + 1, :]
                    arg = jnp.where(diag, rows_ref[t, 0, 2 * H + hh:2 * H + hh + 1, :],
                                    jnp.where(lower, fwd, bwd))
                    ms.append((jnp.exp2(arg) * cb).astype(BF16))
                xp = xb[:, k * L:(k + 1) * L]
                zero = jnp.zeros_like(xp)
                rhs = jnp.concatenate([jnp.where(low_lane, xp, zero), jnp.where(low_lane, zero, xp)], axis=0)
                pairs.append(_dot(jnp.concatenate(ms, axis=1), rhs))
            y = jnp.concatenate(pairs, axis=1)
            c = step * SSD_T + t
            states = jnp.concatenate([h_scr[...].astype(BF16), gst_scr[c]], axis=1)
            y_off = _dot(ecum_ref[sl, :], e_ref[0]) * _dot(cm, states)
            y = y + y_off[:, :GX] + y_off[:, GX:] + d_ref[...] * xf
            y_ref[sl, :] = (y * _silu(z_ref[sl, :].astype(F32))).astype(BF16)
            state_update(t, 0, xb, bm)


def _ssd(xc, u, cum, wdec, ecum, rows, d_exp, e01, batch, seq):
    n = xc.shape[0]
    nc = seq // CHUNK
    ns = nc // SSD_T
    tl = SSD_T * CHUNK
    nh = 2 * SSD_HEADS
    gx_blocks = D_INNER // D_STATE

    def fb_row(b, p, s):
        return b * ns + jnp.where(p == 0, ns - 1 - s, s)

    def f_row(b, p, s):
        return b * ns + jnp.where(p == 0, 0, s)

    return pl.pallas_call(
        _ssd_kernel,
        grid=(batch, SSD_GROUPS, 2, ns),
        in_specs=[
            pl.BlockSpec((tl, GROUP_X), lambda b, g, p, s: (fb_row(b, p, s), g)),
            pl.BlockSpec((tl, D_STATE), lambda b, g, p, s: (fb_row(b, p, s), gx_blocks + g)),
            pl.BlockSpec((tl, D_STATE), lambda b, g, p, s: (f_row(b, p, s), gx_blocks + SSD_GROUPS + g)),
            pl.BlockSpec((tl, GROUP_X), lambda b, g, p, s: (f_row(b, p, s), U_Z // GROUP_X + g)),
            pl.BlockSpec((tl, nh), lambda b, g, p, s: (f_row(b, p, s), 0)),
            pl.BlockSpec((tl, 2 * nh), lambda b, g, p, s: (fb_row(b, p, s), 0)),
            pl.BlockSpec((tl, 2 * nh), lambda b, g, p, s: (f_row(b, p, s), 0)),
            pl.BlockSpec((SSD_T, 1, ROWS_PER_GROUP, CHUNK), lambda b, g, p, s: (fb_row(b, p, s), g, 0, 0)),
            pl.BlockSpec((1, GROUP_X), lambda b, g, p, s: (0, g)),
            pl.BlockSpec((1, 2 * nh, 2 * GROUP_X), lambda b, g, p, s: (g, 0, 0)),
        ],
        out_specs=pl.BlockSpec((tl, GROUP_X), lambda b, g, p, s: (f_row(b, p, s), g)),
        out_shape=jax.ShapeDtypeStruct((n, D_INNER), BF16),
        scratch_shapes=[
            pltpu.VMEM((nc, D_STATE, GROUP_X), BF16),
            pltpu.VMEM((D_STATE, GROUP_X), F32),
        ],
        compiler_params=_params(("arbitrary", "arbitrary", "arbitrary", "arbitrary")),
        name="ssd",
    )(xc, xc, xc, u, cum, wdec, ecum, rows, d_exp, e01)


def _ssd_expand_matrix():
    e = np.zeros((SSD_GROUPS, 2 * SSD_HEADS, 2 * GROUP_X), np.float32)
    for g in range(SSD_GROUPS):
        for d in range(2):
            for r in range(SSD_HPG):
                lo = d * GROUP_X + r * SSD_HEADDIM
                e[g, d * SSD_HEADS + g * SSD_HPG + r, lo:lo + SSD_HEADDIM] = 1.0
    return np.concatenate([e, e], axis=1)


def _t5_bucket(rel):
    nb = N_BUCKETS // 2
    ret = (rel > 0).astype(np.int32) * nb
    n = np.abs(rel)
    max_exact = nb // 2
    large = max_exact + (np.log(np.maximum(n, 1) / max_exact) / math.log(MAX_DIST / max_exact)
                         * (nb - max_exact)).astype(np.int32)
    large = np.minimum(large, nb - 1)
    return (ret + np.where(n < max_exact, n, large)).astype(np.int32)


def _bucket_table():
    i = np.arange(ATT_BLOCK)[:, None]
    j = np.arange(3 * ATT_BLOCK)[None, :]
    rel = j - ATT_BLOCK - i
    return np.where(np.abs(rel) <= WINDOW, _t5_bucket(rel), -1).astype(np.int32)


N_EDGE_VARIANTS = 4


def _bias_kernel(rel_ref, bucket_ref, o_ref):
    variant = pl.program_id(0)
    bucket = bucket_ref[...]
    kcol = lax.broadcasted_iota(jnp.int32, bucket.shape, 1)
    dead = (((variant & 1) == 1) & (kcol < ATT_BLOCK)) | (((variant & 2) == 2) & (kcol >= 2 * ATT_BLOCK))
    for h in range(ATT_Q_HEADS):
        acc = jnp.full(bucket.shape, NEG_BIG, F32)
        for b in range(N_BUCKETS):
            acc = jnp.where(bucket == b, rel_ref[b, h] * LOG2E, acc)
        o_ref[0, h] = jnp.where(dead, NEG_BIG, acc)


def _bias_table(rel_bias):
    bucket = jnp.asarray(_bucket_table())
    return pl.pallas_call(
        _bias_kernel,
        grid=(N_EDGE_VARIANTS,),
        in_specs=[
            pl.BlockSpec(memory_space=pltpu.SMEM),
            pl.BlockSpec((ATT_BLOCK, 3 * ATT_BLOCK), lambda v: (0, 0)),
        ],
        out_specs=pl.BlockSpec((1, ATT_Q_HEADS, ATT_BLOCK, 3 * ATT_BLOCK), lambda v: (v, 0, 0, 0)),
        out_shape=jax.ShapeDtypeStruct((N_EDGE_VARIANTS, ATT_Q_HEADS, ATT_BLOCK, 3 * ATT_BLOCK), F32),
        compiler_params=_params(("arbitrary",)),
        name="rel_bias_table",
    )(rel_bias, bucket)


def _attn_kv_head(g, q_ref, kp_ref, kc_ref, kn_ref, vp_ref, vc_ref, vn_ref, bias_ref, sink_ref, store):
    T = ATT_BLOCK
    hd = ATT_HEAD_DIM
    ones = jnp.ones((3 * T, hd), BF16)
    ks = slice(g * hd, (g + 1) * hd)
    kwin = jnp.concatenate([kp_ref[:, ks], kc_ref[:, ks], kn_ref[:, ks]], axis=0)
    vwin = jnp.concatenate([vp_ref[:, ks], vc_ref[:, ks], vn_ref[:, ks]], axis=0)
    qs = jnp.concatenate(
        [q_ref[:, (g * ATT_REP + r) * hd:(g * ATT_REP + r + 1) * hd] for r in range(ATT_REP)],
        axis=0)
    s = lax.dot_general(qs, kwin, (((1,), (1,)), ((), ())), preferred_element_type=F32)
    s = s + jnp.concatenate([bias_ref[0, g * ATT_REP + r] for r in range(ATT_REP)], axis=0)
    sink = jnp.concatenate(
        [jnp.full((T, 1), sink_ref[g * ATT_REP + r] * LOG2E, F32) for r in range(ATT_REP)], axis=0)
    m = jnp.maximum(jnp.max(s, axis=-1, keepdims=True), sink)
    p = jnp.exp2(s - m).astype(BF16)
    ov = _dot(p, jnp.concatenate([vwin, ones], axis=1))
    o = ov[:, :hd] / (ov[:, hd:] + jnp.exp2(sink - m))
    for r in range(ATT_REP):
        store(g * ATT_REP + r, o[r * T:(r + 1) * T, :].astype(BF16))


BR_TM = 512
BR_TN = 512
BR_STEPS = D_MODEL // BR_TN
assert BR_TM == BR_STEPS * ATT_BLOCK


def _attn_branch_kernel(q_ref, kp_ref, kc_ref, kn_ref, vp_ref, vc_ref, vn_ref, bias_ref, sink_ref,
                        ys_ref, ga_ref, gb_ref, ng_ref, ws_ref, wa_ref, o_ref, ya_even, ya_odd):
    i = pl.program_id(0)
    j = pl.program_id(1)
    hd = ATT_HEAD_DIM

    @pl.when((i == 0) & (j == 0))
    def _():
        ya_odd[...] = jnp.zeros_like(ya_odd)

    def step(fill_scr, use_scr):
        row0 = pl.multiple_of(j * ATT_BLOCK, ATT_BLOCK)

        def store(head, val):
            fill_scr[pl.ds(row0, ATT_BLOCK), head * hd:(head + 1) * hd] = val

        n_slices = ATT_KV_HEADS
        ka = D_INNER // n_slices
        kb = D_MODEL // n_slices
        acc = [None, None, None]

        def matmul_slice(c):
            yg = ys_ref[:, c * ka:(c + 1) * ka]
            yf = yg.astype(F32)
            pq = jnp.sum(yf * yf, axis=-1, keepdims=True)
            pa = _dot(yg * ng_ref[:, c * ka:(c + 1) * ka].astype(BF16), ws_ref[c * ka:(c + 1) * ka, :])
            pb = _dot(use_scr[:, c * kb:(c + 1) * kb], wa_ref[c * kb:(c + 1) * kb, :])
            acc[:] = (pa, pb, pq) if acc[0] is None else (acc[0] + pa, acc[1] + pb, acc[2] + pq)

        for g in range(ATT_KV_HEADS):
            _attn_kv_head(g, q_ref, kp_ref, kc_ref, kn_ref, vp_ref, vc_ref, vn_ref, bias_ref, sink_ref, store)
            matmul_slice(g)
        a, b, ssq = acc
        scale = lax.rsqrt(ssq * (1.0 / D_INNER) + EPS)
        a = jax.nn.sigmoid(ga_ref[...].astype(F32)) * (scale * a)
        b = jax.nn.sigmoid(gb_ref[...].astype(F32)) * b
        o_ref[...] = (a + b).astype(BF16)

    @pl.when(i % 2 == 0)
    def _():
        step(ya_even, ya_odd)

    @pl.when(i % 2 == 1)
    def _():
        step(ya_odd, ya_even)


def _attn_branch(ys, u, bias, sink, norm_gain, w_ssd, w_att, seq):
    n = ys.shape[0]
    nblk = n // ATT_BLOCK
    ntile = n // BR_TM
    blocks_per_seq = seq // ATT_BLOCK
    kcol = U_K // KV_DIM
    vcol = U_V // KV_DIM
    blk = lambda i, j: jnp.minimum(i * BR_STEPS + j, nblk - 1)
    prev = lambda i, j: jnp.maximum(blk(i, j) - 1, 0)
    nxt = lambda i, j: jnp.minimum(blk(i, j) + 1, nblk - 1)
    tile = lambda i: jnp.maximum(i - 1, 0)
    kv_spec = lambda rowf, c: pl.BlockSpec((ATT_BLOCK, KV_DIM), lambda i, j: (rowf(i, j), c))

    def edge_variant(i, j):
        b = blk(i, j)
        first = (b % blocks_per_seq == 0).astype(jnp.int32)
        last = ((b + 1) % blocks_per_seq == 0).astype(jnp.int32)
        return first + 2 * last

    return pl.pallas_call(
        _attn_branch_kernel,
        grid=(ntile + 1, BR_STEPS),
        in_specs=[
            pl.BlockSpec((ATT_BLOCK, D_MODEL), lambda i, j: (blk(i, j), U_Q // D_MODEL)),
            kv_spec(prev, kcol), kv_spec(blk, kcol), kv_spec(nxt, kcol),
            kv_spec(prev, vcol), kv_spec(blk, vcol), kv_spec(nxt, vcol),
            pl.BlockSpec((1, ATT_Q_HEADS, ATT_BLOCK, 3 * ATT_BLOCK), lambda i, j: (edge_variant(i, j), 0, 0, 0)),
            pl.BlockSpec(memory_space=pltpu.SMEM),
            pl.BlockSpec((BR_TM, D_INNER), lambda i, j: (tile(i), 0)),
            pl.BlockSpec((BR_TM, BR_TN), lambda i, j: (tile(i), U_GA // BR_TN + j)),
            pl.BlockSpec((BR_TM, BR_TN), lambda i, j: (tile(i), U_GB // BR_TN + j)),
            pl.BlockSpec((1, D_INNER), lambda i, j: (0, 0)),
            pl.BlockSpec((D_INNER, BR_TN), lambda i, j: (0, j)),
            pl.BlockSpec((D_MODEL, BR_TN), lambda i, j: (0, j)),
        ],
        out_specs=pl.BlockSpec((BR_TM, BR_TN), lambda i, j: (i, j)),
        out_shape=jax.ShapeDtypeStruct((n + BR_TM, D_MODEL), BF16),
        scratch_shapes=[pltpu.VMEM((BR_TM, D_MODEL), BF16), pltpu.VMEM((BR_TM, D_MODEL), BF16)],
        compiler_params=_params(("arbitrary", "arbitrary")),
        name="attn_branch",
    )(u, u, u, u, u, u, u, bias, sink, ys, u, u, norm_gain, w_ssd, w_att)


OUT_TM = 512


def _outproj_kernel(m_ref, w_ref, x_ref, o_ref):
    o_ref[...] = x_ref[...] + _dot(m_ref[...], w_ref[...])


def _outproj(merged, w_out, x1):
    assert OUT_TM == BR_TM
    n = x1.shape[0]
    return pl.pallas_call(
        _outproj_kernel,
        grid=(n // OUT_TM,),
        in_specs=[
            pl.BlockSpec((OUT_TM, D_MODEL), lambda i: (i + 1, 0)),
            pl.BlockSpec((D_MODEL, D_MODEL), lambda i: (0, 0)),
            pl.BlockSpec((OUT_TM, D_MODEL), lambda i: (i, 0)),
        ],
        out_specs=pl.BlockSpec((OUT_TM, D_MODEL), lambda i: (i, 0)),
        out_shape=jax.ShapeDtypeStruct((n, D_MODEL), F32),
        compiler_params=_params(("parallel",)),
        name="out_proj",
    )(merged, w_out, x1)


def _prepare(ffn1_norm, ffn1_w_gate, ffn1_w_up, ffn1_w_down, mix_norm, w_in, conv_w, conv_b,
             ssd_A_log, ssd_dt_bias, ssd_D, ssd_out_norm, q_norm, k_norm, attn_sink, rel_bias,
             w_branch_ssd, w_branch_attn, w_out, ffn2_norm, ffn2_w_gate, ffn2_w_up, ffn2_w_down,
             final_norm):
    l = 0
    w = w_in[l]
    o_z = 0
    o_xbc = o_z + D_INNER
    o_dt = o_xbc + CONV_DIM
    o_q = o_dt + 2 * SSD_HEADS
    o_k = o_q + D_MODEL
    o_v = o_k + KV_DIM
    o_ga = o_v + KV_DIM
    o_gb = o_ga + D_MODEL
    w_u = jnp.concatenate(
        [w[:, o_z:o_dt], w[:, o_q:o_k], w[:, o_ga:o_gb + D_MODEL], w[:, o_k:o_ga]], axis=1).astype(BF16)
    row = lambda v: v.reshape(1, -1).astype(F32)
    return dict(
        ffn1=(row(ffn1_norm[l]), ffn1_w_gate[l].astype(BF16), ffn1_w_up[l].astype(BF16),
              ffn1_w_down[l].astype(BF16), row(mix_norm[l])),
        ffn2=(row(ffn2_norm[l]), ffn2_w_gate[l].astype(BF16), ffn2_w_up[l].astype(BF16),
              ffn2_w_down[l].astype(BF16), row(final_norm[l])),
        w_u=w_u, w_dt=w[:, o_dt:o_q].astype(BF16), dt_bias=row(ssd_dt_bias[l]),
        conv_w=conv_w[l].astype(F32), conv_b=row(conv_b[l]),
        a_log=row(ssd_A_log[l]), d_exp=row(jnp.repeat(ssd_D[l], SSD_HEADDIM)),
        e01=jnp.asarray(_ssd_expand_matrix(), BF16), out_norm=row(ssd_out_norm[l]),
        q_gain=row(q_norm[l]), k_gain=row(k_norm[l]), sink=attn_sink[l].astype(F32),
        bias=_bias_table(rel_bias.astype(F32)),
        w_ssd=w_branch_ssd[l].astype(BF16), w_att=w_branch_attn[l].astype(BF16),
        w_out=w_out[l].astype(BF16),
    )


def _layer(x, p):
    batch, seq, _ = x.shape
    x2d = x.reshape(batch * seq, D_MODEL)
    x1, h = _ffn(x2d, *p["ffn1"], final=False)
    u, dt = _inproj(h, p["w_u"], p["w_dt"], p["dt_bias"], p["q_gain"], p["k_gain"])
    xc = _conv(u, p["conv_w"], p["conv_b"], seq)
    cum, wdec, ecum, rows = _ssd_prep(dt, p["a_log"])
    ys = _ssd(xc, u, cum, wdec, ecum, rows, p["d_exp"], p["e01"], batch, seq)
    merged = _attn_branch(ys, u, p["bias"], p["sink"], p["out_norm"], p["w_ssd"], p["w_att"], seq)
    x2 = _outproj(merged, p["w_out"], x1)
    (y,) = _ffn(x2, *p["ffn2"], final=True)
    return y.reshape(batch, seq, D_MODEL)


def kernel(x_prompt, x_sample, ffn1_norm, ffn1_w_gate, ffn1_w_up, ffn1_w_down, mix_norm, w_in, conv_w, conv_b, ssd_A_log, ssd_dt_bias, ssd_D, ssd_out_norm, q_norm, k_norm, attn_sink, rel_bias, w_branch_ssd, w_branch_attn, w_out, ffn2_norm, ffn2_w_gate, ffn2_w_up, ffn2_w_down, final_norm):
    p = _prepare(ffn1_norm, ffn1_w_gate, ffn1_w_up, ffn1_w_down, mix_norm, w_in, conv_w, conv_b,
                 ssd_A_log, ssd_dt_bias, ssd_D, ssd_out_norm, q_norm, k_norm, attn_sink, rel_bias,
                 w_branch_ssd, w_branch_attn, w_out, ffn2_norm, ffn2_w_gate, ffn2_w_up, ffn2_w_down,
                 final_norm)
    return (_layer(x_prompt, p), _layer(x_sample, p))
```

```python
import functools
import math

import jax
import jax.numpy as jnp
import numpy as np
from jax import lax
from jax.experimental import pallas as pl
from jax.experimental.pallas import tpu as pltpu

D_MODEL = 2048
D_INNER = 2 * D_MODEL
SSD_HEADDIM = 64
SSD_HEADS = D_INNER // SSD_HEADDIM
SSD_GROUPS = 8
SSD_HPG = SSD_HEADS // SSD_GROUPS
D_STATE = 128
CONV_K = 5
CHUNK = 128
CONV_DIM = D_INNER + 2 * SSD_GROUPS * D_STATE
ATT_HEAD_DIM = 128
ATT_Q_HEADS = D_MODEL // ATT_HEAD_DIM
ATT_KV_HEADS = 4
ATT_REP = ATT_Q_HEADS // ATT_KV_HEADS
WINDOW = 128
ATT_BLOCK = 128
N_BUCKETS = 32
MAX_DIST = 128
D_FF = ((8 * D_MODEL // 3 + 255) // 256) * 256
EPS = 1e-6
GROUP_X = SSD_HPG * SSD_HEADDIM
KV_DIM = ATT_KV_HEADS * ATT_HEAD_DIM

U_Z = 0
U_XBC = U_Z + D_INNER
U_Q = U_XBC + CONV_DIM
U_GATES = U_Q + D_MODEL
U_K = U_GATES + 2 * D_MODEL
U_V = U_K + KV_DIM
U_TOTAL = U_V + KV_DIM

V7X_VMEM_LIMIT_BYTES = 56 * 1024 * 1024
NEG_BIG = -1e30

BF16 = jnp.bfloat16
F32 = jnp.float32


def _params(semantics):
    return pltpu.CompilerParams(dimension_semantics=semantics,
                                vmem_limit_bytes=V7X_VMEM_LIMIT_BYTES)


def _rms(x, gain):
    return x * lax.rsqrt(jnp.mean(x * x, axis=-1, keepdims=True) + EPS) * gain


def _silu(x):
    return x * jax.nn.sigmoid(x)


def _dot(a, b):
    return jnp.dot(a, b, preferred_element_type=F32)


FFN_TM = 512
FFN_TF = 512


def _ffn_kernel(x_ref, g_ref, wg_ref, wu_ref, wd_ref, pg_ref, *refs, final):
    if final:
        y_ref, h_scr, acc_scr = refs
    else:
        x1_ref, h_ref, h_scr, acc_scr = refs
    j = pl.program_id(1)

    @pl.when(j == 0)
    def _():
        h_scr[...] = _rms(x_ref[...], g_ref[...]).astype(BF16)
        acc_scr[...] = jnp.zeros_like(acc_scr)

    h = h_scr[...]
    t = _silu(_dot(h, wg_ref[...])) * _dot(h, wu_ref[...])
    acc_scr[...] += _dot(t.astype(BF16), wd_ref[...])

    @pl.when(j == pl.num_programs(1) - 1)
    def _():
        x1 = x_ref[...] + 0.5 * acc_scr[...]
        normed = _rms(x1, pg_ref[...])
        if final:
            y_ref[...] = normed
        else:
            x1_ref[...] = x1
            h_ref[...] = normed.astype(BF16)


def _ffn(x, gain, wg, wu, wd, post_gain, final):
    n = x.shape[0]
    row = lambda i, j: (i, 0)
    fixed = lambda i, j: (0, 0)
    out_shape = [jax.ShapeDtypeStruct((n, D_MODEL), F32)]
    out_specs = [pl.BlockSpec((FFN_TM, D_MODEL), row)]
    if not final:
        out_shape.append(jax.ShapeDtypeStruct((n, D_MODEL), BF16))
        out_specs.append(pl.BlockSpec((FFN_TM, D_MODEL), row))
    return pl.pallas_call(
        functools.partial(_ffn_kernel, final=final),
        grid=(n // FFN_TM, D_FF // FFN_TF),
        in_specs=[
            pl.BlockSpec((FFN_TM, D_MODEL), row),
            pl.BlockSpec((1, D_MODEL), fixed),
            pl.BlockSpec((D_MODEL, FFN_TF), lambda i, j: (0, j)),
            pl.BlockSpec((D_MODEL, FFN_TF), lambda i, j: (0, j)),
            pl.BlockSpec((FFN_TF, D_MODEL), lambda i, j: (j, 0)),
            pl.BlockSpec((1, D_MODEL), fixed),
        ],
        out_specs=out_specs,
        out_shape=out_shape,
        scratch_shapes=[pltpu.VMEM((FFN_TM, D_MODEL), BF16), pltpu.VMEM((FFN_TM, D_MODEL), F32)],
        compiler_params=_params(("parallel", "arbitrary")),
        name="ffn_final" if final else "ffn",
    )(x, gain, wg, wu, wd, post_gain)


INP_TM = 2048
INP_TN = 1024


def _inproj_kernel(h_ref, w_ref, wdt_ref, bdt_ref, qg_ref, kg_ref, u_ref, dt_ref):
    h = h_ref[...]
    acc = _dot(h, w_ref[...])
    j = pl.program_id(1)
    hd = ATT_HEAD_DIM
    heads_per_tile = INP_TN // hd
    is_q = (j >= U_Q // INP_TN) & (j < U_GATES // INP_TN)
    is_kv = j == U_K // INP_TN

    u_ref[...] = acc.astype(BF16)

    def norm_heads(gain, n_normed):
        for hh in range(n_normed):
            cols = slice(hh * hd, (hh + 1) * hd)
            u_ref[:, cols] = _rms(u_ref[:, cols].astype(F32), gain).astype(BF16)

    @pl.when(is_q)
    def _():
        norm_heads(qg_ref[...] * (hd ** -0.5 * LOG2E), heads_per_tile)

    @pl.when(is_kv)
    def _():
        norm_heads(kg_ref[...], ATT_KV_HEADS)

    @pl.when(j == 0)
    def _():
        raw = _dot(h, wdt_ref[...]) + bdt_ref[...]
        dt_ref[...] = jnp.maximum(raw, 0.0) + jnp.log1p(jnp.exp(-jnp.abs(raw)))


def _inproj(h, w_u, w_dt, dt_bias, q_gain, k_gain):
    assert U_Q % INP_TN == 0 and U_GATES % INP_TN == 0 and U_K % INP_TN == 0 and KV_DIM * 2 == INP_TN
    n = h.shape[0]
    ndt = 2 * SSD_HEADS
    return pl.pallas_call(
        _inproj_kernel,
        grid=(n // INP_TM, U_TOTAL // INP_TN),
        in_specs=[
            pl.BlockSpec((INP_TM, D_MODEL), lambda i, j: (i, 0)),
            pl.BlockSpec((D_MODEL, INP_TN), lambda i, j: (0, j)),
            pl.BlockSpec((D_MODEL, ndt), lambda i, j: (0, 0)),
            pl.BlockSpec((1, ndt), lambda i, j: (0, 0)),
            pl.BlockSpec((1, ATT_HEAD_DIM), lambda i, j: (0, 0)),
            pl.BlockSpec((1, ATT_HEAD_DIM), lambda i, j: (0, 0)),
        ],
        out_specs=[
            pl.BlockSpec((INP_TM, INP_TN), lambda i, j: (i, j)),
            pl.BlockSpec((INP_TM, ndt), lambda i, j: (i, 0)),
        ],
        out_shape=[jax.ShapeDtypeStruct((n, U_TOTAL), BF16), jax.ShapeDtypeStruct((n, ndt), F32)],
        compiler_params=_params(("parallel", "arbitrary")),
        name="in_proj",
    )(h, w_u, w_dt, dt_bias, q_gain, k_gain)


CONV_TS = 1024
CONV_TC = 2048
CONV_HALO = 16
CONV_TB = 32
CONV_WIN = CONV_TB + 2 * CONV_HALO
CONV_SIDE_TAPS = tuple(k for k in range(CONV_K) if k != CONV_K // 2)


def _conv_kernel(xm_ref, xp_ref, xn_ref, w_ref, b_ref, o_ref, xs_scr, pw_scr, *, tiles_per_seq):
    i = pl.program_id(0)
    first = (i % tiles_per_seq) == 0
    last = ((i + 1) % tiles_per_seq) == 0
    H = CONV_HALO
    zero_halo = jnp.zeros((H, CONV_TC), BF16)
    xs_scr[0:H, :] = jnp.where(first, zero_halo, xp_ref[...])
    xs_scr[H:H + CONV_TS, :] = xm_ref[...]
    xs_scr[H + CONV_TS:, :] = jnp.where(last, zero_halo, xn_ref[...])
    xs = xs_scr[...]
    for n, k in enumerate(CONV_SIDE_TAPS):
        pw_scr[n] = xs * (0.5 * w_ref[k:k + 1, :]).astype(BF16)
    r = lax.broadcasted_iota(jnp.int32, (CONV_TB, len(CONV_SIDE_TAPS) * CONV_WIN), 0)
    c = lax.broadcasted_iota(jnp.int32, (CONV_TB, len(CONV_SIDE_TAPS) * CONV_WIN), 1)
    hit = None
    for n, k in enumerate(CONV_SIDE_TAPS):
        sel = c == r + (n * CONV_WIN + H + k - CONV_K // 2)
        hit = sel if hit is None else hit | sel
    shift = jnp.where(hit, 1.0, 0.0).astype(BF16)
    w_mid = 0.5 * w_ref[CONV_K // 2:CONV_K // 2 + 1, :]
    bias = 0.5 * b_ref[...]
    for blk in range(CONV_TS // CONV_TB):
        t0 = blk * CONV_TB
        taps = jnp.concatenate([pw_scr[n, t0:t0 + CONV_WIN, :] for n in range(len(CONV_SIDE_TAPS))], axis=0)
        mid = xs_scr[H + t0:H + t0 + CONV_TB, :].astype(F32) * w_mid + bias
        half = _dot(shift, taps) + mid
        o_ref[t0:t0 + CONV_TB, :] = (half + half * jnp.tanh(half)).astype(BF16)


def _conv(u, conv_w, conv_b, seq):
    n = u.shape[0]
    tiles_per_seq = seq // CONV_TS
    halo_per_tile = CONV_TS // CONV_HALO
    n_halo = n // CONV_HALO
    col0 = U_XBC // CONV_TC
    return pl.pallas_call(
        functools.partial(_conv_kernel, tiles_per_seq=tiles_per_seq),
        grid=(n // CONV_TS, CONV_DIM // CONV_TC),
        in_specs=[
            pl.BlockSpec((CONV_TS, CONV_TC), lambda i, j: (i, col0 + j)),
            pl.BlockSpec((CONV_HALO, CONV_TC),
                         lambda i, j: (jnp.maximum(i * halo_per_tile - 1, 0), col0 + j)),
            pl.BlockSpec((CONV_HALO, CONV_TC),
                         lambda i, j: (jnp.minimum((i + 1) * halo_per_tile, n_halo - 1), col0 + j)),
            pl.BlockSpec((CONV_K, CONV_TC), lambda i, j: (0, j)),
            pl.BlockSpec((1, CONV_TC), lambda i, j: (0, j)),
        ],
        out_specs=pl.BlockSpec((CONV_TS, CONV_TC), lambda i, j: (i, j)),
        out_shape=jax.ShapeDtypeStruct((n, CONV_DIM), BF16),
        scratch_shapes=[pltpu.VMEM((CONV_TS + 2 * CONV_HALO, CONV_TC), BF16),
                        pltpu.VMEM((len(CONV_SIDE_TAPS), CONV_TS + 2 * CONV_HALO, CONV_TC), BF16)],
        compiler_params=_params(("parallel", "parallel")),
        name="conv",
    )(u, u, u, conv_w, conv_b)


def _split3(a):
    hi = a.astype(BF16)
    r1 = a - hi.astype(F32)
    mid = r1.astype(BF16)
    lo = (r1 - mid.astype(F32)).astype(BF16)
    return hi, mid, lo


def _exact_lhs01(m01, a):
    hi, mid, lo = _split3(a)
    return _dot(jnp.concatenate([m01, m01, m01], axis=1), jnp.concatenate([hi, mid, lo], axis=0))


def _split2(a):
    hi = a.astype(BF16)
    mid = (a - hi.astype(F32)).astype(BF16)
    return jnp.concatenate([hi, mid], axis=1)


PREP_CHUNKS = 8
ROWS_PER_GROUP = 4 * SSD_HPG
LOG2E = 1.4426950408889634


def _ssd_prep_kernel(dt_ref, a_ref, cum_ref, wdec_ref, ecum_ref, rows_ref):
    L = CHUNK
    H = SSD_HPG
    row = lax.broadcasted_iota(jnp.int32, (L, L), 0)
    col = lax.broadcasted_iota(jnp.int32, (L, L), 1)
    lower = (row >= col).astype(BF16)
    upper = (row <= col).astype(BF16)
    is_fwd = col < SSD_HEADS
    a = -jnp.exp(a_ref[...])
    for t in range(PREP_CHUNKS):
        sl = slice(t * L, (t + 1) * L)
        dt = dt_ref[sl, :]
        dA = dt * a
        pre = _exact_lhs01(lower, dA)
        suf = _exact_lhs01(upper, dA)
        cum = jnp.where(is_fwd, pre, suf)
        tot = jnp.where(is_fwd[:1], pre[L - 1:L, :], suf[0:1, :])
        cum_ref[sl, :] = cum * LOG2E
        wdec_ref[sl, :] = _split2(jnp.exp(tot - cum) * dt)
        ecum_ref[sl, :] = _split2(jnp.exp(cum))
        dt_t = dt.T
        src_t = (cum.T - jnp.log(dt_t)) * LOG2E
        lds_t = jnp.log2(dt_t[:SSD_HEADS] + dt_t[SSD_HEADS:])
        etot = jnp.broadcast_to(jnp.exp(tot), (H, L))
        for g in range(SSD_GROUPS):
            f = slice(g * H, (g + 1) * H)
            b = slice(SSD_HEADS + g * H, SSD_HEADS + (g + 1) * H)
            rows_ref[t, g] = jnp.concatenate([src_t[f], src_t[b], lds_t[f], etot], axis=0)


def _ssd_prep(dt, a_log):
    n = dt.shape[0]
    rows_blk = PREP_CHUNKS * CHUNK
    nh = 2 * SSD_HEADS
    return pl.pallas_call(
        _ssd_prep_kernel,
        grid=(n // rows_blk,),
        in_specs=[pl.BlockSpec((rows_blk, nh), lambda i: (i, 0)), pl.BlockSpec((1, nh), lambda i: (0, 0))],
        out_specs=[pl.BlockSpec((rows_blk, nh), lambda i: (i, 0)),
                   pl.BlockSpec((rows_blk, 2 * nh), lambda i: (i, 0)),
                   pl.BlockSpec((rows_blk, 2 * nh), lambda i: (i, 0)),
                   pl.BlockSpec((PREP_CHUNKS, SSD_GROUPS, ROWS_PER_GROUP, CHUNK), lambda i: (i, 0, 0, 0))],
        out_shape=[jax.ShapeDtypeStruct((n, nh), F32),
                   jax.ShapeDtypeStruct((n, 2 * nh), BF16),
                   jax.ShapeDtypeStruct((n, 2 * nh), BF16),
                   jax.ShapeDtypeStruct((n // CHUNK, SSD_GROUPS, ROWS_PER_GROUP, CHUNK), F32)],
        compiler_params=_params(("parallel",)),
        name="ssd_prep",
    )(dt, a_log)


SSD_T = 16


def _ssd_kernel(x_ref, b_ref, c_ref, z_ref, cum_ref, wdec_ref, ecum_ref, rows_ref, d_ref, e_ref,
                y_ref, gst_scr, h_scr):
    g = pl.program_id(1)
    phase = pl.program_id(2)
    step = pl.program_id(3)
    ns = pl.num_programs(3)
    L = CHUNK
    GX = GROUP_X
    H = SSD_HPG

    @pl.when(step == 0)
    def _():
        h_scr[...] = jnp.zeros_like(h_scr)

    def state_update(t, direction, xb, bm):
        sl = slice(t * L, (t + 1) * L)
        e_dir = e_ref[0, :, direction * GX:(direction + 1) * GX]
        w = _dot(wdec_ref[sl, :], e_dir)
        dec = _dot(_split2(rows_ref[t, 0, 3 * H:4 * H, :]), e_dir)[0:1, :]
        xdec = w.astype(BF16) * xb
        b_t = bm.astype(F32).T.astype(BF16)
        h_scr[...] = dec * h_scr[...] + _dot(b_t, xdec)

    @pl.when(phase == 0)
    def _():
        for t in reversed(range(SSD_T)):
            sl = slice(t * L, (t + 1) * L)
            gst_scr[(ns - 1 - step) * SSD_T + t] = h_scr[...].astype(BF16)
            state_update(t, 1, x_ref[sl, :], b_ref[sl, :])

    @pl.when(phase == 1)
    def _():
        row = lax.broadcasted_iota(jnp.int32, (L, L), 0)
        col = lax.broadcasted_iota(jnp.int32, (L, L), 1)
        lower = row >= col
        diag = row == col
        low_lane = col < SSD_HEADDIM
        shift = (2 * SSD_HEADS - H * g) % (2 * SSD_HEADS)
        for t in range(SSD_T):
            sl = slice(t * L, (t + 1) * L)
            cm = c_ref[sl, :]
            bm = b_ref[sl, :]
            xb = x_ref[sl, :]
            xf = xb.astype(F32)
            cb = lax.dot_general(cm, bm, (((1,), (1,)), ((), ())), preferred_element_type=F32)
            cum_r = pltpu.roll(cum_ref[sl, :], shift, axis=1)
            pairs = []
            for k in range(H // 2):
                ms = []
                for half in range(2):
                    hh = 2 * k + half
                    fwd = cum_r[:, hh:hh + 1] - rows_ref[t, 0, hh:hh + 1, :]
                    bwd = cum_r[:, SSD_HEADS + hh:SSD_HEADS + hh + 1] - rows_ref[t, 0, H + hh:H + hh + 1, :]
                    arg = jnp.where(diag, rows_ref[t, 0, 2 * H + hh:2 * H + hh + 1, :],
                                    jnp.where(lower, fwd, bwd))
                    ms.append((jnp.exp2(arg) * cb).astype(BF16))
                xp = xb[:, k * L:(k + 1) * L]
                zero = jnp.zeros_like(xp)
                rhs = jnp.concatenate([jnp.where(low_lane, xp, zero), jnp.where(low_lane, zero, xp)], axis=0)
                pairs.append(_dot(jnp.concatenate(ms, axis=1), rhs))
            y = jnp.concatenate(pairs, axis=1)
            c = step * SSD_T + t
            states = jnp.concatenate([h_scr[...].astype(BF16), gst_scr[c]], axis=1)
            y_off = _dot(ecum_ref[sl, :], e_ref[0]) * _dot(cm, states)
            y = y + y_off[:, :GX] + y_off[:, GX:] + d_ref[...] * xf
            y_ref[sl, :] = (y * _silu(z_ref[sl, :].astype(F32))).astype(BF16)
            state_update(t, 0, xb, bm)


def _ssd(xc, u, cum, wdec, ecum, rows, d_exp, e01, batch, seq):
    n = xc.shape[0]
    nc = seq // CHUNK
    ns = nc // SSD_T
    tl = SSD_T * CHUNK
    nh = 2 * SSD_HEADS
    gx_blocks = D_INNER // D_STATE

    def fb_row(b, p, s):
        return b * ns + jnp.where(p == 0, ns - 1 - s, s)

    def f_row(b, p, s):
        return b * ns + jnp.where(p == 0, 0, s)

    return pl.pallas_call(
        _ssd_kernel,
        grid=(batch, SSD_GROUPS, 2, ns),
        in_specs=[
            pl.BlockSpec((tl, GROUP_X), lambda b, g, p, s: (fb_row(b, p, s), g)),
            pl.BlockSpec((tl, D_STATE), lambda b, g, p, s: (fb_row(b, p, s), gx_blocks + g)),
            pl.BlockSpec((tl, D_STATE), lambda b, g, p, s: (f_row(b, p, s), gx_blocks + SSD_GROUPS + g)),
            pl.BlockSpec((tl, GROUP_X), lambda b, g, p, s: (f_row(b, p, s), U_Z // GROUP_X + g)),
            pl.BlockSpec((tl, nh), lambda b, g, p, s: (f_row(b, p, s), 0)),
            pl.BlockSpec((tl, 2 * nh), lambda b, g, p, s: (fb_row(b, p, s), 0)),
            pl.BlockSpec((tl, 2 * nh), lambda b, g, p, s: (f_row(b, p, s), 0)),
            pl.BlockSpec((SSD_T, 1, ROWS_PER_GROUP, CHUNK), lambda b, g, p, s: (fb_row(b, p, s), g, 0, 0)),
            pl.BlockSpec((1, GROUP_X), lambda b, g, p, s: (0, g)),
            pl.BlockSpec((1, 2 * nh, 2 * GROUP_X), lambda b, g, p, s: (g, 0, 0)),
        ],
        out_specs=pl.BlockSpec((tl, GROUP_X), lambda b, g, p, s: (f_row(b, p, s), g)),
        out_shape=jax.ShapeDtypeStruct((n, D_INNER), BF16),
        scratch_shapes=[
            pltpu.VMEM((nc, D_STATE, GROUP_X), BF16),
            pltpu.VMEM((D_STATE, GROUP_X), F32),
        ],
        compiler_params=_params(("arbitrary", "arbitrary", "arbitrary", "arbitrary")),
        name="ssd",
    )(xc, xc, xc, u, cum, wdec, ecum, rows, d_exp, e01)


def _ssd_expand_matrix():
    e = np.zeros((SSD_GROUPS, 2 * SSD_HEADS, 2 * GROUP_X), np.float32)
    for g in range(SSD_GROUPS):
        for d in range(2):
            for r in range(SSD_HPG):
                lo = d * GROUP_X + r * SSD_HEADDIM
                e[g, d * SSD_HEADS + g * SSD_HPG + r, lo:lo + SSD_HEADDIM] = 1.0
    return np.concatenate([e, e], axis=1)


def _t5_bucket(rel):
    nb = N_BUCKETS // 2
    ret = (rel > 0).astype(np.int32) * nb
    n = np.abs(rel)
    max_exact = nb // 2
    large = max_exact + (np.log(np.maximum(n, 1) / max_exact) / math.log(MAX_DIST / max_exact)
                         * (nb - max_exact)).astype(np.int32)
    large = np.minimum(large, nb - 1)
    return (ret + np.where(n < max_exact, n, large)).astype(np.int32)


def _bucket_table():
    i = np.arange(ATT_BLOCK)[:, None]
    j = np.arange(3 * ATT_BLOCK)[None, :]
    rel = j - ATT_BLOCK - i
    return np.where(np.abs(rel) <= WINDOW, _t5_bucket(rel), -1).astype(np.int32)


N_EDGE_VARIANTS = 4


def _bias_kernel(rel_ref, bucket_ref, o_ref):
    variant = pl.program_id(0)
    bucket = bucket_ref[...]
    kcol = lax.broadcasted_iota(jnp.int32, bucket.shape, 1)
    dead = (((variant & 1) == 1) & (kcol < ATT_BLOCK)) | (((variant & 2) == 2) & (kcol >= 2 * ATT_BLOCK))
    for h in range(ATT_Q_HEADS):
        acc = jnp.full(bucket.shape, NEG_BIG, F32)
        for b in range(N_BUCKETS):
            acc = jnp.where(bucket == b, rel_ref[b, h] * LOG2E, acc)
        o_ref[0, h] = jnp.where(dead, NEG_BIG, acc)


def _bias_table(rel_bias):
    bucket = jnp.asarray(_bucket_table())
    return pl.pallas_call(
        _bias_kernel,
        grid=(N_EDGE_VARIANTS,),
        in_specs=[
            pl.BlockSpec(memory_space=pltpu.SMEM),
            pl.BlockSpec((ATT_BLOCK, 3 * ATT_BLOCK), lambda v: (0, 0)),
        ],
        out_specs=pl.BlockSpec((1, ATT_Q_HEADS, ATT_BLOCK, 3 * ATT_BLOCK), lambda v: (v, 0, 0, 0)),
        out_shape=jax.ShapeDtypeStruct((N_EDGE_VARIANTS, ATT_Q_HEADS, ATT_BLOCK, 3 * ATT_BLOCK), F32),
        compiler_params=_params(("arbitrary",)),
        name="rel_bias_table",
    )(rel_bias, bucket)


def _attn_kv_head(g, q_ref, kvp_ref, kvc_ref, kvn_ref, bias_ref, sink_ref, store):
    T = ATT_BLOCK
    hd = ATT_HEAD_DIM
    ones = jnp.ones((3 * T, hd), BF16)
    ks = slice(g * hd, (g + 1) * hd)
    vs = slice(KV_DIM + g * hd, KV_DIM + (g + 1) * hd)
    kwin = jnp.concatenate([kvp_ref[:, ks], kvc_ref[:, ks], kvn_ref[:, ks]], axis=0)
    vwin = jnp.concatenate([kvp_ref[:, vs], kvc_ref[:, vs], kvn_ref[:, vs]], axis=0)
    qs = jnp.concatenate(
        [q_ref[:, (g * ATT_REP + r) * hd:(g * ATT_REP + r + 1) * hd] for r in range(ATT_REP)],
        axis=0)
    s = lax.dot_general(qs, kwin, (((1,), (1,)), ((), ())), preferred_element_type=F32)
    s = s + jnp.concatenate([bias_ref[0, g * ATT_REP + r] for r in range(ATT_REP)], axis=0)
    sink = jnp.concatenate(
        [jnp.full((T, 1), sink_ref[g * ATT_REP + r] * LOG2E, F32) for r in range(ATT_REP)], axis=0)
    m = jnp.maximum(jnp.max(s, axis=-1, keepdims=True), sink)
    p = jnp.exp2(s - m).astype(BF16)
    ov = _dot(p, jnp.concatenate([vwin, ones], axis=1))
    o = ov[:, :hd] / (ov[:, hd:] + jnp.exp2(sink - m))
    for r in range(ATT_REP):
        store(g * ATT_REP + r, o[r * T:(r + 1) * T, :].astype(BF16))


BR_TM = 512
BR_TN = 512
BR_STEPS = D_MODEL // BR_TN
assert BR_TM == BR_STEPS * ATT_BLOCK


def _attn_branch_kernel(q_ref, kvp_ref, kvc_ref, kvn_ref, bias_ref, sink_ref,
                        ys_ref, gab_ref, ng_ref, ws_ref, wa_ref, o_ref, ya_even, ya_odd):
    i = pl.program_id(0)
    j = pl.program_id(1)
    hd = ATT_HEAD_DIM

    @pl.when((i == 0) & (j == 0))
    def _():
        ya_odd[...] = jnp.zeros_like(ya_odd)

    def step(fill_scr, use_scr):
        row0 = pl.multiple_of(j * ATT_BLOCK, ATT_BLOCK)

        def store(head, val):
            fill_scr[pl.ds(row0, ATT_BLOCK), head * hd:(head + 1) * hd] = val

        n_slices = ATT_KV_HEADS
        ka = D_INNER // n_slices
        kb = D_MODEL // n_slices
        acc = [None, None, None]

        def matmul_slice(c):
            yg = ys_ref[:, c * ka:(c + 1) * ka]
            yf = yg.astype(F32)
            pq = jnp.sum(yf * yf, axis=-1, keepdims=True)
            pa = _dot(yg * ng_ref[:, c * ka:(c + 1) * ka].astype(BF16), ws_ref[c * ka:(c + 1) * ka, :])
            pb = _dot(use_scr[:, c * kb:(c + 1) * kb], wa_ref[c * kb:(c + 1) * kb, :])
            acc[:] = (pa, pb, pq) if acc[0] is None else (acc[0] + pa, acc[1] + pb, acc[2] + pq)

        for g in range(ATT_KV_HEADS):
            _attn_kv_head(g, q_ref, kvp_ref, kvc_ref, kvn_ref, bias_ref, sink_ref, store)
            matmul_slice(g)
        a, b, ssq = acc
        scale = lax.rsqrt(ssq * (1.0 / D_INNER) + EPS)
        a = jax.nn.sigmoid(gab_ref[:, :BR_TN].astype(F32)) * (scale * a)
        b = jax.nn.sigmoid(gab_ref[:, BR_TN:].astype(F32)) * b
        o_ref[...] = (a + b).astype(BF16)

    @pl.when(i % 2 == 0)
    def _():
        step(ya_even, ya_odd)

    @pl.when(i % 2 == 1)
    def _():
        step(ya_odd, ya_even)


def _attn_branch(ys, u, bias, sink, norm_gain, w_ssd, w_att, seq):
    n = ys.shape[0]
    nblk = n // ATT_BLOCK
    ntile = n // BR_TM
    blocks_per_seq = seq // ATT_BLOCK
    assert U_V == U_K + KV_DIM and U_K % (2 * KV_DIM) == 0
    blk = lambda i, j: jnp.minimum(i * BR_STEPS + j, nblk - 1)
    prev = lambda i, j: jnp.maximum(blk(i, j) - 1, 0)
    nxt = lambda i, j: jnp.minimum(blk(i, j) + 1, nblk - 1)
    tile = lambda i: jnp.maximum(i - 1, 0)
    kv_spec = lambda rowf: pl.BlockSpec((ATT_BLOCK, 2 * KV_DIM), lambda i, j: (rowf(i, j), U_K // (2 * KV_DIM)))

    def edge_variant(i, j):
        b = blk(i, j)
        first = (b % blocks_per_seq == 0).astype(jnp.int32)
        last = ((b + 1) % blocks_per_seq == 0).astype(jnp.int32)
        return first + 2 * last

    return pl.pallas_call(
        _attn_branch_kernel,
        grid=(ntile + 1, BR_STEPS),
        in_specs=[
            pl.BlockSpec((ATT_BLOCK, D_MODEL), lambda i, j: (blk(i, j), U_Q // D_MODEL)),
            kv_spec(prev), kv_spec(blk), kv_spec(nxt),
            pl.BlockSpec((1, ATT_Q_HEADS, ATT_BLOCK, 3 * ATT_BLOCK), lambda i, j: (edge_variant(i, j), 0, 0, 0)),
            pl.BlockSpec(memory_space=pltpu.SMEM),
            pl.BlockSpec((BR_TM, D_INNER), lambda i, j: (tile(i), 0)),
            pl.BlockSpec((BR_TM, 2 * BR_TN), lambda i, j: (tile(i), U_GATES // (2 * BR_TN) + j)),
            pl.BlockSpec((1, D_INNER), lambda i, j: (0, 0)),
            pl.BlockSpec((D_INNER, BR_TN), lambda i, j: (0, j)),
            pl.BlockSpec((D_MODEL, BR_TN), lambda i, j: (0, j)),
        ],
        out_specs=pl.BlockSpec((BR_TM, BR_TN), lambda i, j: (i, j)),
        out_shape=jax.ShapeDtypeStruct((n + BR_TM, D_MODEL), BF16),
        scratch_shapes=[pltpu.VMEM((BR_TM, D_MODEL), BF16), pltpu.VMEM((BR_TM, D_MODEL), BF16)],
        compiler_params=_params(("arbitrary", "arbitrary")),
        name="attn_branch",
    )(u, u, u, u, bias, sink, ys, u, norm_gain, w_ssd, w_att)


OUT_TM = 512


def _outproj_kernel(m_ref, w_ref, x_ref, o_ref):
    o_ref[...] = x_ref[...] + _dot(m_ref[...], w_ref[...])


def _outproj(merged, w_out, x1):
    assert OUT_TM == BR_TM
    n = x1.shape[0]
    return pl.pallas_call(
        _outproj_kernel,
        grid=(n // OUT_TM,),
        in_specs=[
            pl.BlockSpec((OUT_TM, D_MODEL), lambda i: (i + 1, 0)),
            pl.BlockSpec((D_MODEL, D_MODEL), lambda i: (0, 0)),
            pl.BlockSpec((OUT_TM, D_MODEL), lambda i: (i, 0)),
        ],
        out_specs=pl.BlockSpec((OUT_TM, D_MODEL), lambda i: (i, 0)),
        out_shape=jax.ShapeDtypeStruct((n, D_MODEL), F32),
        compiler_params=_params(("parallel",)),
        name="out_proj",
    )(merged, w_out, x1)


def _prepare(ffn1_norm, ffn1_w_gate, ffn1_w_up, ffn1_w_down, mix_norm, w_in, conv_w, conv_b,
             ssd_A_log, ssd_dt_bias, ssd_D, ssd_out_norm, q_norm, k_norm, attn_sink, rel_bias,
             w_branch_ssd, w_branch_attn, w_out, ffn2_norm, ffn2_w_gate, ffn2_w_up, ffn2_w_down,
             final_norm):
    l = 0
    w = w_in[l]
    o_z = 0
    o_xbc = o_z + D_INNER
    o_dt = o_xbc + CONV_DIM
    o_q = o_dt + 2 * SSD_HEADS
    o_k = o_q + D_MODEL
    o_v = o_k + KV_DIM
    o_ga = o_v + KV_DIM
    o_gb = o_ga + D_MODEL
    n_gt = D_MODEL // BR_TN
    gates = jnp.stack([w[:, o_ga:o_gb].reshape(D_MODEL, n_gt, BR_TN),
                       w[:, o_gb:o_gb + D_MODEL].reshape(D_MODEL, n_gt, BR_TN)], axis=2).reshape(D_MODEL, 2 * D_MODEL)
    w_u = jnp.concatenate([w[:, o_z:o_dt], w[:, o_q:o_k], gates, w[:, o_k:o_ga]], axis=1).astype(BF16)
    row = lambda v: v.reshape(1, -1).astype(F32)
    return dict(
        ffn1=(row(ffn1_norm[l]), ffn1_w_gate[l].astype(BF16), ffn1_w_up[l].astype(BF16),
              ffn1_w_down[l].astype(BF16), row(mix_norm[l])),
        ffn2=(row(ffn2_norm[l]), ffn2_w_gate[l].astype(BF16), ffn2_w_up[l].astype(BF16),
              ffn2_w_down[l].astype(BF16), row(final_norm[l])),
        w_u=w_u, w_dt=w[:, o_dt:o_q].astype(BF16), dt_bias=row(ssd_dt_bias[l]),
        conv_w=conv_w[l].astype(F32), conv_b=row(conv_b[l]),
        a_log=row(ssd_A_log[l]), d_exp=row(jnp.repeat(ssd_D[l], SSD_HEADDIM)),
        e01=jnp.asarray(_ssd_expand_matrix(), BF16), out_norm=row(ssd_out_norm[l]),
        q_gain=row(q_norm[l]), k_gain=row(k_norm[l]), sink=attn_sink[l].astype(F32),
        bias=_bias_table(rel_bias.astype(F32)),
        w_ssd=w_branch_ssd[l].astype(BF16), w_att=w_branch_attn[l].astype(BF16),
        w_out=w_out[l].astype(BF16),
    )


def _layer(x, p):
    batch, seq, _ = x.shape
    x2d = x.reshape(batch * seq, D_MODEL)
    x1, h = _ffn(x2d, *p["ffn1"], final=False)
    u, dt = _inproj(h, p["w_u"], p["w_dt"], p["dt_bias"], p["q_gain"], p["k_gain"])
    xc = _conv(u, p["conv_w"], p["conv_b"], seq)
    cum, wdec, ecum, rows = _ssd_prep(dt, p["a_log"])
    ys = _ssd(xc, u, cum, wdec, ecum, rows, p["d_exp"], p["e01"], batch, seq)
    merged = _attn_branch(ys, u, p["bias"], p["sink"], p["out_norm"], p["w_ssd"], p["w_att"], seq)
    x2 = _outproj(merged, p["w_out"], x1)
    (y,) = _ffn(x2, *p["ffn2"], final=True)
    return y.reshape(batch, seq, D_MODEL)


def kernel(x_prompt, x_sample, ffn1_norm, ffn1_w_gate, ffn1_w_up, ffn1_w_down, mix_norm, w_in, conv_w, conv_b, ssd_A_log, ssd_dt_bias, ssd_D, ssd_out_norm, q_norm, k_norm, attn_sink, rel_bias, w_branch_ssd, w_branch_attn, w_out, ffn2_norm, ffn2_w_gate, ffn2_w_up, ffn2_w_down, final_norm):
    p = _prepare(ffn1_norm, ffn1_w_gate, ffn1_w_up, ffn1_w_down, mix_norm, w_in, conv_w, conv_b,
                 ssd_A_log, ssd_dt_bias, ssd_D, ssd_out_norm, q_norm, k_norm, attn_sink, rel_bias,
                 w_branch_ssd, w_branch_attn, w_out, ffn2_norm, ffn2_w_gate, ffn2_w_up, ffn2_w_down,
                 final_norm)
    return (_layer(x_prompt, p), _layer(x_sample, p))
```

```python
import functools
import math

import jax
import jax.numpy as jnp
import numpy as np
from jax import lax
from jax.experimental import pallas as pl
from jax.experimental.pallas import tpu as pltpu

D_MODEL = 2048
D_INNER = 2 * D_MODEL
SSD_HEADDIM = 64
SSD_HEADS = D_INNER // SSD_HEADDIM
SSD_GROUPS = 8
SSD_HPG = SSD_HEADS // SSD_GROUPS
D_STATE = 128
CONV_K = 5
CHUNK = 128
CONV_DIM = D_INNER + 2 * SSD_GROUPS * D_STATE
ATT_HEAD_DIM = 128
ATT_Q_HEADS = D_MODEL // ATT_HEAD_DIM
ATT_KV_HEADS = 4
ATT_REP = ATT_Q_HEADS // ATT_KV_HEADS
WINDOW = 128
ATT_BLOCK = 128
N_BUCKETS = 32
MAX_DIST = 128
D_FF = ((8 * D_MODEL // 3 + 255) // 256) * 256
EPS = 1e-6
GROUP_X = SSD_HPG * SSD_HEADDIM
KV_DIM = ATT_KV_HEADS * ATT_HEAD_DIM

U_Z = 0
U_XBC = U_Z + D_INNER
U_Q = U_XBC + CONV_DIM
U_GA = U_Q + D_MODEL
U_GB = U_GA + D_MODEL
U_K = U_GB + D_MODEL
U_V = U_K + KV_DIM
U_TOTAL = U_V + KV_DIM

V7X_VMEM_LIMIT_BYTES = 56 * 1024 * 1024
NEG_BIG = -1e30

BF16 = jnp.bfloat16
F32 = jnp.float32


def _params(semantics):
    return pltpu.CompilerParams(dimension_semantics=semantics,
                                vmem_limit_bytes=V7X_VMEM_LIMIT_BYTES)


def _rms(x, gain):
    return x * lax.rsqrt(jnp.mean(x * x, axis=-1, keepdims=True) + EPS) * gain


def _silu(x):
    return x * jax.nn.sigmoid(x)


def _dot(a, b):
    return jnp.dot(a, b, preferred_element_type=F32)


FFN_TM = 512
FFN_TF = 512


def _ffn_kernel(x_ref, g_ref, wg_ref, wu_ref, wd_ref, pg_ref, *refs, final):
    if final:
        y_ref, h_scr, acc_scr = refs
    else:
        x1_ref, h_ref, h_scr, acc_scr = refs
    j = pl.program_id(1)

    @pl.when(j == 0)
    def _():
        h_scr[...] = _rms(x_ref[...], g_ref[...]).astype(BF16)
        acc_scr[...] = jnp.zeros_like(acc_scr)

    h = h_scr[...]
    t = _silu(_dot(h, wg_ref[...])) * _dot(h, wu_ref[...])
    acc_scr[...] += _dot(t.astype(BF16), wd_ref[...])

    @pl.when(j == pl.num_programs(1) - 1)
    def _():
        x1 = x_ref[...] + 0.5 * acc_scr[...]
        normed = _rms(x1, pg_ref[...])
        if final:
            y_ref[...] = normed
        else:
            x1_ref[...] = x1
            h_ref[...] = normed.astype(BF16)


def _ffn(x, gain, wg, wu, wd, post_gain, final):
    n = x.shape[0]
    row = lambda i, j: (i, 0)
    fixed = lambda i, j: (0, 0)
    out_shape = [jax.ShapeDtypeStruct((n, D_MODEL), F32)]
    out_specs = [pl.BlockSpec((FFN_TM, D_MODEL), row)]
    if not final:
        out_shape.append(jax.ShapeDtypeStruct((n, D_MODEL), BF16))
        out_specs.append(pl.BlockSpec((FFN_TM, D_MODEL), row))
    return pl.pallas_call(
        functools.partial(_ffn_kernel, final=final),
        grid=(n // FFN_TM, D_FF // FFN_TF),
        in_specs=[
            pl.BlockSpec((FFN_TM, D_MODEL), row),
            pl.BlockSpec((1, D_MODEL), fixed),
            pl.BlockSpec((D_MODEL, FFN_TF), lambda i, j: (0, j)),
            pl.BlockSpec((D_MODEL, FFN_TF), lambda i, j: (0, j)),
            pl.BlockSpec((FFN_TF, D_MODEL), lambda i, j: (j, 0)),
            pl.BlockSpec((1, D_MODEL), fixed),
        ],
        out_specs=out_specs,
        out_shape=out_shape,
        scratch_shapes=[pltpu.VMEM((FFN_TM, D_MODEL), BF16), pltpu.VMEM((FFN_TM, D_MODEL), F32)],
        compiler_params=_params(("parallel", "arbitrary")),
        name="ffn_final" if final else "ffn",
    )(x, gain, wg, wu, wd, post_gain)


INP_TM = 2048
INP_TN = 1024


def _inproj_kernel(h_ref, w_ref, wdt_ref, bdt_ref, qg_ref, kg_ref, u_ref, dt_ref):
    h = h_ref[...]
    acc = _dot(h, w_ref[...])
    j = pl.program_id(1)
    hd = ATT_HEAD_DIM
    heads_per_tile = INP_TN // hd
    is_q = (j >= U_Q // INP_TN) & (j < U_GA // INP_TN)
    is_kv = j == U_K // INP_TN

    u_ref[...] = acc.astype(BF16)

    def norm_heads(gain, n_normed):
        for hh in range(n_normed):
            cols = slice(hh * hd, (hh + 1) * hd)
            u_ref[:, cols] = _rms(u_ref[:, cols].astype(F32), gain).astype(BF16)

    @pl.when(is_q)
    def _():
        norm_heads(qg_ref[...] * (hd ** -0.5 * LOG2E), heads_per_tile)

    @pl.when(is_kv)
    def _():
        norm_heads(kg_ref[...], ATT_KV_HEADS)

    @pl.when(j == 0)
    def _():
        raw = _dot(h, wdt_ref[...]) + bdt_ref[...]
        dt_ref[...] = jnp.maximum(raw, 0.0) + jnp.log1p(jnp.exp(-jnp.abs(raw)))


def _inproj(h, w_u, w_dt, dt_bias, q_gain, k_gain):
    assert U_Q % INP_TN == 0 and U_GA % INP_TN == 0 and U_K % INP_TN == 0 and KV_DIM * 2 == INP_TN
    n = h.shape[0]
    ndt = 2 * SSD_HEADS
    return pl.pallas_call(
        _inproj_kernel,
        grid=(n // INP_TM, U_TOTAL // INP_TN),
        in_specs=[
            pl.BlockSpec((INP_TM, D_MODEL), lambda i, j: (i, 0)),
            pl.BlockSpec((D_MODEL, INP_TN), lambda i, j: (0, j)),
            pl.BlockSpec((D_MODEL, ndt), lambda i, j: (0, 0)),
            pl.BlockSpec((1, ndt), lambda i, j: (0, 0)),
            pl.BlockSpec((1, ATT_HEAD_DIM), lambda i, j: (0, 0)),
            pl.BlockSpec((1, ATT_HEAD_DIM), lambda i, j: (0, 0)),
        ],
        out_specs=[
            pl.BlockSpec((INP_TM, INP_TN), lambda i, j: (i, j)),
            pl.BlockSpec((INP_TM, ndt), lambda i, j: (i, 0)),
        ],
        out_shape=[jax.ShapeDtypeStruct((n, U_TOTAL), BF16), jax.ShapeDtypeStruct((n, ndt), F32)],
        compiler_params=_params(("parallel", "arbitrary")),
        name="in_proj",
    )(h, w_u, w_dt, dt_bias, q_gain, k_gain)


CONV_TS = 1024
CONV_TC = 2048
CONV_HALO = 16
CONV_TB = 32
CONV_WIN = CONV_TB + 2 * CONV_HALO
CONV_SIDE_TAPS = tuple(k for k in range(CONV_K) if k != CONV_K // 2)


def _conv_kernel(xm_ref, xp_ref, xn_ref, w_ref, b_ref, o_ref, xs_scr, pw_scr, *, tiles_per_seq):
    i = pl.program_id(0)
    first = (i % tiles_per_seq) == 0
    last = ((i + 1) % tiles_per_seq) == 0
    H = CONV_HALO
    zero_halo = jnp.zeros((H, CONV_TC), BF16)
    xs_scr[0:H, :] = jnp.where(first, zero_halo, xp_ref[...])
    xs_scr[H:H + CONV_TS, :] = xm_ref[...]
    xs_scr[H + CONV_TS:, :] = jnp.where(last, zero_halo, xn_ref[...])
    xs = xs_scr[...]
    for n, k in enumerate(CONV_SIDE_TAPS):
        pw_scr[n] = xs * (0.5 * w_ref[k:k + 1, :]).astype(BF16)
    r = lax.broadcasted_iota(jnp.int32, (CONV_TB, len(CONV_SIDE_TAPS) * CONV_WIN), 0)
    c = lax.broadcasted_iota(jnp.int32, (CONV_TB, len(CONV_SIDE_TAPS) * CONV_WIN), 1)
    hit = None
    for n, k in enumerate(CONV_SIDE_TAPS):
        sel = c == r + (n * CONV_WIN + H + k - CONV_K // 2)
        hit = sel if hit is None else hit | sel
    shift = jnp.where(hit, 1.0, 0.0).astype(BF16)
    w_mid = 0.5 * w_ref[CONV_K // 2:CONV_K // 2 + 1, :]
    bias = 0.5 * b_ref[...]
    for blk in range(CONV_TS // CONV_TB):
        t0 = blk * CONV_TB
        taps = jnp.concatenate([pw_scr[n, t0:t0 + CONV_WIN, :] for n in range(len(CONV_SIDE_TAPS))], axis=0)
        mid = xs_scr[H + t0:H + t0 + CONV_TB, :].astype(F32) * w_mid + bias
        half = _dot(shift, taps) + mid
        o_ref[t0:t0 + CONV_TB, :] = (half + half * jnp.tanh(half)).astype(BF16)


def _conv(u, conv_w, conv_b, seq):
    n = u.shape[0]
    tiles_per_seq = seq // CONV_TS
    halo_per_tile = CONV_TS // CONV_HALO
    n_halo = n // CONV_HALO
    col0 = U_XBC // CONV_TC
    return pl.pallas_call(
        functools.partial(_conv_kernel, tiles_per_seq=tiles_per_seq),
        grid=(n // CONV_TS, CONV_DIM // CONV_TC),
        in_specs=[
            pl.BlockSpec((CONV_TS, CONV_TC), lambda i, j: (i, col0 + j)),
            pl.BlockSpec((CONV_HALO, CONV_TC),
                         lambda i, j: (jnp.maximum(i * halo_per_tile - 1, 0), col0 + j)),
            pl.BlockSpec((CONV_HALO, CONV_TC),
                         lambda i, j: (jnp.minimum((i + 1) * halo_per_tile, n_halo - 1), col0 + j)),
            pl.BlockSpec((CONV_K, CONV_TC), lambda i, j: (0, j)),
            pl.BlockSpec((1, CONV_TC), lambda i, j: (0, j)),
        ],
        out_specs=pl.BlockSpec((CONV_TS, CONV_TC), lambda i, j: (i, j)),
        out_shape=jax.ShapeDtypeStruct((n, CONV_DIM), BF16),
        scratch_shapes=[pltpu.VMEM((CONV_TS + 2 * CONV_HALO, CONV_TC), BF16),
                        pltpu.VMEM((len(CONV_SIDE_TAPS), CONV_TS + 2 * CONV_HALO, CONV_TC), BF16)],
        compiler_params=_params(("parallel", "parallel")),
        name="conv",
    )(u, u, u, conv_w, conv_b)


def _split3(a):
    hi = a.astype(BF16)
    r1 = a - hi.astype(F32)
    mid = r1.astype(BF16)
    lo = (r1 - mid.astype(F32)).astype(BF16)
    return hi, mid, lo


def _exact_lhs01(m01, a):
    hi, mid, lo = _split3(a)
    return _dot(jnp.concatenate([m01, m01, m01], axis=1), jnp.concatenate([hi, mid, lo], axis=0))


def _split2(a):
    hi = a.astype(BF16)
    mid = (a - hi.astype(F32)).astype(BF16)
    return jnp.concatenate([hi, mid], axis=1)


PREP_CHUNKS = 8
ROWS_PER_GROUP = 4 * SSD_HPG
LOG2E = 1.4426950408889634


def _ssd_prep_kernel(dt_ref, a_ref, cum_ref, wdec_ref, ecum_ref, rows_ref):
    L = CHUNK
    H = SSD_HPG
    row = lax.broadcasted_iota(jnp.int32, (L, L), 0)
    col = lax.broadcasted_iota(jnp.int32, (L, L), 1)
    lower = (row >= col).astype(BF16)
    upper = (row <= col).astype(BF16)
    is_fwd = col < SSD_HEADS
    a = -jnp.exp(a_ref[...])
    for t in range(PREP_CHUNKS):
        sl = slice(t * L, (t + 1) * L)
        dt = dt_ref[sl, :]
        dA = dt * a
        pre = _exact_lhs01(lower, dA)
        suf = _exact_lhs01(upper, dA)
        cum = jnp.where(is_fwd, pre, suf)
        tot = jnp.where(is_fwd[:1], pre[L - 1:L, :], suf[0:1, :])
        cum_ref[sl, :] = cum * LOG2E
        wdec_ref[sl, :] = _split2(jnp.exp(tot - cum) * dt)
        ecum_ref[sl, :] = _split2(jnp.exp(cum))
        dt_t = dt.T
        src_t = (cum.T - jnp.log(dt_t)) * LOG2E
        lds_t = jnp.log2(dt_t[:SSD_HEADS] + dt_t[SSD_HEADS:])
        etot = jnp.broadcast_to(jnp.exp(tot), (H, L))
        for g in range(SSD_GROUPS):
            f = slice(g * H, (g + 1) * H)
            b = slice(SSD_HEADS + g * H, SSD_HEADS + (g + 1) * H)
            rows_ref[t, g] = jnp.concatenate([src_t[f], src_t[b], lds_t[f], etot], axis=0)


def _ssd_prep(dt, a_log):
    n = dt.shape[0]
    rows_blk = PREP_CHUNKS * CHUNK
    nh = 2 * SSD_HEADS
    return pl.pallas_call(
        _ssd_prep_kernel,
        grid=(n // rows_blk,),
        in_specs=[pl.BlockSpec((rows_blk, nh), lambda i: (i, 0)), pl.BlockSpec((1, nh), lambda i: (0, 0))],
        out_specs=[pl.BlockSpec((rows_blk, nh), lambda i: (i, 0)),
                   pl.BlockSpec((rows_blk, 2 * nh), lambda i: (i, 0)),
                   pl.BlockSpec((rows_blk, 2 * nh), lambda i: (i, 0)),
                   pl.BlockSpec((PREP_CHUNKS, SSD_GROUPS, ROWS_PER_GROUP, CHUNK), lambda i: (i, 0, 0, 0))],
        out_shape=[jax.ShapeDtypeStruct((n, nh), F32),
                   jax.ShapeDtypeStruct((n, 2 * nh), BF16),
                   jax.ShapeDtypeStruct((n, 2 * nh), BF16),
                   jax.ShapeDtypeStruct((n // CHUNK, SSD_GROUPS, ROWS_PER_GROUP, CHUNK), F32)],
        compiler_params=_params(("parallel",)),
        name="ssd_prep",
    )(dt, a_log)


SSD_T = 16


def _ssd_kernel(x_ref, b_ref, c_ref, z_ref, cum_ref, wdec_ref, ecum_ref, rows_ref, d_ref, e_ref,
                y_ref, gst_scr, h_scr):
    g = pl.program_id(1)
    phase = pl.program_id(2)
    step = pl.program_id(3)
    ns = pl.num_programs(3)
    L = CHUNK
    GX = GROUP_X
    H = SSD_HPG

    @pl.when(step == 0)
    def _():
        h_scr[...] = jnp.zeros_like(h_scr)

    def state_update(t, direction, xb, bm):
        sl = slice(t * L, (t + 1) * L)
        e_dir = e_ref[0, :, direction * GX:(direction + 1) * GX]
        w = _dot(wdec_ref[sl, :], e_dir)
        dec = _dot(_split2(rows_ref[t, 0, 3 * H:4 * H, :]), e_dir)[0:1, :]
        xdec = w.astype(BF16) * xb
        b_t = bm.astype(F32).T.astype(BF16)
        h_scr[...] = dec * h_scr[...] + _dot(b_t, xdec)

    @pl.when(phase == 0)
    def _():
        for t in reversed(range(SSD_T)):
            sl = slice(t * L, (t + 1) * L)
            gst_scr[(ns - 1 - step) * SSD_T + t] = h_scr[...].astype(BF16)
            state_update(t, 1, x_ref[sl, :], b_ref[sl, :])

    @pl.when(phase == 1)
    def _():
        row = lax.broadcasted_iota(jnp.int32, (L, L), 0)
        col = lax.broadcasted_iota(jnp.int32, (L, L), 1)
        lower = row >= col
        diag = row == col
        low_lane = col < SSD_HEADDIM
        shift = (2 * SSD_HEADS - H * g) % (2 * SSD_HEADS)
        for t in range(SSD_T):
            sl = slice(t * L, (t + 1) * L)
            cm = c_ref[sl, :]
            bm = b_ref[sl, :]
            xb = x_ref[sl, :]
            xf = xb.astype(F32)
            cb = lax.dot_general(cm, bm, (((1,), (1,)), ((), ())), preferred_element_type=F32)
            cum_r = pltpu.roll(cum_ref[sl, :], shift, axis=1)
            pairs = []
            for k in range(H // 2):
                ms = []
                for half in range(2):
                    hh = 2 * k + half
                    fwd = cum_r[:, hh:hh + 1] - rows_ref[t, 0, hh:hh + 1, :]
                    bwd = cum_r[:, SSD_HEADS + hh:SSD_HEADS + hh + 1] - rows_ref[t, 0, H + hh:H + hh + 1, :]
                    arg = jnp.where(diag, rows_ref[t, 0, 2 * H + hh:2 * H + hh + 1, :],
                                    jnp.where(lower, fwd, bwd))
                    ms.append((jnp.exp2(arg) * cb).astype(BF16))
                xp = xb[:, k * L:(k + 1) * L]
                zero = jnp.zeros_like(xp)
                rhs = jnp.concatenate([jnp.where(low_lane, xp, zero), jnp.where(low_lane, zero, xp)], axis=0)
                pairs.append(_dot(jnp.concatenate(ms, axis=1), rhs))
            y = jnp.concatenate(pairs, axis=1)
            c = step * SSD_T + t
            states = jnp.concatenate([h_scr[...].astype(BF16), gst_scr[c]], axis=1)
            y_off = _dot(ecum_ref[sl, :], e_ref[0]) * _dot(cm, states)
            y = y + y_off[:, :GX] + y_off[:, GX:] + d_ref[...] * xf
            y_ref[sl, :] = (y * _silu(z_ref[sl, :].astype(F32))).astype(BF16)
            state_update(t, 0, xb, bm)


def _ssd(xc, bc, u, cum, wdec, ecum, rows, d_exp, e01, batch, seq):
    n = xc.shape[0]
    nc = seq // CHUNK
    ns = nc // SSD_T
    tl = SSD_T * CHUNK
    nh = 2 * SSD_HEADS
    gx_blocks = D_INNER // D_STATE

    def fb_row(b, p, s):
        return b * ns + jnp.where(p == 0, ns - 1 - s, s)

    def f_row(b, p, s):
        return b * ns + jnp.where(p == 0, 0, s)

    return pl.pallas_call(
        _ssd_kernel,
        grid=(batch, SSD_GROUPS, 2, ns),
        in_specs=[
            pl.BlockSpec((tl, GROUP_X), lambda b, g, p, s: (fb_row(b, p, s), g)),
            pl.BlockSpec((None, tl, D_STATE), lambda b, g, p, s: (g, fb_row(b, p, s), 0)),
            pl.BlockSpec((None, tl, D_STATE), lambda b, g, p, s: (SSD_GROUPS + g, f_row(b, p, s), 0)),
            pl.BlockSpec((tl, GROUP_X), lambda b, g, p, s: (f_row(b, p, s), U_Z // GROUP_X + g)),
            pl.BlockSpec((tl, nh), lambda b, g, p, s: (f_row(b, p, s), 0)),
            pl.BlockSpec((tl, 2 * nh), lambda b, g, p, s: (fb_row(b, p, s), 0)),
            pl.BlockSpec((tl, 2 * nh), lambda b, g, p, s: (f_row(b, p, s), 0)),
            pl.BlockSpec((SSD_T, 1, ROWS_PER_GROUP, CHUNK), lambda b, g, p, s: (fb_row(b, p, s), g, 0, 0)),
            pl.BlockSpec((1, GROUP_X), lambda b, g, p, s: (0, g)),
            pl.BlockSpec((1, 2 * nh, 2 * GROUP_X), lambda b, g, p, s: (g, 0, 0)),
        ],
        out_specs=pl.BlockSpec((tl, GROUP_X), lambda b, g, p, s: (f_row(b, p, s), g)),
        out_shape=jax.ShapeDtypeStruct((n, D_INNER), BF16),
        scratch_shapes=[
            pltpu.VMEM((nc, D_STATE, GROUP_X), BF16),
            pltpu.VMEM((D_STATE, GROUP_X), F32),
        ],
        compiler_params=_params(("arbitrary", "arbitrary", "arbitrary", "arbitrary")),
        name="ssd",
    )(xc, bc, bc, u, cum, wdec, ecum, rows, d_exp, e01)


def _ssd_expand_matrix():
    e = np.zeros((SSD_GROUPS, 2 * SSD_HEADS, 2 * GROUP_X), np.float32)
    for g in range(SSD_GROUPS):
        for d in range(2):
            for r in range(SSD_HPG):
                lo = d * GROUP_X + r * SSD_HEADDIM
                e[g, d * SSD_HEADS + g * SSD_HPG + r, lo:lo + SSD_HEADDIM] = 1.0
    return np.concatenate([e, e], axis=1)


def _t5_bucket(rel):
    nb = N_BUCKETS // 2
    ret = (rel > 0).astype(np.int32) * nb
    n = np.abs(rel)
    max_exact = nb // 2
    large = max_exact + (np.log(np.maximum(n, 1) / max_exact) / math.log(MAX_DIST / max_exact)
                         * (nb - max_exact)).astype(np.int32)
    large = np.minimum(large, nb - 1)
    return (ret + np.where(n < max_exact, n, large)).astype(np.int32)


def _bucket_table():
    i = np.arange(ATT_BLOCK)[:, None]
    j = np.arange(3 * ATT_BLOCK)[None, :]
    rel = j - ATT_BLOCK - i
    return np.where(np.abs(rel) <= WINDOW, _t5_bucket(rel), -1).astype(np.int32)


N_EDGE_VARIANTS = 4


def _bias_kernel(rel_ref, bucket_ref, o_ref):
    variant = pl.program_id(0)
    bucket = bucket_ref[...]
    kcol = lax.broadcasted_iota(jnp.int32, bucket.shape, 1)
    dead = (((variant & 1) == 1) & (kcol < ATT_BLOCK)) | (((variant & 2) == 2) & (kcol >= 2 * ATT_BLOCK))
    for h in range(ATT_Q_HEADS):
        acc = jnp.full(bucket.shape, NEG_BIG, F32)
        for b in range(N_BUCKETS):
            acc = jnp.where(bucket == b, rel_ref[b, h] * LOG2E, acc)
        o_ref[0, h] = jnp.where(dead, NEG_BIG, acc)


def _bias_table(rel_bias):
    bucket = jnp.asarray(_bucket_table())
    return pl.pallas_call(
        _bias_kernel,
        grid=(N_EDGE_VARIANTS,),
        in_specs=[
            pl.BlockSpec(memory_space=pltpu.SMEM),
            pl.BlockSpec((ATT_BLOCK, 3 * ATT_BLOCK), lambda v: (0, 0)),
        ],
        out_specs=pl.BlockSpec((1, ATT_Q_HEADS, ATT_BLOCK, 3 * ATT_BLOCK), lambda v: (v, 0, 0, 0)),
        out_shape=jax.ShapeDtypeStruct((N_EDGE_VARIANTS, ATT_Q_HEADS, ATT_BLOCK, 3 * ATT_BLOCK), F32),
        compiler_params=_params(("arbitrary",)),
        name="rel_bias_table",
    )(rel_bias, bucket)


def _attn_kv_head(g, q_ref, kp_ref, kc_ref, kn_ref, vp_ref, vc_ref, vn_ref, bias_ref, sink_ref, store):
    T = ATT_BLOCK
    hd = ATT_HEAD_DIM
    ones = jnp.ones((3 * T, hd), BF16)
    ks = slice(g * hd, (g + 1) * hd)
    kwin = jnp.concatenate([kp_ref[:, ks], kc_ref[:, ks], kn_ref[:, ks]], axis=0)
    vwin = jnp.concatenate([vp_ref[:, ks], vc_ref[:, ks], vn_ref[:, ks]], axis=0)
    qs = jnp.concatenate(
        [q_ref[:, (g * ATT_REP + r) * hd:(g * ATT_REP + r + 1) * hd] for r in range(ATT_REP)],
        axis=0)
    s = lax.dot_general(qs, kwin, (((1,), (1,)), ((), ())), preferred_element_type=F32)
    s = s + jnp.concatenate([bias_ref[0, g * ATT_REP + r] for r in range(ATT_REP)], axis=0)
    sink = jnp.concatenate(
        [jnp.full((T, 1), sink_ref[g * ATT_REP + r] * LOG2E, F32) for r in range(ATT_REP)], axis=0)
    m = jnp.maximum(jnp.max(s, axis=-1, keepdims=True), sink)
    p = jnp.exp2(s - m).astype(BF16)
    ov = _dot(p, jnp.concatenate([vwin, ones], axis=1))
    o = ov[:, :hd] / (ov[:, hd:] + jnp.exp2(sink - m))
    for r in range(ATT_REP):
        store(g * ATT_REP + r, o[r * T:(r + 1) * T, :].astype(BF16))


BR_TM = 512
BR_TN = 512
BR_STEPS = D_MODEL // BR_TN
assert BR_TM == BR_STEPS * ATT_BLOCK


def _attn_branch_kernel(q_ref, kp_ref, kc_ref, kn_ref, vp_ref, vc_ref, vn_ref, bias_ref, sink_ref,
                        ys_ref, ga_ref, gb_ref, ng_ref, ws_ref, wa_ref, o_ref, ya_even, ya_odd):
    i = pl.program_id(0)
    j = pl.program_id(1)
    hd = ATT_HEAD_DIM

    @pl.when((i == 0) & (j == 0))
    def _():
        ya_odd[...] = jnp.zeros_like(ya_odd)

    def step(fill_scr, use_scr):
        row0 = pl.multiple_of(j * ATT_BLOCK, ATT_BLOCK)

        def store(head, val):
            fill_scr[pl.ds(row0, ATT_BLOCK), head * hd:(head + 1) * hd] = val

        n_slices = ATT_KV_HEADS
        ka = D_INNER // n_slices
        kb = D_MODEL // n_slices
        acc = [None, None, None]

        def matmul_slice(c):
            yg = ys_ref[:, c * ka:(c + 1) * ka]
            yf = yg.astype(F32)
            pq = jnp.sum(yf * yf, axis=-1, keepdims=True)
            pa = _dot(yg * ng_ref[:, c * ka:(c + 1) * ka].astype(BF16), ws_ref[c * ka:(c + 1) * ka, :])
            pb = _dot(use_scr[:, c * kb:(c + 1) * kb], wa_ref[c * kb:(c + 1) * kb, :])
            acc[:] = (pa, pb, pq) if acc[0] is None else (acc[0] + pa, acc[1] + pb, acc[2] + pq)

        for g in range(ATT_KV_HEADS):
            _attn_kv_head(g, q_ref, kp_ref, kc_ref, kn_ref, vp_ref, vc_ref, vn_ref, bias_ref, sink_ref, store)
            matmul_slice(g)
        a, b, ssq = acc
        scale = lax.rsqrt(ssq * (1.0 / D_INNER) + EPS)
        a = jax.nn.sigmoid(ga_ref[...].astype(F32)) * (scale * a)
        b = jax.nn.sigmoid(gb_ref[...].astype(F32)) * b
        o_ref[...] = (a + b).astype(BF16)

    @pl.when(i % 2 == 0)
    def _():
        step(ya_even, ya_odd)

    @pl.when(i % 2 == 1)
    def _():
        step(ya_odd, ya_even)


def _attn_branch(ys, u, bias, sink, norm_gain, w_ssd, w_att, seq):
    n = ys.shape[0]
    nblk = n // ATT_BLOCK
    ntile = n // BR_TM
    blocks_per_seq = seq // ATT_BLOCK
    kcol = U_K // KV_DIM
    vcol = U_V // KV_DIM
    blk = lambda i, j: jnp.minimum(i * BR_STEPS + j, nblk - 1)
    prev = lambda i, j: jnp.maximum(blk(i, j) - 1, 0)
    nxt = lambda i, j: jnp.minimum(blk(i, j) + 1, nblk - 1)
    tile = lambda i: jnp.maximum(i - 1, 0)
    kv_spec = lambda rowf, c: pl.BlockSpec((ATT_BLOCK, KV_DIM), lambda i, j: (rowf(i, j), c))

    def edge_variant(i, j):
        b = blk(i, j)
        first = (b % blocks_per_seq == 0).astype(jnp.int32)
        last = ((b + 1) % blocks_per_seq == 0).astype(jnp.int32)
        return first + 2 * last

    return pl.pallas_call(
        _attn_branch_kernel,
        grid=(ntile + 1, BR_STEPS),
        in_specs=[
            pl.BlockSpec((ATT_BLOCK, D_MODEL), lambda i, j: (blk(i, j), U_Q // D_MODEL)),
            kv_spec(prev, kcol), kv_spec(blk, kcol), kv_spec(nxt, kcol),
            kv_spec(prev, vcol), kv_spec(blk, vcol), kv_spec(nxt, vcol),
            pl.BlockSpec((1, ATT_Q_HEADS, ATT_BLOCK, 3 * ATT_BLOCK), lambda i, j: (edge_variant(i, j), 0, 0, 0)),
            pl.BlockSpec(memory_space=pltpu.SMEM),
            pl.BlockSpec((BR_TM, D_INNER), lambda i, j: (tile(i), 0)),
            pl.BlockSpec((BR_TM, BR_TN), lambda i, j: (tile(i), U_GA // BR_TN + j)),
            pl.BlockSpec((BR_TM, BR_TN), lambda i, j: (tile(i), U_GB // BR_TN + j)),
            pl.BlockSpec((1, D_INNER), lambda i, j: (0, 0)),
            pl.BlockSpec((D_INNER, BR_TN), lambda i, j: (0, j)),
            pl.BlockSpec((D_MODEL, BR_TN), lambda i, j: (0, j)),
        ],
        out_specs=pl.BlockSpec((BR_TM, BR_TN), lambda i, j: (i, j)),
        out_shape=jax.ShapeDtypeStruct((n + BR_TM, D_MODEL), BF16),
        scratch_shapes=[pltpu.VMEM((BR_TM, D_MODEL), BF16), pltpu.VMEM((BR_TM, D_MODEL), BF16)],
        compiler_params=_params(("arbitrary", "arbitrary")),
        name="attn_branch",
    )(u, u, u, u, u, u, u, bias, sink, ys, u, u, norm_gain, w_ssd, w_att)


OUT_TM = 512


def _outproj_kernel(m_ref, w_ref, x_ref, o_ref):
    o_ref[...] = x_ref[...] + _dot(m_ref[...], w_ref[...])


def _outproj(merged, w_out, x1):
    assert OUT_TM == BR_TM
    n = x1.shape[0]
    return pl.pallas_call(
        _outproj_kernel,
        grid=(n // OUT_TM,),
        in_specs=[
            pl.BlockSpec((OUT_TM, D_MODEL), lambda i: (i + 1, 0)),
            pl.BlockSpec((D_MODEL, D_MODEL), lambda i: (0, 0)),
            pl.BlockSpec((OUT_TM, D_MODEL), lambda i: (i, 0)),
        ],
        out_specs=pl.BlockSpec((OUT_TM, D_MODEL), lambda i: (i, 0)),
        out_shape=jax.ShapeDtypeStruct((n, D_MODEL), F32),
        compiler_params=_params(("parallel",)),
        name="out_proj",
    )(merged, w_out, x1)


def _prepare(ffn1_norm, ffn1_w_gate, ffn1_w_up, ffn1_w_down, mix_norm, w_in, conv_w, conv_b,
             ssd_A_log, ssd_dt_bias, ssd_D, ssd_out_norm, q_norm, k_norm, attn_sink, rel_bias,
             w_branch_ssd, w_branch_attn, w_out, ffn2_norm, ffn2_w_gate, ffn2_w_up, ffn2_w_down,
             final_norm):
    l = 0
    w = w_in[l]
    o_z = 0
    o_xbc = o_z + D_INNER
    o_dt = o_xbc + CONV_DIM
    o_q = o_dt + 2 * SSD_HEADS
    o_k = o_q + D_MODEL
    o_v = o_k + KV_DIM
    o_ga = o_v + KV_DIM
    o_gb = o_ga + D_MODEL
    w_u = jnp.concatenate(
        [w[:, o_z:o_dt], w[:, o_q:o_k], w[:, o_ga:o_gb + D_MODEL], w[:, o_k:o_ga]], axis=1).astype(BF16)
    row = lambda v: v.reshape(1, -1).astype(F32)
    return dict(
        ffn1=(row(ffn1_norm[l]), ffn1_w_gate[l].astype(BF16), ffn1_w_up[l].astype(BF16),
              ffn1_w_down[l].astype(BF16), row(mix_norm[l])),
        ffn2=(row(ffn2_norm[l]), ffn2_w_gate[l].astype(BF16), ffn2_w_up[l].astype(BF16),
              ffn2_w_down[l].astype(BF16), row(final_norm[l])),
        w_u=w_u, w_dt=w[:, o_dt:o_q].astype(BF16), dt_bias=row(ssd_dt_bias[l]),
        conv_w=conv_w[l].astype(F32), conv_b=row(conv_b[l]),
        a_log=row(ssd_A_log[l]), d_exp=row(jnp.repeat(ssd_D[l], SSD_HEADDIM)),
        e01=jnp.asarray(_ssd_expand_matrix(), BF16), out_norm=row(ssd_out_norm[l]),
        q_gain=row(q_norm[l]), k_gain=row(k_norm[l]), sink=attn_sink[l].astype(F32),
        bias=_bias_table(rel_bias.astype(F32)),
        w_ssd=w_branch_ssd[l].astype(BF16), w_att=w_branch_attn[l].astype(BF16),
        w_out=w_out[l].astype(BF16),
    )


def _layer(x, p):
    batch, seq, _ = x.shape
    x2d = x.reshape(batch * seq, D_MODEL)
    x1, h = _ffn(x2d, *p["ffn1"], final=False)
    u, dt = _inproj(h, p["w_u"], p["w_dt"], p["dt_bias"], p["q_gain"], p["k_gain"])
    xc = _conv(u, p["conv_w"], p["conv_b"], seq)
    cum, wdec, ecum, rows = _ssd_prep(dt, p["a_log"])
    bc = xc[:, D_INNER:].reshape(batch * seq, 2 * SSD_GROUPS, D_STATE).transpose(1, 0, 2)
    ys = _ssd(xc, bc, u, cum, wdec, ecum, rows, p["d_exp"], p["e01"], batch, seq)
    merged = _attn_branch(ys, u, p["bias"], p["sink"], p["out_norm"], p["w_ssd"], p["w_att"], seq)
    x2 = _outproj(merged, p["w_out"], x1)
    (y,) = _ffn(x2, *p["ffn2"], final=True)
    return y.reshape(batch, seq, D_MODEL)


def kernel(x_prompt, x_sample, ffn1_norm, ffn1_w_gate, ffn1_w_up, ffn1_w_down, mix_norm, w_in, conv_w, conv_b, ssd_A_log, ssd_dt_bias, ssd_D, ssd_out_norm, q_norm, k_norm, attn_sink, rel_bias, w_branch_ssd, w_branch_attn, w_out, ffn2_norm, ffn2_w_gate, ffn2_w_up, ffn2_w_down, final_norm):
    p = _prepare(ffn1_norm, ffn1_w_gate, ffn1_w_up, ffn1_w_down, mix_norm, w_in, conv_w, conv_b,
                 ssd_A_log, ssd_dt_bias, ssd_D, ssd_out_norm, q_norm, k_norm, attn_sink, rel_bias,
                 w_branch_ssd, w_branch_attn, w_out, ffn2_norm, ffn2_w_gate, ffn2_w_up, ffn2_w_down,
                 final_norm)
    return (_layer(x_prompt, p), _layer(x_sample, p))
```

```python
import functools
import math

import jax
import jax.numpy as jnp
import numpy as np
from jax import lax
from jax.experimental import pallas as pl
from jax.experimental.pallas import tpu as pltpu

D_MODEL = 2048
D_INNER = 2 * D_MODEL
SSD_HEADDIM = 64
SSD_HEADS = D_INNER // SSD_HEADDIM
SSD_GROUPS = 8
SSD_HPG = SSD_HEADS // SSD_GROUPS
D_STATE = 128
CONV_K = 5
CHUNK = 128
CONV_DIM = D_INNER + 2 * SSD_GROUPS * D_STATE
ATT_HEAD_DIM = 128
ATT_Q_HEADS = D_MODEL // ATT_HEAD_DIM
ATT_KV_HEADS = 4
ATT_REP = ATT_Q_HEADS // ATT_KV_HEADS
WINDOW = 128
ATT_BLOCK = 128
N_BUCKETS = 32
MAX_DIST = 128
D_FF = ((8 * D_MODEL // 3 + 255) // 256) * 256
EPS = 1e-6
GROUP_X = SSD_HPG * SSD_HEADDIM
KV_DIM = ATT_KV_HEADS * ATT_HEAD_DIM

U_Z = 0
U_XBC = U_Z + D_INNER
U_Q = U_XBC + CONV_DIM
U_GA = U_Q + D_MODEL
U_GB = U_GA + D_MODEL
U_K = U_GB + D_MODEL
U_V = U_K + KV_DIM
U_TOTAL = U_V + KV_DIM

V7X_VMEM_LIMIT_BYTES = 56 * 1024 * 1024
NEG_BIG = -1e30

BF16 = jnp.bfloat16
F32 = jnp.float32


def _params(semantics):
    return pltpu.CompilerParams(dimension_semantics=semantics,
                                vmem_limit_bytes=V7X_VMEM_LIMIT_BYTES)


def _rms(x, gain):
    return x * lax.rsqrt(jnp.mean(x * x, axis=-1, keepdims=True) + EPS) * gain


def _silu(x):
    return x * jax.nn.sigmoid(x)


def _dot(a, b):
    return jnp.dot(a, b, preferred_element_type=F32)


FFN_TM = 512
FFN_TF = 512


def _ffn_kernel(x_ref, g_ref, wg_ref, wu_ref, wd_ref, pg_ref, *refs, final):
    if final:
        y_ref, h_scr, acc_scr = refs
    else:
        x1_ref, h_ref, h_scr, acc_scr = refs
    j = pl.program_id(1)

    @pl.when(j == 0)
    def _():
        h_scr[...] = _rms(x_ref[...], g_ref[...]).astype(BF16)
        acc_scr[...] = jnp.zeros_like(acc_scr)

    h = h_scr[...]
    t = _silu(_dot(h, wg_ref[...])) * _dot(h, wu_ref[...])
    acc_scr[...] += _dot(t.astype(BF16), wd_ref[...])

    @pl.when(j == pl.num_programs(1) - 1)
    def _():
        x1 = x_ref[...] + 0.5 * acc_scr[...]
        normed = _rms(x1, pg_ref[...])
        if final:
            y_ref[...] = normed
        else:
            x1_ref[...] = x1
            h_ref[...] = normed.astype(BF16)


def _ffn(x, gain, wg, wu, wd, post_gain, final):
    n = x.shape[0]
    row = lambda i, j: (i, 0)
    fixed = lambda i, j: (0, 0)
    out_shape = [jax.ShapeDtypeStruct((n, D_MODEL), F32)]
    out_specs = [pl.BlockSpec((FFN_TM, D_MODEL), row)]
    if not final:
        out_shape.append(jax.ShapeDtypeStruct((n, D_MODEL), BF16))
        out_specs.append(pl.BlockSpec((FFN_TM, D_MODEL), row))
    return pl.pallas_call(
        functools.partial(_ffn_kernel, final=final),
        grid=(n // FFN_TM, D_FF // FFN_TF),
        in_specs=[
            pl.BlockSpec((FFN_TM, D_MODEL), row),
            pl.BlockSpec((1, D_MODEL), fixed),
            pl.BlockSpec((D_MODEL, FFN_TF), lambda i, j: (0, j)),
            pl.BlockSpec((D_MODEL, FFN_TF), lambda i, j: (0, j)),
            pl.BlockSpec((FFN_TF, D_MODEL), lambda i, j: (j, 0)),
            pl.BlockSpec((1, D_MODEL), fixed),
        ],
        out_specs=out_specs,
        out_shape=out_shape,
        scratch_shapes=[pltpu.VMEM((FFN_TM, D_MODEL), BF16), pltpu.VMEM((FFN_TM, D_MODEL), F32)],
        compiler_params=_params(("parallel", "arbitrary")),
        name="ffn_final" if final else "ffn",
    )(x, gain, wg, wu, wd, post_gain)


INP_TM = 2048
INP_TN = 1024


def _inproj_kernel(h_ref, w_ref, wdt_ref, bdt_ref, qg_ref, kg_ref, u_ref, dt_ref):
    h = h_ref[...]
    acc = _dot(h, w_ref[...])
    j = pl.program_id(1)
    hd = ATT_HEAD_DIM
    heads_per_tile = INP_TN // hd
    is_q = (j >= U_Q // INP_TN) & (j < U_GA // INP_TN)
    is_kv = j == U_K // INP_TN

    u_ref[...] = acc.astype(BF16)

    def norm_heads(gain, n_normed):
        for hh in range(n_normed):
            cols = slice(hh * hd, (hh + 1) * hd)
            u_ref[:, cols] = _rms(u_ref[:, cols].astype(F32), gain).astype(BF16)

    @pl.when(is_q)
    def _():
        norm_heads(qg_ref[...] * (hd ** -0.5 * LOG2E), heads_per_tile)

    @pl.when(is_kv)
    def _():
        norm_heads(kg_ref[...], ATT_KV_HEADS)

    @pl.when(j == 0)
    def _():
        raw = _dot(h, wdt_ref[...]) + bdt_ref[...]
        dt_ref[...] = jnp.maximum(raw, 0.0) + jnp.log1p(jnp.exp(-jnp.abs(raw)))


def _inproj(h, w_u, w_dt, dt_bias, q_gain, k_gain):
    assert U_Q % INP_TN == 0 and U_GA % INP_TN == 0 and U_K % INP_TN == 0 and KV_DIM * 2 == INP_TN
    n = h.shape[0]
    ndt = 2 * SSD_HEADS
    return pl.pallas_call(
        _inproj_kernel,
        grid=(n // INP_TM, U_TOTAL // INP_TN),
        in_specs=[
            pl.BlockSpec((INP_TM, D_MODEL), lambda i, j: (i, 0)),
            pl.BlockSpec((D_MODEL, INP_TN), lambda i, j: (0, j)),
            pl.BlockSpec((D_MODEL, ndt), lambda i, j: (0, 0)),
            pl.BlockSpec((1, ndt), lambda i, j: (0, 0)),
            pl.BlockSpec((1, ATT_HEAD_DIM), lambda i, j: (0, 0)),
            pl.BlockSpec((1, ATT_HEAD_DIM), lambda i, j: (0, 0)),
        ],
        out_specs=[
            pl.BlockSpec((INP_TM, INP_TN), lambda i, j: (i, j)),
            pl.BlockSpec((INP_TM, ndt), lambda i, j: (i, 0)),
        ],
        out_shape=[jax.ShapeDtypeStruct((n, U_TOTAL), BF16), jax.ShapeDtypeStruct((n, ndt), F32)],
        compiler_params=_params(("parallel", "arbitrary")),
        name="in_proj",
    )(h, w_u, w_dt, dt_bias, q_gain, k_gain)


CONV_TS = 1024
CONV_TC = 2048
CONV_HALO = 16
CONV_TB = 32
CONV_WIN = CONV_TB + 2 * CONV_HALO
CONV_SIDE_TAPS = tuple(k for k in range(CONV_K) if k != CONV_K // 2)


def _conv_kernel(xm_ref, xp_ref, xn_ref, w_ref, b_ref, o_ref, xs_scr, pw_scr, *, tiles_per_seq):
    i = pl.program_id(0)
    first = (i % tiles_per_seq) == 0
    last = ((i + 1) % tiles_per_seq) == 0
    H = CONV_HALO
    zero_halo = jnp.zeros((H, CONV_TC), BF16)
    xs_scr[0:H, :] = jnp.where(first, zero_halo, xp_ref[...])
    xs_scr[H:H + CONV_TS, :] = xm_ref[...]
    xs_scr[H + CONV_TS:, :] = jnp.where(last, zero_halo, xn_ref[...])
    xs = xs_scr[...]
    for n, k in enumerate(CONV_SIDE_TAPS):
        pw_scr[n] = xs * (0.5 * w_ref[k:k + 1, :]).astype(BF16)
    r = lax.broadcasted_iota(jnp.int32, (CONV_TB, len(CONV_SIDE_TAPS) * CONV_WIN), 0)
    c = lax.broadcasted_iota(jnp.int32, (CONV_TB, len(CONV_SIDE_TAPS) * CONV_WIN), 1)
    hit = None
    for n, k in enumerate(CONV_SIDE_TAPS):
        sel = c == r + (n * CONV_WIN + H + k - CONV_K // 2)
        hit = sel if hit is None else hit | sel
    shift = jnp.where(hit, 1.0, 0.0).astype(BF16)
    w_mid = 0.5 * w_ref[CONV_K // 2:CONV_K // 2 + 1, :]
    bias = 0.5 * b_ref[...]
    for blk in range(CONV_TS // CONV_TB):
        t0 = blk * CONV_TB
        taps = jnp.concatenate([pw_scr[n, t0:t0 + CONV_WIN, :] for n in range(len(CONV_SIDE_TAPS))], axis=0)
        mid = xs_scr[H + t0:H + t0 + CONV_TB, :].astype(F32) * w_mid + bias
        half = _dot(shift, taps) + mid
        o_ref[t0:t0 + CONV_TB, :] = (half + half * jnp.tanh(half)).astype(BF16)


def _conv(u, conv_w, conv_b, seq):
    n = u.shape[0]
    tiles_per_seq = seq // CONV_TS
    halo_per_tile = CONV_TS // CONV_HALO
    n_halo = n // CONV_HALO
    col0 = U_XBC // CONV_TC
    return pl.pallas_call(
        functools.partial(_conv_kernel, tiles_per_seq=tiles_per_seq),
        grid=(n // CONV_TS, CONV_DIM // CONV_TC),
        in_specs=[
            pl.BlockSpec((CONV_TS, CONV_TC), lambda i, j: (i, col0 + j)),
            pl.BlockSpec((CONV_HALO, CONV_TC),
                         lambda i, j: (jnp.maximum(i * halo_per_tile - 1, 0), col0 + j)),
            pl.BlockSpec((CONV_HALO, CONV_TC),
                         lambda i, j: (jnp.minimum((i + 1) * halo_per_tile, n_halo - 1), col0 + j)),
            pl.BlockSpec((CONV_K, CONV_TC), lambda i, j: (0, j)),
            pl.BlockSpec((1, CONV_TC), lambda i, j: (0, j)),
        ],
        out_specs=pl.BlockSpec((CONV_TS, CONV_TC), lambda i, j: (i, j)),
        out_shape=jax.ShapeDtypeStruct((n, CONV_DIM), BF16),
        scratch_shapes=[pltpu.VMEM((CONV_TS + 2 * CONV_HALO, CONV_TC), BF16),
                        pltpu.VMEM((len(CONV_SIDE_TAPS), CONV_TS + 2 * CONV_HALO, CONV_TC), BF16)],
        compiler_params=_params(("parallel", "parallel")),
        name="conv",
    )(u, u, u, conv_w, conv_b)


def _split3(a):
    hi = a.astype(BF16)
    r1 = a - hi.astype(F32)
    mid = r1.astype(BF16)
    lo = (r1 - mid.astype(F32)).astype(BF16)
    return hi, mid, lo


def _exact_lhs01(m01, a):
    hi, mid, lo = _split3(a)
    return _dot(jnp.concatenate([m01, m01, m01], axis=1), jnp.concatenate([hi, mid, lo], axis=0))


def _split2(a):
    hi = a.astype(BF16)
    mid = (a - hi.astype(F32)).astype(BF16)
    return jnp.concatenate([hi, mid], axis=1)


PREP_CHUNKS = 8
ROWS_PER_GROUP = 4 * SSD_HPG
LOG2E = 1.4426950408889634


def _ssd_prep_kernel(dt_ref, a_ref, cum_ref, wdec_ref, ecum_ref, rows_ref):
    L = CHUNK
    H = SSD_HPG
    row = lax.broadcasted_iota(jnp.int32, (L, L), 0)
    col = lax.broadcasted_iota(jnp.int32, (L, L), 1)
    lower = (row >= col).astype(BF16)
    upper = (row <= col).astype(BF16)
    is_fwd = col < SSD_HEADS
    a = -jnp.exp(a_ref[...])
    for t in range(PREP_CHUNKS):
        sl = slice(t * L, (t + 1) * L)
        dt = dt_ref[sl, :]
        dA = dt * a
        pre = _exact_lhs01(lower, dA)
        suf = _exact_lhs01(upper, dA)
        cum = jnp.where(is_fwd, pre, suf)
        tot = jnp.where(is_fwd[:1], pre[L - 1:L, :], suf[0:1, :])
        cum_ref[sl, :] = cum * LOG2E
        wdec_ref[sl, :] = _split2(jnp.exp(tot - cum) * dt)
        ecum_ref[sl, :] = _split2(jnp.exp(cum))
        dt_t = dt.T
        src_t = (cum.T - jnp.log(dt_t)) * LOG2E
        lds_t = jnp.log2(dt_t[:SSD_HEADS] + dt_t[SSD_HEADS:])
        etot = jnp.broadcast_to(jnp.exp(tot), (H, L))
        for g in range(SSD_GROUPS):
            f = slice(g * H, (g + 1) * H)
            b = slice(SSD_HEADS + g * H, SSD_HEADS + (g + 1) * H)
            rows_ref[t, g] = jnp.concatenate([src_t[f], src_t[b], lds_t[f], etot], axis=0)


def _ssd_prep(dt, a_log):
    n = dt.shape[0]
    rows_blk = PREP_CHUNKS * CHUNK
    nh = 2 * SSD_HEADS
    return pl.pallas_call(
        _ssd_prep_kernel,
        grid=(n // rows_blk,),
        in_specs=[pl.BlockSpec((rows_blk, nh), lambda i: (i, 0)), pl.BlockSpec((1, nh), lambda i: (0, 0))],
        out_specs=[pl.BlockSpec((rows_blk, nh), lambda i: (i, 0)),
                   pl.BlockSpec((rows_blk, 2 * nh), lambda i: (i, 0)),
                   pl.BlockSpec((rows_blk, 2 * nh), lambda i: (i, 0)),
                   pl.BlockSpec((PREP_CHUNKS, SSD_GROUPS, ROWS_PER_GROUP, CHUNK), lambda i: (i, 0, 0, 0))],
        out_shape=[jax.ShapeDtypeStruct((n, nh), F32),
                   jax.ShapeDtypeStruct((n, 2 * nh), BF16),
                   jax.ShapeDtypeStruct((n, 2 * nh), BF16),
                   jax.ShapeDtypeStruct((n // CHUNK, SSD_GROUPS, ROWS_PER_GROUP, CHUNK), F32)],
        compiler_params=_params(("parallel",)),
        name="ssd_prep",
    )(dt, a_log)


SSD_T = 16


def _ssd_kernel(x_ref, b_ref, c_ref, z_ref, cum_ref, wdec_ref, ecum_ref, rows_ref, d_ref, e_ref,
                y_ref, gst_scr, h_scr):
    g = pl.program_id(1)
    phase = pl.program_id(2)
    step = pl.program_id(3)
    ns = pl.num_programs(3)
    L = CHUNK
    GX = GROUP_X
    H = SSD_HPG

    @pl.when(step == 0)
    def _():
        h_scr[...] = jnp.zeros_like(h_scr)

    def chunk_rows(t):
        return slice(t * L, (t + 1) * L)

    def state_update(t, direction, xb, bm):
        sl = chunk_rows(t)
        e_dir = e_ref[0, :, direction * GX:(direction + 1) * GX]
        w = _dot(wdec_ref[sl, :], e_dir)
        dec = _dot(_split2(rows_ref[t, 0, 3 * H:4 * H, :]), e_dir)[0:1, :]
        xdec = w.astype(BF16) * xb
        b_t = bm.astype(F32).T.astype(BF16)
        h_scr[...] = dec * h_scr[...] + _dot(b_t, xdec)

    @pl.when(phase == 0)
    def _():
        def chunk(t):
            sl = chunk_rows(t)
            gst_scr[(ns - 1 - step) * SSD_T + t] = h_scr[...].astype(BF16)
            state_update(t, 1, x_ref[sl, :], b_ref[sl, :])

        for t in reversed(range(SSD_T)):
            chunk(t)

    @pl.when(phase == 1)
    def _():
        row = lax.broadcasted_iota(jnp.int32, (L, L), 0)
        col = lax.broadcasted_iota(jnp.int32, (L, L), 1)
        lower = row >= col
        diag = row == col
        low_lane = col < SSD_HEADDIM
        shift = (2 * SSD_HEADS - H * g) % (2 * SSD_HEADS)
        def chunk(t):
            sl = chunk_rows(t)
            cm = c_ref[sl, :]
            bm = b_ref[sl, :]
            xb = x_ref[sl, :]
            xf = xb.astype(F32)
            cb = lax.dot_general(cm, bm, (((1,), (1,)), ((), ())), preferred_element_type=F32)
            cum_r = pltpu.roll(cum_ref[sl, :], shift, axis=1)
            pairs = []
            for k in range(H // 2):
                ms = []
                for half in range(2):
                    hh = 2 * k + half
                    fwd = cum_r[:, hh:hh + 1] - rows_ref[t, 0, hh:hh + 1, :]
                    bwd = cum_r[:, SSD_HEADS + hh:SSD_HEADS + hh + 1] - rows_ref[t, 0, H + hh:H + hh + 1, :]
                    arg = jnp.where(diag, rows_ref[t, 0, 2 * H + hh:2 * H + hh + 1, :],
                                    jnp.where(lower, fwd, bwd))
                    ms.append((jnp.exp2(arg) * cb).astype(BF16))
                xp = xb[:, k * L:(k + 1) * L]
                zero = jnp.zeros_like(xp)
                rhs = jnp.concatenate([jnp.where(low_lane, xp, zero), jnp.where(low_lane, zero, xp)], axis=0)
                pairs.append(_dot(jnp.concatenate(ms, axis=1), rhs))
            y = jnp.concatenate(pairs, axis=1)
            c = step * SSD_T + t
            states = jnp.concatenate([h_scr[...].astype(BF16), gst_scr[c]], axis=1)
            y_off = _dot(ecum_ref[sl, :], e_ref[0]) * _dot(cm, states)
            y = y + y_off[:, :GX] + y_off[:, GX:] + d_ref[...] * xf
            y_ref[sl, :] = (y * _silu(z_ref[sl, :].astype(F32))).astype(BF16)
            state_update(t, 0, xb, bm)

        for t in range(SSD_T):
            chunk(t)


def _ssd(xc, u, cum, wdec, ecum, rows, d_exp, e01, batch, seq):
    n = xc.shape[0]
    nc = seq // CHUNK
    ns = nc // SSD_T
    tl = SSD_T * CHUNK
    nh = 2 * SSD_HEADS
    gx_blocks = D_INNER // D_STATE

    def fb_row(b, p, s):
        return b * ns + jnp.where(p == 0, ns - 1 - s, s)

    def f_row(b, p, s):
        return b * ns + jnp.where(p == 0, 0, s)

    return pl.pallas_call(
        _ssd_kernel,
        grid=(batch, SSD_GROUPS, 2, ns),
        in_specs=[
            pl.BlockSpec((tl, GROUP_X), lambda b, g, p, s: (fb_row(b, p, s), g)),
            pl.BlockSpec((tl, D_STATE), lambda b, g, p, s: (fb_row(b, p, s), gx_blocks + g)),
            pl.BlockSpec((tl, D_STATE), lambda b, g, p, s: (f_row(b, p, s), gx_blocks + SSD_GROUPS + g)),
            pl.BlockSpec((tl, GROUP_X), lambda b, g, p, s: (f_row(b, p, s), U_Z // GROUP_X + g)),
            pl.BlockSpec((tl, nh), lambda b, g, p, s: (f_row(b, p, s), 0)),
            pl.BlockSpec((tl, 2 * nh), lambda b, g, p, s: (fb_row(b, p, s), 0)),
            pl.BlockSpec((tl, 2 * nh), lambda b, g, p, s: (f_row(b, p, s), 0)),
            pl.BlockSpec((SSD_T, 1, ROWS_PER_GROUP, CHUNK), lambda b, g, p, s: (fb_row(b, p, s), g, 0, 0)),
            pl.BlockSpec((1, GROUP_X), lambda b, g, p, s: (0, g)),
            pl.BlockSpec((1, 2 * nh, 2 * GROUP_X), lambda b, g, p, s: (g, 0, 0)),
        ],
        out_specs=pl.BlockSpec((tl, GROUP_X), lambda b, g, p, s: (f_row(b, p, s), g)),
        out_shape=jax.ShapeDtypeStruct((n, D_INNER), BF16),
        scratch_shapes=[
            pltpu.VMEM((nc, D_STATE, GROUP_X), BF16),
            pltpu.VMEM((D_STATE, GROUP_X), F32),
        ],
        compiler_params=_params(("arbitrary", "arbitrary", "arbitrary", "arbitrary")),
        name="ssd",
    )(xc, xc, xc, u, cum, wdec, ecum, rows, d_exp, e01)


def _ssd_expand_matrix():
    e = np.zeros((SSD_GROUPS, 2 * SSD_HEADS, 2 * GROUP_X), np.float32)
    for g in range(SSD_GROUPS):
        for d in range(2):
            for r in range(SSD_HPG):
                lo = d * GROUP_X + r * SSD_HEADDIM
                e[g, d * SSD_HEADS + g * SSD_HPG + r, lo:lo + SSD_HEADDIM] = 1.0
    return np.concatenate([e, e], axis=1)


def _t5_bucket(rel):
    nb = N_BUCKETS // 2
    ret = (rel > 0).astype(np.int32) * nb
    n = np.abs(rel)
    max_exact = nb // 2
    large = max_exact + (np.log(np.maximum(n, 1) / max_exact) / math.log(MAX_DIST / max_exact)
                         * (nb - max_exact)).astype(np.int32)
    large = np.minimum(large, nb - 1)
    return (ret + np.where(n < max_exact, n, large)).astype(np.int32)


def _bucket_table():
    i = np.arange(ATT_BLOCK)[:, None]
    j = np.arange(3 * ATT_BLOCK)[None, :]
    rel = j - ATT_BLOCK - i
    return np.where(np.abs(rel) <= WINDOW, _t5_bucket(rel), -1).astype(np.int32)


N_EDGE_VARIANTS = 4


def _bias_kernel(rel_ref, bucket_ref, o_ref):
    variant = pl.program_id(0)
    bucket = bucket_ref[...]
    kcol = lax.broadcasted_iota(jnp.int32, bucket.shape, 1)
    dead = (((variant & 1) == 1) & (kcol < ATT_BLOCK)) | (((variant & 2) == 2) & (kcol >= 2 * ATT_BLOCK))
    for h in range(ATT_Q_HEADS):
        acc = jnp.full(bucket.shape, NEG_BIG, F32)
        for b in range(N_BUCKETS):
            acc = jnp.where(bucket == b, rel_ref[b, h] * LOG2E, acc)
        o_ref[0, h] = jnp.where(dead, NEG_BIG, acc)


def _bias_table(rel_bias):
    bucket = jnp.asarray(_bucket_table())
    return pl.pallas_call(
        _bias_kernel,
        grid=(N_EDGE_VARIANTS,),
        in_specs=[
            pl.BlockSpec(memory_space=pltpu.SMEM),
            pl.BlockSpec((ATT_BLOCK, 3 * ATT_BLOCK), lambda v: (0, 0)),
        ],
        out_specs=pl.BlockSpec((1, ATT_Q_HEADS, ATT_BLOCK, 3 * ATT_BLOCK), lambda v: (v, 0, 0, 0)),
        out_shape=jax.ShapeDtypeStruct((N_EDGE_VARIANTS, ATT_Q_HEADS, ATT_BLOCK, 3 * ATT_BLOCK), F32),
        compiler_params=_params(("arbitrary",)),
        name="rel_bias_table",
    )(rel_bias, bucket)


def _attn_kv_head(g, q_ref, kp_ref, kc_ref, kn_ref, vp_ref, vc_ref, vn_ref, bias_ref, sink_ref, store):
    T = ATT_BLOCK
    hd = ATT_HEAD_DIM
    ones = jnp.ones((3 * T, hd), BF16)
    ks = slice(g * hd, (g + 1) * hd)
    kwin = jnp.concatenate([kp_ref[:, ks], kc_ref[:, ks], kn_ref[:, ks]], axis=0)
    vwin = jnp.concatenate([vp_ref[:, ks], vc_ref[:, ks], vn_ref[:, ks]], axis=0)
    qs = jnp.concatenate(
        [q_ref[:, (g * ATT_REP + r) * hd:(g * ATT_REP + r + 1) * hd] for r in range(ATT_REP)],
        axis=0)
    s = lax.dot_general(qs, kwin, (((1,), (1,)), ((), ())), preferred_element_type=F32)
    s = s + jnp.concatenate([bias_ref[0, g * ATT_REP + r] for r in range(ATT_REP)], axis=0)
    sink = jnp.concatenate(
        [jnp.full((T, 1), sink_ref[g * ATT_REP + r] * LOG2E, F32) for r in range(ATT_REP)], axis=0)
    m = jnp.maximum(jnp.max(s, axis=-1, keepdims=True), sink)
    p = jnp.exp2(s - m).astype(BF16)
    ov = _dot(p, jnp.concatenate([vwin, ones], axis=1))
    o = ov[:, :hd] / (ov[:, hd:] + jnp.exp2(sink - m))
    for r in range(ATT_REP):
        store(g * ATT_REP + r, o[r * T:(r + 1) * T, :].astype(BF16))


BR_TM = 512
BR_TN = 512
BR_STEPS = D_MODEL // BR_TN
assert BR_TM == BR_STEPS * ATT_BLOCK


def _attn_branch_kernel(q_ref, kp_ref, kc_ref, kn_ref, vp_ref, vc_ref, vn_ref, bias_ref, sink_ref,
                        ys_ref, ga_ref, gb_ref, ng_ref, ws_ref, wa_ref, o_ref, ya_even, ya_odd):
    i = pl.program_id(0)
    j = pl.program_id(1)
    hd = ATT_HEAD_DIM

    @pl.when((i == 0) & (j == 0))
    def _():
        ya_odd[...] = jnp.zeros_like(ya_odd)

    def step(fill_scr, use_scr):
        row0 = pl.multiple_of(j * ATT_BLOCK, ATT_BLOCK)

        def store(head, val):
            fill_scr[pl.ds(row0, ATT_BLOCK), head * hd:(head + 1) * hd] = val

        n_slices = ATT_KV_HEADS
        ka = D_INNER // n_slices
        kb = D_MODEL // n_slices
        acc = [None, None, None]

        def matmul_slice(c):
            yg = ys_ref[:, c * ka:(c + 1) * ka]
            yf = yg.astype(F32)
            pq = jnp.sum(yf * yf, axis=-1, keepdims=True)
            pa = _dot(yg * ng_ref[:, c * ka:(c + 1) * ka].astype(BF16), ws_ref[c * ka:(c + 1) * ka, :])
            acc[:] = (pa, None, pq) if acc[0] is None else (acc[0] + pa, None, acc[2] + pq)

        b = None
        for g in range(ATT_KV_HEADS):
            _attn_kv_head(g, q_ref, kp_ref, kc_ref, kn_ref, vp_ref, vc_ref, vn_ref, bias_ref, sink_ref, store)
            matmul_slice(g)
            if g == 1:
                b = _dot(use_scr[...], wa_ref[...])
        a, _, ssq = acc
        scale = lax.rsqrt(ssq * (1.0 / D_INNER) + EPS)
        a = jax.nn.sigmoid(ga_ref[...].astype(F32)) * (scale * a)
        b = jax.nn.sigmoid(gb_ref[...].astype(F32)) * b
        o_ref[...] = (a + b).astype(BF16)

    @pl.when(i % 2 == 0)
    def _():
        step(ya_even, ya_odd)

    @pl.when(i % 2 == 1)
    def _():
        step(ya_odd, ya_even)


def _attn_branch(ys, u, bias, sink, norm_gain, w_ssd, w_att, seq):
    n = ys.shape[0]
    nblk = n // ATT_BLOCK
    ntile = n // BR_TM
    blocks_per_seq = seq // ATT_BLOCK
    kcol = U_K // KV_DIM
    vcol = U_V // KV_DIM
    blk = lambda i, j: jnp.minimum(i * BR_STEPS + j, nblk - 1)
    prev = lambda i, j: jnp.maximum(blk(i, j) - 1, 0)
    nxt = lambda i, j: jnp.minimum(blk(i, j) + 1, nblk - 1)
    tile = lambda i: jnp.maximum(i - 1, 0)
    kv_spec = lambda rowf, c: pl.BlockSpec((ATT_BLOCK, KV_DIM), lambda i, j: (rowf(i, j), c))

    def edge_variant(i, j):
        b = blk(i, j)
        first = (b % blocks_per_seq == 0).astype(jnp.int32)
        last = ((b + 1) % blocks_per_seq == 0).astype(jnp.int32)
        return first + 2 * last

    return pl.pallas_call(
        _attn_branch_kernel,
        grid=(ntile + 1, BR_STEPS),
        in_specs=[
            pl.BlockSpec((ATT_BLOCK, D_MODEL), lambda i, j: (blk(i, j), U_Q // D_MODEL)),
            kv_spec(prev, kcol), kv_spec(blk, kcol), kv_spec(nxt, kcol),
            kv_spec(prev, vcol), kv_spec(blk, vcol), kv_spec(nxt, vcol),
            pl.BlockSpec((1, ATT_Q_HEADS, ATT_BLOCK, 3 * ATT_BLOCK), lambda i, j: (edge_variant(i, j), 0, 0, 0)),
            pl.BlockSpec(memory_space=pltpu.SMEM),
            pl.BlockSpec((BR_TM, D_INNER), lambda i, j: (tile(i), 0)),
            pl.BlockSpec((BR_TM, BR_TN), lambda i, j: (tile(i), U_GA // BR_TN + j)),
            pl.BlockSpec((BR_TM, BR_TN), lambda i, j: (tile(i), U_GB // BR_TN + j)),
            pl.BlockSpec((1, D_INNER), lambda i, j: (0, 0)),
            pl.BlockSpec((D_INNER, BR_TN), lambda i, j: (0, j)),
            pl.BlockSpec((D_MODEL, BR_TN), lambda i, j: (0, j)),
        ],
        out_specs=pl.BlockSpec((BR_TM, BR_TN), lambda i, j: (i, j)),
        out_shape=jax.ShapeDtypeStruct((n + BR_TM, D_MODEL), BF16),
        scratch_shapes=[pltpu.VMEM((BR_TM, D_MODEL), BF16), pltpu.VMEM((BR_TM, D_MODEL), BF16)],
        compiler_params=_params(("arbitrary", "arbitrary")),
        name="attn_branch",
    )(u, u, u, u, u, u, u, bias, sink, ys, u, u, norm_gain, w_ssd, w_att)


OUT_TM = 512


def _outproj_kernel(m_ref, w_ref, x_ref, o_ref):
    o_ref[...] = x_ref[...] + _dot(m_ref[...], w_ref[...])


def _outproj(merged, w_out, x1):
    assert OUT_TM == BR_TM
    n = x1.shape[0]
    return pl.pallas_call(
        _outproj_kernel,
        grid=(n // OUT_TM,),
        in_specs=[
            pl.BlockSpec((OUT_TM, D_MODEL), lambda i: (i + 1, 0)),
            pl.BlockSpec((D_MODEL, D_MODEL), lambda i: (0, 0)),
            pl.BlockSpec((OUT_TM, D_MODEL), lambda i: (i, 0)),
        ],
        out_specs=pl.BlockSpec((OUT_TM, D_MODEL), lambda i: (i, 0)),
        out_shape=jax.ShapeDtypeStruct((n, D_MODEL), F32),
        compiler_params=_params(("parallel",)),
        name="out_proj",
    )(merged, w_out, x1)


def _prepare(ffn1_norm, ffn1_w_gate, ffn1_w_up, ffn1_w_down, mix_norm, w_in, conv_w, conv_b,
             ssd_A_log, ssd_dt_bias, ssd_D, ssd_out_norm, q_norm, k_norm, attn_sink, rel_bias,
             w_branch_ssd, w_branch_attn, w_out, ffn2_norm, ffn2_w_gate, ffn2_w_up, ffn2_w_down,
             final_norm):
    l = 0
    w = w_in[l]
    o_z = 0
    o_xbc = o_z + D_INNER
    o_dt = o_xbc + CONV_DIM
    o_q = o_dt + 2 * SSD_HEADS
    o_k = o_q + D_MODEL
    o_v = o_k + KV_DIM
    o_ga = o_v + KV_DIM
    o_gb = o_ga + D_MODEL
    w_u = jnp.concatenate(
        [w[:, o_z:o_dt], w[:, o_q:o_k], w[:, o_ga:o_gb + D_MODEL], w[:, o_k:o_ga]], axis=1).astype(BF16)
    row = lambda v: v.reshape(1, -1).astype(F32)
    return dict(
        ffn1=(row(ffn1_norm[l]), ffn1_w_gate[l].astype(BF16), ffn1_w_up[l].astype(BF16),
              ffn1_w_down[l].astype(BF16), row(mix_norm[l])),
        ffn2=(row(ffn2_norm[l]), ffn2_w_gate[l].astype(BF16), ffn2_w_up[l].astype(BF16),
              ffn2_w_down[l].astype(BF16), row(final_norm[l])),
        w_u=w_u, w_dt=w[:, o_dt:o_q].astype(BF16), dt_bias=row(ssd_dt_bias[l]),
        conv_w=conv_w[l].astype(F32), conv_b=row(conv_b[l]),
        a_log=row(ssd_A_log[l]), d_exp=row(jnp.repeat(ssd_D[l], SSD_HEADDIM)),
        e01=jnp.asarray(_ssd_expand_matrix(), BF16), out_norm=row(ssd_out_norm[l]),
        q_gain=row(q_norm[l]), k_gain=row(k_norm[l]), sink=attn_sink[l].astype(F32),
        bias=_bias_table(rel_bias.astype(F32)),
        w_ssd=w_branch_ssd[l].astype(BF16), w_att=w_branch_attn[l].astype(BF16),
        w_out=w_out[l].astype(BF16),
    )


def _layer(x, p):
    batch, seq, _ = x.shape
    x2d = x.reshape(batch * seq, D_MODEL)
    x1, h = _ffn(x2d, *p["ffn1"], final=False)
    u, dt = _inproj(h, p["w_u"], p["w_dt"], p["dt_bias"], p["q_gain"], p["k_gain"])
    xc = _conv(u, p["conv_w"], p["conv_b"], seq)
    cum, wdec, ecum, rows = _ssd_prep(dt, p["a_log"])
    ys = _ssd(xc, u, cum, wdec, ecum, rows, p["d_exp"], p["e01"], batch, seq)
    merged = _attn_branch(ys, u, p["bias"], p["sink"], p["out_norm"], p["w_ssd"], p["w_att"], seq)
    x2 = _outproj(merged, p["w_out"], x1)
    (y,) = _ffn(x2, *p["ffn2"], final=True)
    return y.reshape(batch, seq, D_MODEL)


def kernel(x_prompt, x_sample, ffn1_norm, ffn1_w_gate, ffn1_w_up, ffn1_w_down, mix_norm, w_in, conv_w, conv_b, ssd_A_log, ssd_dt_bias, ssd_D, ssd_out_norm, q_norm, k_norm, attn_sink, rel_bias, w_branch_ssd, w_branch_attn, w_out, ffn2_norm, ffn2_w_gate, ffn2_w_up, ffn2_w_down, final_norm):
    p = _prepare(ffn1_norm, ffn1_w_gate, ffn1_w_up, ffn1_w_down, mix_norm, w_in, conv_w, conv_b,
                 ssd_A_log, ssd_dt_bias, ssd_D, ssd_out_norm, q_norm, k_norm, attn_sink, rel_bias,
                 w_branch_ssd, w_branch_attn, w_out, ffn2_norm, ffn2_w_gate, ffn2_w_up, ffn2_w_down,
                 final_norm)
    return (_layer(x_prompt, p), _layer(x_sample, p))
```

```python
import functools
import math

import jax
import jax.numpy as jnp
import numpy as np
from jax import lax
from jax.experimental import pallas as pl
from jax.experimental.pallas import tpu as pltpu

D_MODEL = 2048
D_INNER = 2 * D_MODEL
SSD_HEADDIM = 64
SSD_HEADS = D_INNER // SSD_HEADDIM
SSD_GROUPS = 8
SSD_HPG = SSD_HEADS // SSD_GROUPS
D_STATE = 128
CONV_K = 5
CHUNK = 128
CONV_DIM = D_INNER + 2 * SSD_GROUPS * D_STATE
ATT_HEAD_DIM = 128
ATT_Q_HEADS = D_MODEL // ATT_HEAD_DIM
ATT_KV_HEADS = 4
ATT_REP = ATT_Q_HEADS // ATT_KV_HEADS
WINDOW = 128
ATT_BLOCK = 128
N_BUCKETS = 32
MAX_DIST = 128
D_FF = ((8 * D_MODEL // 3 + 255) // 256) * 256
EPS = 1e-6
GROUP_X = SSD_HPG * SSD_HEADDIM
KV_DIM = ATT_KV_HEADS * ATT_HEAD_DIM

U_Z = 0
U_XBC = U_Z + D_INNER
U_Q = U_XBC + CONV_DIM
U_GA = U_Q + D_MODEL
U_GB = U_GA + D_MODEL
U_K = U_GB + D_MODEL
U_V = U_K + KV_DIM
U_TOTAL = U_V + KV_DIM

V7X_VMEM_LIMIT_BYTES = 56 * 1024 * 1024
NEG_BIG = -1e30

BF16 = jnp.bfloat16
F32 = jnp.float32


def _params(semantics):
    return pltpu.CompilerParams(dimension_semantics=semantics,
                                vmem_limit_bytes=V7X_VMEM_LIMIT_BYTES)


def _rms(x, gain):
    return x * lax.rsqrt(jnp.mean(x * x, axis=-1, keepdims=True) + EPS) * gain


def _silu(x):
    return x * jax.nn.sigmoid(x)


def _dot(a, b):
    return jnp.dot(a, b, preferred_element_type=F32)


FFN_TM = 512
FFN_TF = 512


def _ffn_kernel(x_ref, g_ref, wg_ref, wu_ref, wd_ref, pg_ref, *refs, final):
    if final:
        y_ref, h_scr, acc_scr = refs
    else:
        x1_ref, h_ref, h_scr, acc_scr = refs
    j = pl.program_id(1)

    @pl.when(j == 0)
    def _():
        h_scr[...] = _rms(x_ref[...], g_ref[...]).astype(BF16)
        acc_scr[...] = jnp.zeros_like(acc_scr)

    h = h_scr[...]
    t = _silu(_dot(h, wg_ref[...])) * _dot(h, wu_ref[...])
    acc_scr[...] += _dot(t.astype(BF16), wd_ref[...])

    @pl.when(j == pl.num_programs(1) - 1)
    def _():
        x1 = x_ref[...] + 0.5 * acc_scr[...]
        normed = _rms(x1, pg_ref[...])
        if final:
            y_ref[...] = normed
        else:
            x1_ref[...] = x1
            h_ref[...] = normed.astype(BF16)


def _ffn(x, gain, wg, wu, wd, post_gain, final):
    n = x.shape[0]
    row = lambda i, j: (i, 0)
    fixed = lambda i, j: (0, 0)
    out_shape = [jax.ShapeDtypeStruct((n, D_MODEL), F32)]
    out_specs = [pl.BlockSpec((FFN_TM, D_MODEL), row)]
    if not final:
        out_shape.append(jax.ShapeDtypeStruct((n, D_MODEL), BF16))
        out_specs.append(pl.BlockSpec((FFN_TM, D_MODEL), row))
    return pl.pallas_call(
        functools.partial(_ffn_kernel, final=final),
        grid=(n // FFN_TM, D_FF // FFN_TF),
        in_specs=[
            pl.BlockSpec((FFN_TM, D_MODEL), row),
            pl.BlockSpec((1, D_MODEL), fixed),
            pl.BlockSpec((D_MODEL, FFN_TF), lambda i, j: (0, j)),
            pl.BlockSpec((D_MODEL, FFN_TF), lambda i, j: (0, j)),
            pl.BlockSpec((FFN_TF, D_MODEL), lambda i, j: (j, 0)),
            pl.BlockSpec((1, D_MODEL), fixed),
        ],
        out_specs=out_specs,
        out_shape=out_shape,
        scratch_shapes=[pltpu.VMEM((FFN_TM, D_MODEL), BF16), pltpu.VMEM((FFN_TM, D_MODEL), F32)],
        compiler_params=_params(("parallel", "arbitrary")),
        name="ffn_final" if final else "ffn",
    )(x, gain, wg, wu, wd, post_gain)


INP_TM = 2048
INP_TN = 1024


def _inproj_kernel(h_ref, w_ref, wdt_ref, bdt_ref, qg_ref, kg_ref, u_ref, dt_ref):
    h = h_ref[...]
    acc = _dot(h, w_ref[...])
    j = pl.program_id(1)
    hd = ATT_HEAD_DIM
    heads_per_tile = INP_TN // hd
    is_q = (j >= U_Q // INP_TN) & (j < U_GA // INP_TN)
    is_kv = j == U_K // INP_TN

    u_ref[...] = acc.astype(BF16)

    def norm_heads(gain, n_normed):
        for hh in range(n_normed):
            cols = slice(hh * hd, (hh + 1) * hd)
            u_ref[:, cols] = _rms(u_ref[:, cols].astype(F32), gain).astype(BF16)

    @pl.when(is_q)
    def _():
        norm_heads(qg_ref[...] * (hd ** -0.5 * LOG2E), heads_per_tile)

    @pl.when(is_kv)
    def _():
        norm_heads(kg_ref[...], ATT_KV_HEADS)

    @pl.when(j == 0)
    def _():
        raw = _dot(h, wdt_ref[...]) + bdt_ref[...]
        dt_ref[...] = jnp.maximum(raw, 0.0) + jnp.log1p(jnp.exp(-jnp.abs(raw)))


def _inproj(h, w_u, w_dt, dt_bias, q_gain, k_gain):
    assert U_Q % INP_TN == 0 and U_GA % INP_TN == 0 and U_K % INP_TN == 0 and KV_DIM * 2 == INP_TN
    n = h.shape[0]
    ndt = 2 * SSD_HEADS
    return pl.pallas_call(
        _inproj_kernel,
        grid=(n // INP_TM, U_TOTAL // INP_TN),
        in_specs=[
            pl.BlockSpec((INP_TM, D_MODEL), lambda i, j: (i, 0)),
            pl.BlockSpec((D_MODEL, INP_TN), lambda i, j: (0, j)),
            pl.BlockSpec((D_MODEL, ndt), lambda i, j: (0, 0)),
            pl.BlockSpec((1, ndt), lambda i, j: (0, 0)),
            pl.BlockSpec((1, ATT_HEAD_DIM), lambda i, j: (0, 0)),
            pl.BlockSpec((1, ATT_HEAD_DIM), lambda i, j: (0, 0)),
        ],
        out_specs=[
            pl.BlockSpec((INP_TM, INP_TN), lambda i, j: (i, j)),
            pl.BlockSpec((INP_TM, ndt), lambda i, j: (i, 0)),
        ],
        out_shape=[jax.ShapeDtypeStruct((n, U_TOTAL), BF16), jax.ShapeDtypeStruct((n, ndt), F32)],
        compiler_params=_params(("parallel", "arbitrary")),
        name="in_proj",
    )(h, w_u, w_dt, dt_bias, q_gain, k_gain)


CONV_TS = 1024
CONV_TC = 2048
CONV_HALO = 16
CONV_TB = 64
CONV_WIN = CONV_TB + 2 * CONV_HALO
CONV_SIDE_TAPS = tuple(k for k in range(CONV_K) if k != CONV_K // 2)


def _conv_kernel(xm_ref, xp_ref, xn_ref, w_ref, b_ref, o_ref, xs_scr, pw_scr, *, tiles_per_seq):
    i = pl.program_id(0)
    first = (i % tiles_per_seq) == 0
    last = ((i + 1) % tiles_per_seq) == 0
    H = CONV_HALO
    zero_halo = jnp.zeros((H, CONV_TC), BF16)
    xs_scr[0:H, :] = jnp.where(first, zero_halo, xp_ref[...])
    xs_scr[H:H + CONV_TS, :] = xm_ref[...]
    xs_scr[H + CONV_TS:, :] = jnp.where(last, zero_halo, xn_ref[...])
    xs = xs_scr[...]
    for n, k in enumerate(CONV_SIDE_TAPS):
        pw_scr[n] = xs * (0.5 * w_ref[k:k + 1, :]).astype(BF16)
    r = lax.broadcasted_iota(jnp.int32, (CONV_TB, len(CONV_SIDE_TAPS) * CONV_WIN), 0)
    c = lax.broadcasted_iota(jnp.int32, (CONV_TB, len(CONV_SIDE_TAPS) * CONV_WIN), 1)
    hit = None
    for n, k in enumerate(CONV_SIDE_TAPS):
        sel = c == r + (n * CONV_WIN + H + k - CONV_K // 2)
        hit = sel if hit is None else hit | sel
    shift = jnp.where(hit, 1.0, 0.0).astype(BF16)
    w_mid = 0.5 * w_ref[CONV_K // 2:CONV_K // 2 + 1, :]
    bias = 0.5 * b_ref[...]
    for blk in range(CONV_TS // CONV_TB):
        t0 = blk * CONV_TB
        taps = jnp.concatenate([pw_scr[n, t0:t0 + CONV_WIN, :] for n in range(len(CONV_SIDE_TAPS))], axis=0)
        mid = xs_scr[H + t0:H + t0 + CONV_TB, :].astype(F32) * w_mid + bias
        half = _dot(shift, taps) + mid
        o_ref[t0:t0 + CONV_TB, :] = (half + half * jnp.tanh(half)).astype(BF16)


def _conv(u, conv_w, conv_b, seq):
    n = u.shape[0]
    tiles_per_seq = seq // CONV_TS
    halo_per_tile = CONV_TS // CONV_HALO
    n_halo = n // CONV_HALO
    col0 = U_XBC // CONV_TC
    return pl.pallas_call(
        functools.partial(_conv_kernel, tiles_per_seq=tiles_per_seq),
        grid=(n // CONV_TS, CONV_DIM // CONV_TC),
        in_specs=[
            pl.BlockSpec((CONV_TS, CONV_TC), lambda i, j: (i, col0 + j)),
            pl.BlockSpec((CONV_HALO, CONV_TC),
                         lambda i, j: (jnp.maximum(i * halo_per_tile - 1, 0), col0 + j)),
            pl.BlockSpec((CONV_HALO, CONV_TC),
                         lambda i, j: (jnp.minimum((i + 1) * halo_per_tile, n_halo - 1), col0 + j)),
            pl.BlockSpec((CONV_K, CONV_TC), lambda i, j: (0, j)),
            pl.BlockSpec((1, CONV_TC), lambda i, j: (0, j)),
        ],
        out_specs=pl.BlockSpec((CONV_TS, CONV_TC), lambda i, j: (i, j)),
        out_shape=jax.ShapeDtypeStruct((n, CONV_DIM), BF16),
        scratch_shapes=[pltpu.VMEM((CONV_TS + 2 * CONV_HALO, CONV_TC), BF16),
                        pltpu.VMEM((len(CONV_SIDE_TAPS), CONV_TS + 2 * CONV_HALO, CONV_TC), BF16)],
        compiler_params=_params(("parallel", "parallel")),
        name="conv",
    )(u, u, u, conv_w, conv_b)


def _split3(a):
    hi = a.astype(BF16)
    r1 = a - hi.astype(F32)
    mid = r1.astype(BF16)
    lo = (r1 - mid.astype(F32)).astype(BF16)
    return hi, mid, lo


def _exact_lhs01(m01, a):
    hi, mid, lo = _split3(a)
    return _dot(jnp.concatenate([m01, m01, m01], axis=1), jnp.concatenate([hi, mid, lo], axis=0))


def _split2(a):
    hi = a.astype(BF16)
    mid = (a - hi.astype(F32)).astype(BF16)
    return jnp.concatenate([hi, mid], axis=1)


PREP_CHUNKS = 8
ROWS_PER_GROUP = 4 * SSD_HPG
LOG2E = 1.4426950408889634


def _ssd_prep_kernel(dt_ref, a_ref, cum_ref, wdec_ref, ecum_ref, rows_ref):
    L = CHUNK
    H = SSD_HPG
    row = lax.broadcasted_iota(jnp.int32, (L, L), 0)
    col = lax.broadcasted_iota(jnp.int32, (L, L), 1)
    lower = (row >= col).astype(BF16)
    upper = (row <= col).astype(BF16)
    is_fwd = col < SSD_HEADS
    a = -jnp.exp(a_ref[...])
    for t in range(PREP_CHUNKS):
        sl = slice(t * L, (t + 1) * L)
        dt = dt_ref[sl, :]
        dA = dt * a
        pre = _exact_lhs01(lower, dA)
        suf = _exact_lhs01(upper, dA)
        cum = jnp.where(is_fwd, pre, suf)
        tot = jnp.where(is_fwd[:1], pre[L - 1:L, :], suf[0:1, :])
        cum_ref[sl, :] = cum * LOG2E
        wdec_ref[sl, :] = _split2(jnp.exp(tot - cum) * dt)
        ecum_ref[sl, :] = _split2(jnp.exp(cum))
        dt_t = dt.T
        src_t = (cum.T - jnp.log(dt_t)) * LOG2E
        lds_t = jnp.log2(dt_t[:SSD_HEADS] + dt_t[SSD_HEADS:])
        etot = jnp.broadcast_to(jnp.exp(tot), (H, L))
        for g in range(SSD_GROUPS):
            f = slice(g * H, (g + 1) * H)
            b = slice(SSD_HEADS + g * H, SSD_HEADS + (g + 1) * H)
            rows_ref[t, g] = jnp.concatenate([src_t[f], src_t[b], lds_t[f], etot], axis=0)


def _ssd_prep(dt, a_log):
    n = dt.shape[0]
    rows_blk = PREP_CHUNKS * CHUNK
    nh = 2 * SSD_HEADS
    return pl.pallas_call(
        _ssd_prep_kernel,
        grid=(n // rows_blk,),
        in_specs=[pl.BlockSpec((rows_blk, nh), lambda i: (i, 0)), pl.BlockSpec((1, nh), lambda i: (0, 0))],
        out_specs=[pl.BlockSpec((rows_blk, nh), lambda i: (i, 0)),
                   pl.BlockSpec((rows_blk, 2 * nh), lambda i: (i, 0)),
                   pl.BlockSpec((rows_blk, 2 * nh), lambda i: (i, 0)),
                   pl.BlockSpec((PREP_CHUNKS, SSD_GROUPS, ROWS_PER_GROUP, CHUNK), lambda i: (i, 0, 0, 0))],
        out_shape=[jax.ShapeDtypeStruct((n, nh), F32),
                   jax.ShapeDtypeStruct((n, 2 * nh), BF16),
                   jax.ShapeDtypeStruct((n, 2 * nh), BF16),
                   jax.ShapeDtypeStruct((n // CHUNK, SSD_GROUPS, ROWS_PER_GROUP, CHUNK), F32)],
        compiler_params=_params(("parallel",)),
        name="ssd_prep",
    )(dt, a_log)


SSD_T = 16


def _ssd_kernel(x_ref, b_ref, c_ref, z_ref, cum_ref, wdec_ref, ecum_ref, rows_ref, d_ref, e_ref,
                y_ref, gst_scr, h_scr):
    g = pl.program_id(1)
    phase = pl.program_id(2)
    step = pl.program_id(3)
    ns = pl.num_programs(3)
    L = CHUNK
    GX = GROUP_X
    H = SSD_HPG

    @pl.when(step == 0)
    def _():
        h_scr[...] = jnp.zeros_like(h_scr)

    def state_update(t, direction, xb, bm):
        sl = slice(t * L, (t + 1) * L)
        e_dir = e_ref[0, :, direction * GX:(direction + 1) * GX]
        w = _dot(wdec_ref[sl, :], e_dir)
        dec = _dot(_split2(rows_ref[t, 0, 3 * H:4 * H, :]), e_dir)[0:1, :]
        xdec = w.astype(BF16) * xb
        b_t = bm.astype(F32).T.astype(BF16)
        h_scr[...] = dec * h_scr[...] + _dot(b_t, xdec)

    @pl.when(phase == 0)
    def _():
        for t in reversed(range(SSD_T)):
            sl = slice(t * L, (t + 1) * L)
            gst_scr[(ns - 1 - step) * SSD_T + t] = h_scr[...].astype(BF16)
            state_update(t, 1, x_ref[sl, :], b_ref[sl, :])

    @pl.when(phase == 1)
    def _():
        row = lax.broadcasted_iota(jnp.int32, (L, L), 0)
        col = lax.broadcasted_iota(jnp.int32, (L, L), 1)
        lower = row >= col
        diag = row == col
        low_lane = col < SSD_HEADDIM
        shift = (2 * SSD_HEADS - H * g) % (2 * SSD_HEADS)
        for t in range(SSD_T):
            sl = slice(t * L, (t + 1) * L)
            cm = c_ref[sl, :]
            bm = b_ref[sl, :]
            xb = x_ref[sl, :]
            xf = xb.astype(F32)
            cb = lax.dot_general(cm, bm, (((1,), (1,)), ((), ())), preferred_element_type=F32)
            cum_r = pltpu.roll(cum_ref[sl, :], shift, axis=1)
            pairs = []
            for k in range(H // 2):
                ms = []
                for half in range(2):
                    hh = 2 * k + half
                    fwd = cum_r[:, hh:hh + 1] - rows_ref[t, 0, hh:hh + 1, :]
                    bwd = cum_r[:, SSD_HEADS + hh:SSD_HEADS + hh + 1] - rows_ref[t, 0, H + hh:H + hh + 1, :]
                    arg = jnp.where(diag, rows_ref[t, 0, 2 * H + hh:2 * H + hh + 1, :],
                                    jnp.where(lower, fwd, bwd))
                    ms.append((jnp.exp2(arg) * cb).astype(BF16))
                xp = xb[:, k * L:(k + 1) * L]
                zero = jnp.zeros_like(xp)
                rhs = jnp.concatenate([jnp.where(low_lane, xp, zero), jnp.where(low_lane, zero, xp)], axis=0)
                pairs.append(_dot(jnp.concatenate(ms, axis=1), rhs))
            y = jnp.concatenate(pairs, axis=1)
            c = step * SSD_T + t
            states = jnp.concatenate([h_scr[...].astype(BF16), gst_scr[c]], axis=1)
            y_off = _dot(ecum_ref[sl, :], e_ref[0]) * _dot(cm, states)
            y = y + y_off[:, :GX] + y_off[:, GX:] + d_ref[...] * xf
            y_ref[sl, :] = (y * _silu(z_ref[sl, :].astype(F32))).astype(BF16)
            state_update(t, 0, xb, bm)


def _ssd(xc, u, cum, wdec, ecum, rows, d_exp, e01, batch, seq):
    n = xc.shape[0]
    nc = seq // CHUNK
    ns = nc // SSD_T
    tl = SSD_T * CHUNK
    nh = 2 * SSD_HEADS
    gx_blocks = D_INNER // D_STATE

    def fb_row(b, p, s):
        return b * ns + jnp.where(p == 0, ns - 1 - s, s)

    def f_row(b, p, s):
        return b * ns + jnp.where(p == 0, 0, s)

    return pl.pallas_call(
        _ssd_kernel,
        grid=(batch, SSD_GROUPS, 2, ns),
        in_specs=[
            pl.BlockSpec((tl, GROUP_X), lambda b, g, p, s: (fb_row(b, p, s), g)),
            pl.BlockSpec((tl, D_STATE), lambda b, g, p, s: (fb_row(b, p, s), gx_blocks + g)),
            pl.BlockSpec((tl, D_STATE), lambda b, g, p, s: (f_row(b, p, s), gx_blocks + SSD_GROUPS + g)),
            pl.BlockSpec((tl, GROUP_X), lambda b, g, p, s: (f_row(b, p, s), U_Z // GROUP_X + g)),
            pl.BlockSpec((tl, nh), lambda b, g, p, s: (f_row(b, p, s), 0)),
            pl.BlockSpec((tl, 2 * nh), lambda b, g, p, s: (fb_row(b, p, s), 0)),
            pl.BlockSpec((tl, 2 * nh), lambda b, g, p, s: (f_row(b, p, s), 0)),
            pl.BlockSpec((SSD_T, 1, ROWS_PER_GROUP, CHUNK), lambda b, g, p, s: (fb_row(b, p, s), g, 0, 0)),
            pl.BlockSpec((1, GROUP_X), lambda b, g, p, s: (0, g)),
            pl.BlockSpec((1, 2 * nh, 2 * GROUP_X), lambda b, g, p, s: (g, 0, 0)),
        ],
        out_specs=pl.BlockSpec((tl, GROUP_X), lambda b, g, p, s: (f_row(b, p, s), g)),
        out_shape=jax.ShapeDtypeStruct((n, D_INNER), BF16),
        scratch_shapes=[
            pltpu.VMEM((nc, D_STATE, GROUP_X), BF16),
            pltpu.VMEM((D_STATE, GROUP_X), F32),
        ],
        compiler_params=_params(("arbitrary", "arbitrary", "arbitrary", "arbitrary")),
        name="ssd",
    )(xc, xc, xc, u, cum, wdec, ecum, rows, d_exp, e01)


def _ssd_expand_matrix():
    e = np.zeros((SSD_GROUPS, 2 * SSD_HEADS, 2 * GROUP_X), np.float32)
    for g in range(SSD_GROUPS):
        for d in range(2):
            for r in range(SSD_HPG):
                lo = d * GROUP_X + r * SSD_HEADDIM
                e[g, d * SSD_HEADS + g * SSD_HPG + r, lo:lo + SSD_HEADDIM] = 1.0
    return np.concatenate([e, e], axis=1)


def _t5_bucket(rel):
    nb = N_BUCKETS // 2
    ret = (rel > 0).astype(np.int32) * nb
    n = np.abs(rel)
    max_exact = nb // 2
    large = max_exact + (np.log(np.maximum(n, 1) / max_exact) / math.log(MAX_DIST / max_exact)
                         * (nb - max_exact)).astype(np.int32)
    large = np.minimum(large, nb - 1)
    return (ret + np.where(n < max_exact, n, large)).astype(np.int32)


def _bucket_table():
    i = np.arange(ATT_BLOCK)[:, None]
    j = np.arange(3 * ATT_BLOCK)[None, :]
    rel = j - ATT_BLOCK - i
    return np.where(np.abs(rel) <= WINDOW, _t5_bucket(rel), -1).astype(np.int32)


N_EDGE_VARIANTS = 4


def _bias_kernel(rel_ref, bucket_ref, o_ref):
    variant = pl.program_id(0)
    bucket = bucket_ref[...]
    kcol = lax.broadcasted_iota(jnp.int32, bucket.shape, 1)
    dead = (((variant & 1) == 1) & (kcol < ATT_BLOCK)) | (((variant & 2) == 2) & (kcol >= 2 * ATT_BLOCK))
    for h in range(ATT_Q_HEADS):
        acc = jnp.full(bucket.shape, NEG_BIG, F32)
        for b in range(N_BUCKETS):
            acc = jnp.where(bucket == b, rel_ref[b, h] * LOG2E, acc)
        o_ref[0, h] = jnp.where(dead, NEG_BIG, acc)


def _bias_table(rel_bias):
    bucket = jnp.asarray(_bucket_table())
    return pl.pallas_call(
        _bias_kernel,
        grid=(N_EDGE_VARIANTS,),
        in_specs=[
            pl.BlockSpec(memory_space=pltpu.SMEM),
            pl.BlockSpec((ATT_BLOCK, 3 * ATT_BLOCK), lambda v: (0, 0)),
        ],
        out_specs=pl.BlockSpec((1, ATT_Q_HEADS, ATT_BLOCK, 3 * ATT_BLOCK), lambda v: (v, 0, 0, 0)),
        out_shape=jax.ShapeDtypeStruct((N_EDGE_VARIANTS, ATT_Q_HEADS, ATT_BLOCK, 3 * ATT_BLOCK), F32),
        compiler_params=_params(("arbitrary",)),
        name="rel_bias_table",
    )(rel_bias, bucket)


def _attn_kv_head(g, q_ref, kp_ref, kc_ref, kn_ref, vp_ref, vc_ref, vn_ref, bias_ref, sink_ref, store):
    T = ATT_BLOCK
    hd = ATT_HEAD_DIM
    ones = jnp.ones((3 * T, hd), BF16)
    ks = slice(g * hd, (g + 1) * hd)
    kwin = jnp.concatenate([kp_ref[:, ks], kc_ref[:, ks], kn_ref[:, ks]], axis=0)
    vwin = jnp.concatenate([vp_ref[:, ks], vc_ref[:, ks], vn_ref[:, ks]], axis=0)
    qs = jnp.concatenate(
        [q_ref[:, (g * ATT_REP + r) * hd:(g * ATT_REP + r + 1) * hd] for r in range(ATT_REP)],
        axis=0)
    s = lax.dot_general(qs, kwin, (((1,), (1,)), ((), ())), preferred_element_type=F32)
    s = s + jnp.concatenate([bias_ref[0, g * ATT_REP + r] for r in range(ATT_REP)], axis=0)
    sink = jnp.concatenate(
        [jnp.full((T, 1), sink_ref[g * ATT_REP + r] * LOG2E, F32) for r in range(ATT_REP)], axis=0)
    m = jnp.maximum(jnp.max(s, axis=-1, keepdims=True), sink)
    p = jnp.exp2(s - m).astype(BF16)
    ov = _dot(p, jnp.concatenate([vwin, ones], axis=1))
    o = ov[:, :hd] / (ov[:, hd:] + jnp.exp2(sink - m))
    for r in range(ATT_REP):
        store(g * ATT_REP + r, o[r * T:(r + 1) * T, :].astype(BF16))


BR_TM = 512
BR_TN = 512
BR_STEPS = D_MODEL // BR_TN
assert BR_TM == BR_STEPS * ATT_BLOCK


def _attn_branch_kernel(q_ref, kp_ref, kc_ref, kn_ref, vp_ref, vc_ref, vn_ref, bias_ref, sink_ref,
                        ys_ref, ga_ref, gb_ref, ng_ref, ws_ref, wa_ref, o_ref, ya_even, ya_odd):
    i = pl.program_id(0)
    j = pl.program_id(1)
    hd = ATT_HEAD_DIM

    @pl.when((i == 0) & (j == 0))
    def _():
        ya_odd[...] = jnp.zeros_like(ya_odd)

    def step(fill_scr, use_scr):
        row0 = pl.multiple_of(j * ATT_BLOCK, ATT_BLOCK)

        def store(head, val):
            fill_scr[pl.ds(row0, ATT_BLOCK), head * hd:(head + 1) * hd] = val

        n_slices = ATT_KV_HEADS
        ka = D_INNER // n_slices
        kb = D_MODEL // n_slices
        acc = [None, None, None]

        def matmul_slice(c):
            yg = ys_ref[:, c * ka:(c + 1) * ka]
            yf = yg.astype(F32)
            pq = jnp.sum(yf * yf, axis=-1, keepdims=True)
            pa = _dot(yg * ng_ref[:, c * ka:(c + 1) * ka].astype(BF16), ws_ref[c * ka:(c + 1) * ka, :])
            pb = _dot(use_scr[:, c * kb:(c + 1) * kb], wa_ref[c * kb:(c + 1) * kb, :])
            acc[:] = (pa, pb, pq) if acc[0] is None else (acc[0] + pa, acc[1] + pb, acc[2] + pq)

        for g in range(ATT_KV_HEADS):
            _attn_kv_head(g, q_ref, kp_ref, kc_ref, kn_ref, vp_ref, vc_ref, vn_ref, bias_ref, sink_ref, store)
            matmul_slice(g)
        a, b, ssq = acc
        scale = lax.rsqrt(ssq * (1.0 / D_INNER) + EPS)
        a = jax.nn.sigmoid(ga_ref[...].astype(F32)) * (scale * a)
        b = jax.nn.sigmoid(gb_ref[...].astype(F32)) * b
        o_ref[...] = (a + b).astype(BF16)

    @pl.when(i % 2 == 0)
    def _():
        step(ya_even, ya_odd)

    @pl.when(i % 2 == 1)
    def _():
        step(ya_odd, ya_even)


def _attn_branch(ys, u, bias, sink, norm_gain, w_ssd, w_att, seq):
    n = ys.shape[0]
    nblk = n // ATT_BLOCK
    ntile = n // BR_TM
    blocks_per_seq = seq // ATT_BLOCK
    kcol = U_K // KV_DIM
    vcol = U_V // KV_DIM
    blk = lambda i, j: jnp.minimum(i * BR_STEPS + j, nblk - 1)
    prev = lambda i, j: jnp.maximum(blk(i, j) - 1, 0)
    nxt = lambda i, j: jnp.minimum(blk(i, j) + 1, nblk - 1)
    tile = lambda i: jnp.maximum(i - 1, 0)
    kv_spec = lambda rowf, c: pl.BlockSpec((ATT_BLOCK, KV_DIM), lambda i, j: (rowf(i, j), c))

    def edge_variant(i, j):
        b = blk(i, j)
        first = (b % blocks_per_seq == 0).astype(jnp.int32)
        last = ((b + 1) % blocks_per_seq == 0).astype(jnp.int32)
        return first + 2 * last

    return pl.pallas_call(
        _attn_branch_kernel,
        grid=(ntile + 1, BR_STEPS),
        in_specs=[
            pl.BlockSpec((ATT_BLOCK, D_MODEL), lambda i, j: (blk(i, j), U_Q // D_MODEL)),
            kv_spec(prev, kcol), kv_spec(blk, kcol), kv_spec(nxt, kcol),
            kv_spec(prev, vcol), kv_spec(blk, vcol), kv_spec(nxt, vcol),
            pl.BlockSpec((1, ATT_Q_HEADS, ATT_BLOCK, 3 * ATT_BLOCK), lambda i, j: (edge_variant(i, j), 0, 0, 0)),
            pl.BlockSpec(memory_space=pltpu.SMEM),
            pl.BlockSpec((BR_TM, D_INNER), lambda i, j: (tile(i), 0)),
            pl.BlockSpec((BR_TM, BR_TN), lambda i, j: (tile(i), U_GA // BR_TN + j)),
            pl.BlockSpec((BR_TM, BR_TN), lambda i, j: (tile(i), U_GB // BR_TN + j)),
            pl.BlockSpec((1, D_INNER), lambda i, j: (0, 0)),
            pl.BlockSpec((D_INNER, BR_TN), lambda i, j: (0, j)),
            pl.BlockSpec((D_MODEL, BR_TN), lambda i, j: (0, j)),
        ],
        out_specs=pl.BlockSpec((BR_TM, BR_TN), lambda i, j: (i, j)),
        out_shape=jax.ShapeDtypeStruct((n + BR_TM, D_MODEL), BF16),
        scratch_shapes=[pltpu.VMEM((BR_TM, D_MODEL), BF16), pltpu.VMEM((BR_TM, D_MODEL), BF16)],
        compiler_params=_params(("arbitrary", "arbitrary")),
        name="attn_branch",
    )(u, u, u, u, u, u, u, bias, sink, ys, u, u, norm_gain, w_ssd, w_att)


OUT_TM = 512


def _outproj_kernel(m_ref, w_ref, x_ref, o_ref):
    o_ref[...] = x_ref[...] + _dot(m_ref[...], w_ref[...])


def _outproj(merged, w_out, x1):
    assert OUT_TM == BR_TM
    n = x1.shape[0]
    return pl.pallas_call(
        _outproj_kernel,
        grid=(n // OUT_TM,),
        in_specs=[
            pl.BlockSpec((OUT_TM, D_MODEL), lambda i: (i + 1, 0)),
            pl.BlockSpec((D_MODEL, D_MODEL), lambda i: (0, 0)),
            pl.BlockSpec((OUT_TM, D_MODEL), lambda i: (i, 0)),
        ],
        out_specs=pl.BlockSpec((OUT_TM, D_MODEL), lambda i: (i, 0)),
        out_shape=jax.ShapeDtypeStruct((n, D_MODEL), F32),
        compiler_params=_params(("parallel",)),
        name="out_proj",
    )(merged, w_out, x1)


def _prepare(ffn1_norm, ffn1_w_gate, ffn1_w_up, ffn1_w_down, mix_norm, w_in, conv_w, conv_b,
             ssd_A_log, ssd_dt_bias, ssd_D, ssd_out_norm, q_norm, k_norm, attn_sink, rel_bias,
             w_branch_ssd, w_branch_attn, w_out, ffn2_norm, ffn2_w_gate, ffn2_w_up, ffn2_w_down,
             final_norm):
    l = 0
    w = w_in[l]
    o_z = 0
    o_xbc = o_z + D_INNER
    o_dt = o_xbc + CONV_DIM
    o_q = o_dt + 2 * SSD_HEADS
    o_k = o_q + D_MODEL
    o_v = o_k + KV_DIM
    o_ga = o_v + KV_DIM
    o_gb = o_ga + D_MODEL
    w_u = jnp.concatenate(
        [w[:, o_z:o_dt], w[:, o_q:o_k], w[:, o_ga:o_gb + D_MODEL], w[:, o_k:o_ga]], axis=1).astype(BF16)
    row = lambda v: v.reshape(1, -1).astype(F32)
    return dict(
        ffn1=(row(ffn1_norm[l]), ffn1_w_gate[l].astype(BF16), ffn1_w_up[l].astype(BF16),
              ffn1_w_down[l].astype(BF16), row(mix_norm[l])),
        ffn2=(row(ffn2_norm[l]), ffn2_w_gate[l].astype(BF16), ffn2_w_up[l].astype(BF16),
              ffn2_w_down[l].astype(BF16), row(final_norm[l])),
        w_u=w_u, w_dt=w[:, o_dt:o_q].astype(BF16), dt_bias=row(ssd_dt_bias[l]),
        conv_w=conv_w[l].astype(F32), conv_b=row(conv_b[l]),
        a_log=row(ssd_A_log[l]), d_exp=row(jnp.repeat(ssd_D[l], SSD_HEADDIM)),
        e01=jnp.asarray(_ssd_expand_matrix(), BF16), out_norm=row(ssd_out_norm[l]),
        q_gain=row(q_norm[l]), k_gain=row(k_norm[l]), sink=attn_sink[l].astype(F32),
        bias=_bias_table(rel_bias.astype(F32)),
        w_ssd=w_branch_ssd[l].astype(BF16), w_att=w_branch_attn[l].astype(BF16),
        w_out=w_out[l].astype(BF16),
    )


def _layer(x, p):
    batch, seq, _ = x.shape
    x2d = x.reshape(batch * seq, D_MODEL)
    x1, h = _ffn(x2d, *p["ffn1"], final=False)
    u, dt = _inproj(h, p["w_u"], p["w_dt"], p["dt_bias"], p["q_gain"], p["k_gain"])
    xc = _conv(u, p["conv_w"], p["conv_b"], seq)
    cum, wdec, ecum, rows = _ssd_prep(dt, p["a_log"])
    ys = _ssd(xc, u, cum, wdec, ecum, rows, p["d_exp"], p["e01"], batch, seq)
    merged = _attn_branch(ys, u, p["bias"], p["sink"], p["out_norm"], p["w_ssd"], p["w_att"], seq)
    x2 = _outproj(merged, p["w_out"], x1)
    (y,) = _ffn(x2, *p["ffn2"], final=True)
    return y.reshape(batch, seq, D_MODEL)


def kernel(x_prompt, x_sample, ffn1_norm, ffn1_w_gate, ffn1_w_up, ffn1_w_down, mix_norm, w_in, conv_w, conv_b, ssd_A_log, ssd_dt_bias, ssd_D, ssd_out_norm, q_norm, k_norm, attn_sink, rel_bias, w_branch_ssd, w_branch_attn, w_out, ffn2_norm, ffn2_w_gate, ffn2_w_up, ffn2_w_down, final_norm):
    p = _prepare(ffn1_norm, ffn1_w_gate, ffn1_w_up, ffn1_w_down, mix_norm, w_in, conv_w, conv_b,
                 ssd_A_log, ssd_dt_bias, ssd_D, ssd_out_norm, q_norm, k_norm, attn_sink, rel_bias,
                 w_branch_ssd, w_branch_attn, w_out, ffn2_norm, ffn2_w_gate, ffn2_w_up, ffn2_w_down,
                 final_norm)
    return (_layer(x_prompt, p), _layer(x_sample, p))
```
